```python
import math
import jax, jax.numpy as jnp
from jax import lax
import numpy as np

D_MODEL = 1024
BATCH = 8
SEQ = 2048
DEPTH = 1

PLE_DIM = 256
M_HEADS = 4
M_HEAD_DIM = 128
M_WIDTH = M_HEADS * M_HEAD_DIM
M_CONV = 4
M_CHUNK = 64
DA_HEADS = 4
DA_HEAD_DIM = 64
DA_V_DIM = 2 * DA_HEAD_DIM
DA_QK_WIDTH = DA_HEADS * 2 * DA_HEAD_DIM
DA_WIDTH = DA_HEADS * DA_V_DIM
Q_BLOCK = 128
REL_BUCKETS = 32
REL_MAX_DIST = 128
D_FF = 2816
FFN_CONV = 3
N_BRANCHES = 2
EPS = 1e-6
NEG_BIG = -1e30

SPLITS = (M_WIDTH, M_WIDTH, M_WIDTH, M_WIDTH, M_HEADS, M_HEADS,
          DA_QK_WIDTH, DA_QK_WIDTH, DA_WIDTH, N_BRANCHES * D_MODEL)
IN_COLS = sum(SPLITS)

kernel_name = "hybrid_mlstm_diffattn_convffn_block"


def rmsnorm(x, g):
    xf = x.astype(jnp.float32)
    y = xf * lax.rsqrt(jnp.mean(xf * xf, -1, keepdims=True) + EPS)
    return (y * g.astype(jnp.float32)).astype(x.dtype)


def head_rmsnorm(x, g):
    H, d = x.shape[-2], x.shape[-1]
    xf = x.astype(jnp.float32)
    y = xf * lax.rsqrt(jnp.mean(xf * xf, -1, keepdims=True) + EPS)
    return y * g.reshape(H, d).astype(jnp.float32)


def causal_dwconv(x, w, b):
    K, C = w.shape
    y = lax.conv_general_dilated(x, w[:, None, :].astype(x.dtype), window_strides=(1,),
                                 padding=[(K - 1, 0)],
                                 dimension_numbers=("NWC", "WIO", "NWC"),
                                 feature_group_count=C)
    return y + b.astype(x.dtype)


def t5_causal_bucket(q_pos, k_pos):
    n = jnp.maximum(q_pos[:, None] - k_pos[None, :], 0)
    max_exact = REL_BUCKETS // 2
    nf = jnp.maximum(n, 1).astype(jnp.float32)
    large = max_exact + (jnp.log(nf / max_exact) / math.log(REL_MAX_DIST / max_exact)
                         * (REL_BUCKETS - max_exact)).astype(jnp.int32)
    large = jnp.minimum(large, REL_BUCKETS - 1)
    return jnp.where(n < max_exact, n, large)


def mlstm_chunkwise(q, k, v, i_pre, f_pre):
    Bsz, H, S, d = q.shape
    L = M_CHUNK
    NC = S // L
    f32 = jnp.float32
    qc = q.astype(f32).reshape(Bsz, H, NC, L, d)
    kc = (k.astype(f32) * (d ** -0.5)).reshape(Bsz, H, NC, L, d)
    vc = v.astype(f32).reshape(Bsz, H, NC, L, d)
    logf = jax.nn.log_sigmoid(f_pre).reshape(Bsz, H, NC, L)
    ig = i_pre.reshape(Bsz, H, NC, L)
    b = jnp.cumsum(logf, -1)
    b_tot = b[..., -1]
    causal = jnp.tril(jnp.ones((L, L), bool))
    log_d = jnp.where(causal, b[..., :, None] - b[..., None, :] + ig[..., None, :], -jnp.inf)

    w_end = b_tot[..., None] - b + ig
    m_loc = jnp.max(w_end, -1)
    a = jnp.exp(w_end - m_loc[..., None])
    ak = a[..., None] * kc
    C_loc = jnp.einsum('bhcld,bhcle->bhcde', ak, vc)
    n_loc = jnp.sum(ak, axis=3)

    def step(carry, inp):
        C, n, m = carry
        C_l, n_l, m_l, bt = inp
        m_new = jnp.maximum(bt + m, m_l)
        s_old = jnp.exp(bt + m - m_new)
        s_loc = jnp.exp(m_l - m_new)
        C_new = s_old[..., None, None] * C + s_loc[..., None, None] * C_l
        n_new = s_old[..., None] * n + s_loc[..., None] * n_l
        return (C_new, n_new, m_new), (C, n, m)

    init = (jnp.zeros((Bsz, H, d, d), f32), jnp.zeros((Bsz, H, d), f32), jnp.zeros((Bsz, H), f32))
    xs = (jnp.moveaxis(C_loc, 2, 0), jnp.moveaxis(n_loc, 2, 0),
          jnp.moveaxis(m_loc, 2, 0), jnp.moveaxis(b_tot, 2, 0))
    _, (C_prev, n_prev, m_prev) = lax.scan(step, init, xs)
    C_prev = jnp.moveaxis(C_prev, 0, 2)
    n_prev = jnp.moveaxis(n_prev, 0, 2)
    m_prev = jnp.moveaxis(m_prev, 0, 2)

    log_inter = b + m_prev[..., None]
    m_t = jnp.maximum(log_inter, jnp.max(log_d, -1))
    s_inter = jnp.exp(log_inter - m_t)
    dmat = jnp.exp(log_d - m_t[..., None])
    sqk = jnp.einsum('bhcld,bhcsd->bhcls', qc, kc) * dmat
    num = (s_inter[..., None] * jnp.einsum('bhcld,bhcde->bhcle', qc, C_prev)
           + jnp.einsum('bhcls,bhcse->bhcle', sqk, vc))
    den = s_inter * jnp.einsum('bhcld,bhcd->bhcl', qc, n_prev) + jnp.sum(sqk, -1)
    h = num / jnp.maximum(jnp.abs(den), jnp.exp(-m_t))[..., None]
    return h.reshape(Bsz, H, S, d)


def diff_attention(q1, q2, k1, k2, v, lam, rel_bias):
    Bsz, H, S, dh = q1.shape
    NB = S // Q_BLOCK
    scale = DA_HEAD_DIM ** -0.5
    k_pos = jnp.arange(S)

    def to_blocks(t):
        return jnp.moveaxis(t.reshape(Bsz, H, NB, Q_BLOCK, dh), 2, 0)

    def block(args):
        qb1, qb2, q_start = args
        q_pos = q_start + jnp.arange(Q_BLOCK)
        bucket = t5_causal_bucket(q_pos, k_pos)
        bias = jnp.moveaxis(rel_bias[bucket], -1, 0).astype(jnp.float32)
        mask = q_pos[:, None] >= k_pos[None, :]

        def probs(qb, kk):
            s = jnp.einsum('bhqd,bhkd->bhqk', qb, kk, preferred_element_type=jnp.float32) * scale + bias
            return jax.nn.softmax(jnp.where(mask, s, NEG_BIG), axis=-1)

        a = probs(qb1, k1) - lam * probs(qb2, k2)
        return jnp.einsum('bhqk,bhkd->bhqd', a.astype(v.dtype), v)

    out = lax.map(block, (to_blocks(q1), to_blocks(q2), jnp.arange(NB) * Q_BLOCK))
    return jnp.moveaxis(out, 0, 2).reshape(Bsz, H, S, v.shape[-1])


def setup_inputs(seed: int = 0) -> dict:
    key = jax.random.key(seed)
    ks = jax.random.split(key, 24)
    f32 = jnp.float32
    nrm = lambda k, shape, s: jax.random.normal(k, shape, f32) * s
    b_i = nrm(ks[4], (DEPTH, M_HEADS), 0.1)
    b_f = jnp.broadcast_to(jnp.linspace(3.0, 6.0, M_HEADS, dtype=f32), (DEPTH, M_HEADS)) + nrm(ks[5], (DEPTH, M_HEADS), 0.05)
    return {
        "x": nrm(ks[0], (BATCH, SEQ, D_MODEL), 1.0),
        "p": nrm(ks[1], (DEPTH, BATCH, SEQ, PLE_DIM), 1.0),
        "rel_bias": nrm(ks[2], (REL_BUCKETS, DA_HEADS), 0.2),
        "norm_mix_g": 1.0 + nrm(ks[3], (DEPTH, D_MODEL), 0.02),
        "w_in": nrm(ks[6], (DEPTH, D_MODEL, IN_COLS), D_MODEL ** -0.5),
        "b_if": jnp.concatenate([b_i, b_f], axis=-1),
        "m_conv_w": nrm(ks[7], (DEPTH, M_CONV, 2 * M_WIDTH), M_CONV ** -0.5),
        "m_conv_b": nrm(ks[8], (DEPTH, 2 * M_WIDTH), 0.01),
        "m_norm_g": 1.0 + nrm(ks[9], (DEPTH, M_WIDTH), 0.02),
        "da_lambda": nrm(ks[10], (DEPTH, 4, DA_HEAD_DIM), 0.1),
        "da_norm_g": 1.0 + nrm(ks[11], (DEPTH, DA_WIDTH), 0.02),
        "w_br_m": nrm(ks[12], (DEPTH, M_WIDTH, D_MODEL), M_WIDTH ** -0.5),
        "w_br_d": nrm(ks[13], (DEPTH, DA_WIDTH, D_MODEL), DA_WIDTH ** -0.5),
        "w_out": nrm(ks[14], (DEPTH, D_MODEL, D_MODEL), D_MODEL ** -0.5),
        "norm_ffn_g": 1.0 + nrm(ks[15], (DEPTH, D_MODEL), 0.02),
        "w_up": nrm(ks[16], (DEPTH, D_MODEL, 2 * D_FF), D_MODEL ** -0.5),
        "ffn_conv_w": nrm(ks[17], (DEPTH, FFN_CONV, 2 * D_FF), FFN_CONV ** -0.5),
        "ffn_conv_b": nrm(ks[18], (DEPTH, 2 * D_FF), 0.01),
        "w_down": nrm(ks[19], (DEPTH, D_FF, D_MODEL), D_FF ** -0.5),
        "norm_ple_g": 1.0 + nrm(ks[20], (DEPTH, D_MODEL), 0.02),
        "w_ple_gate": nrm(ks[21], (DEPTH, D_MODEL, D_MODEL), D_MODEL ** -0.5),
        "w_ple": nrm(ks[22], (DEPTH, PLE_DIM, D_MODEL), PLE_DIM ** -0.5),
        "norm_final_g": 1.0 + nrm(ks[23], (D_MODEL,), 0.02),
    }


def reference(x, p, rel_bias, norm_mix_g, w_in, b_if, m_conv_w, m_conv_b, m_norm_g,
              da_lambda, da_norm_g, w_br_m, w_br_d, w_out, norm_ffn_g, w_up, ffn_conv_w,
              ffn_conv_b, w_down, norm_ple_g, w_ple_gate, w_ple, norm_final_g):
    Bsz, S, _ = x.shape
    f32 = jnp.float32
    split_idx = [int(c) for c in np.cumsum(SPLITS)[:-1]]

    def heads(t, H):
        return t.reshape(Bsz, S, H, -1).transpose(0, 2, 1, 3)

    for l in range(DEPTH):
        h = rmsnorm(x, norm_mix_g[l])
        proj = h @ w_in[l]
        mq, mk, mv, mo, mi, mf, dq, dk, dv, gates = jnp.split(proj, split_idx, axis=-1)

        qk = jax.nn.silu(causal_dwconv(jnp.concatenate([mq, mk], -1), m_conv_w[l], m_conv_b[l]))
        mq_c, mk_c = jnp.split(qk, 2, axis=-1)
        i_pre = (mi.astype(f32) + b_if[l, :M_HEADS].astype(f32)).transpose(0, 2, 1)
        f_pre = (mf.astype(f32) + b_if[l, M_HEADS:].astype(f32)).transpose(0, 2, 1)
        hm = mlstm_chunkwise(heads(mq_c, M_HEADS), heads(mk_c, M_HEADS), heads(mv, M_HEADS), i_pre, f_pre)
        hm = head_rmsnorm(hm.transpose(0, 2, 1, 3), m_norm_g[l]).reshape(Bsz, S, M_WIDTH).astype(x.dtype)
        ya = (jax.nn.sigmoid(mo) * hm) @ w_br_m[l]

        dq5 = dq.reshape(Bsz, S, DA_HEADS, 2, DA_HEAD_DIM)
        dk5 = dk.reshape(Bsz, S, DA_HEADS, 2, DA_HEAD_DIM)
        q1, q2 = dq5[..., 0, :].transpose(0, 2, 1, 3), dq5[..., 1, :].transpose(0, 2, 1, 3)
        k1, k2 = dk5[..., 0, :].transpose(0, 2, 1, 3), dk5[..., 1, :].transpose(0, 2, 1, 3)
        lam_init = 0.8 - 0.6 * math.exp(-0.3 * l)
        lv = da_lambda[l].astype(f32)
        lam = jnp.exp(jnp.sum(lv[0] * lv[1])) - jnp.exp(jnp.sum(lv[2] * lv[3])) + lam_init
        hd = diff_attention(q1, q2, k1, k2, heads(dv, DA_HEADS), lam, rel_bias)
        hd = head_rmsnorm(hd.transpose(0, 2, 1, 3), da_norm_g[l]) * (1.0 - lam_init)
        yb = hd.reshape(Bsz, S, DA_WIDTH).astype(x.dtype) @ w_br_d[l]

        g_a, g_b = jnp.split(gates, 2, axis=-1)
        mixed = jax.nn.sigmoid(g_a) * ya + jax.nn.sigmoid(g_b) * yb
        x = x + mixed @ w_out[l]

        h = rmsnorm(x, norm_ffn_g[l])
        u = causal_dwconv(h @ w_up[l], ffn_conv_w[l], ffn_conv_b[l])
        val, gt = jnp.split(u, 2, axis=-1)
        x = x + (jax.nn.gelu(gt) * val) @ w_down[l]

        hg = rmsnorm(x, norm_ple_g[l])
        x = x + jax.nn.sigmoid(hg @ w_ple_gate[l]) * (p[l].astype(x.dtype) @ w_ple[l])

    return rmsnorm(x, norm_final_g)
```

```python
import functools
import math

import jax
import jax.numpy as jnp
from jax import lax
from jax.experimental import pallas as pl
from jax.experimental.pallas import tpu as pltpu

D_MODEL = 1024
PLE_DIM = 256
M_HEADS = 4
M_HEAD_DIM = 128
M_WIDTH = M_HEADS * M_HEAD_DIM
M_CONV = 4
DA_HEADS = 4
DA_HEAD_DIM = 64
DA_V_DIM = 2 * DA_HEAD_DIM
DA_WIDTH = DA_HEADS * DA_V_DIM
REL_BUCKETS = 32
REL_MAX_DIST = 128
D_FF = 2816
FFN_CONV = 3
EPS = 1e-6
NEG_BIG = -1e30

LANES = 128
SUBLANES = 8
VMEM_LIMIT = 56 * 1024 * 1024

TOKEN_TILE = 512
M_CHUNK = 256
ATT_BLOCK = 256
FF_CHUNK = 256
N_FF_CHUNKS = D_FF // FF_CHUNK

F32 = jnp.float32
BF16 = jnp.bfloat16


def _dot(a, b):
    return jnp.dot(a, b, preferred_element_type=F32)


def _dot_nt(a, b):
    return lax.dot_general(a, b, (((1,), (1,)), ((), ())), preferred_element_type=F32)


def _dot_tn(a, b):
    return lax.dot_general(a, b, (((0,), (0,)), ((), ())), preferred_element_type=F32)


def _rms(x, g):
    return x * lax.rsqrt(jnp.mean(x * x, -1, keepdims=True) + EPS) * g


def _sigmoid(x):
    return 1.0 / (1.0 + jnp.exp(-x))


def _const_spec(shape):
    nd = len(shape)
    return pl.BlockSpec(shape, lambda *_: (0,) * nd, pipeline_mode=pl.Buffered(1))


def _inproj_kernel(x_ref, g_ref, w_ref, cw_ref, cb_ref, bif_ref,
                   mq_ref, mk_ref, mv_ref, mo_ref, dq_ref, dk_ref, dv_ref, gates_ref, ifg_ref,
                   ext_ref, *, tiles_per_seq):
    tm = x_ref.shape[0]
    i = pl.program_id(0)
    hb = _rms(x_ref[...], g_ref[...]).astype(BF16)

    @pl.when(i % tiles_per_seq == 0)
    def _():
        ext_ref[0:SUBLANES, :] = jnp.zeros((SUBLANES, 2 * M_WIDTH), F32)

    @pl.when(i % tiles_per_seq != 0)
    def _():
        ext_ref[0:SUBLANES, :] = ext_ref[tm:tm + SUBLANES, :]

    ext_ref[SUBLANES:SUBLANES + tm, :] = _dot(hb, w_ref[:, 0:2 * M_WIDTH])
    acc = cb_ref[...]
    for j in range(M_CONV):
        off = SUBLANES - (M_CONV - 1) + j
        acc = acc + cw_ref[j:j + 1, :] * ext_ref[off:off + tm, :]
    y = acc * _sigmoid(acc)
    mq_ref[...] = y[:, :M_WIDTH].astype(BF16)
    mk_ref[...] = (y[:, M_WIDTH:] * (M_HEAD_DIM ** -0.5)).astype(BF16)

    c = 2 * M_WIDTH
    mv_ref[...] = _dot(hb, w_ref[:, c:c + M_WIDTH]).astype(BF16); c += M_WIDTH
    mo_ref[...] = _dot(hb, w_ref[:, c:c + M_WIDTH]).astype(BF16); c += M_WIDTH
    dq_ref[...] = (_dot(hb, w_ref[:, c:c + DA_WIDTH]) * (DA_HEAD_DIM ** -0.5)).astype(BF16); c += DA_WIDTH
    dk_ref[...] = _dot(hb, w_ref[:, c:c + DA_WIDTH]).astype(BF16); c += DA_WIDTH
    dv_ref[...] = _dot(hb, w_ref[:, c:c + DA_WIDTH]).astype(BF16); c += DA_WIDTH
    for j in range(4):
        gates_ref[:, j * 512:(j + 1) * 512] = _dot(hb, w_ref[:, c:c + 512]).astype(BF16); c += 512
    ifg_ref[...] = _dot(hb, w_ref[:, c:c + LANES]) + bif_ref[...]


def _inproj(x2, g, w_all, cw, cb, bif, seq):
    T = x2.shape[0]
    tm = TOKEN_TILE
    n_cols = w_all.shape[1]
    row = lambda i: (i, 0)
    bf = lambda n: jax.ShapeDtypeStruct((T, n), BF16)
    out_shapes = [bf(M_WIDTH)] * 4 + [bf(DA_WIDTH)] * 3 + [bf(2 * D_MODEL), jax.ShapeDtypeStruct((T, LANES), F32)]
    out_specs = [pl.BlockSpec((tm, M_WIDTH), row)] * 7 + [pl.BlockSpec((tm, 2 * D_MODEL), row),
                                                          pl.BlockSpec((tm, LANES), row)]
    return pl.pallas_call(
        functools.partial(_inproj_kernel, tiles_per_seq=seq // tm),
        grid=(T // tm,),
        in_specs=[pl.BlockSpec((tm, D_MODEL), row), _const_spec((1, D_MODEL)), _const_spec((D_MODEL, n_cols)),
                  _const_spec((M_CONV, 2 * M_WIDTH)), _const_spec((1, 2 * M_WIDTH)), _const_spec((1, LANES))],
        out_specs=out_specs,
        out_shape=out_shapes,
        scratch_shapes=[pltpu.VMEM((tm + SUBLANES, 2 * M_WIDTH), F32)],
        compiler_params=pltpu.CompilerParams(dimension_semantics=("arbitrary",), vmem_limit_bytes=VMEM_LIMIT),
        name="inproj",
    )(x2, g, w_all, cw, cb, bif)


def _cumsum_rows(tril, x):
    hi = x.astype(BF16)
    r1 = x - hi.astype(F32)
    mid = r1.astype(BF16)
    lo = (r1 - mid.astype(F32)).astype(BF16)
    return _dot(tril, hi) + _dot(tril, mid) + _dot(tril, lo)


def _mlstm_kernel(mq_ref, mk_ref, mv_ref, mo_ref, ifg_ref, g_ref, out_ref, c_ref, n_ref, m_ref):
    L = mq_ref.shape[0]
    d = M_HEAD_DIM

    @pl.when(pl.program_id(1) == 0)
    def _():
        c_ref[...] = jnp.zeros(c_ref.shape, F32)
        n_ref[...] = jnp.zeros(n_ref.shape, F32)
        m_ref[...] = jnp.zeros(m_ref.shape, F32)

    ig = ifg_ref[...]
    fg = pltpu.roll(ig, LANES - M_HEADS, axis=1)
    logf = jnp.minimum(fg, 0.0) - jnp.log1p(jnp.exp(-jnp.abs(fg)))
    rows = lax.broadcasted_iota(jnp.int32, (L, L), 0)
    cols = lax.broadcasted_iota(jnp.int32, (L, L), 1)
    causal = rows >= cols
    b = _cumsum_rows(jnp.where(causal, 1.0, 0.0).astype(BF16), logf)
    b_tot = b[L - 1:L, :]
    u = ig - b
    w_end = b_tot + u
    m_loc = jnp.max(w_end, axis=0, keepdims=True)
    a = jnp.exp(w_end - m_loc)
    m_prev = m_ref[...]
    log_inter = b + m_prev
    m_new = jnp.maximum(b_tot + m_prev, m_loc)
    s_old = jnp.exp(b_tot + m_prev - m_new)
    s_loc = jnp.exp(m_loc - m_new)
    u_t = u.T

    for h in range(M_HEADS):
        hs = slice(h * d, (h + 1) * d)
        q = mq_ref[:, hs]
        k = mk_ref[:, hs]
        v = mv_ref[:, hs]
        log_d = jnp.where(causal, b[:, h:h + 1] + u_t[h:h + 1, :], -jnp.inf)
        li = log_inter[:, h:h + 1]
        m_t = jnp.maximum(li, jnp.max(log_d, axis=1, keepdims=True))
        dmat = jnp.exp(log_d - m_t)
        s_inter = jnp.exp(li - m_t)
        sqk = _dot_nt(q, k) * dmat
        c_prev = c_ref[h]
        n_prev = n_ref[h:h + 1, :]
        num = s_inter * _dot(q, c_prev.astype(BF16)) + _dot(sqk.astype(BF16), v)
        den = (s_inter * jnp.sum(q.astype(F32) * n_prev, axis=1, keepdims=True)
               + jnp.sum(sqk, axis=1, keepdims=True))
        hh = num / jnp.maximum(jnp.abs(den), jnp.exp(-m_t))
        y = _rms(hh, g_ref[:, hs])
        out_ref[:, hs] = (_sigmoid(mo_ref[:, hs].astype(F32)) * y).astype(BF16)

        ak = a[:, h:h + 1] * k.astype(F32)
        c_loc = _dot_tn(ak.astype(BF16), v)
        n_loc = jnp.sum(ak, axis=0, keepdims=True)
        so = s_old[:, h:h + 1]
        sl = s_loc[:, h:h + 1]
        c_ref[h] = so * c_prev + sl * c_loc
        n_ref[h:h + 1, :] = so * n_prev + sl * n_loc

    m_ref[...] = m_new


def _mlstm(mq, mk, mv, mo, ifg, g, batch, seq):
    T = mq.shape[0]
    L = M_CHUNK
    nc = seq // L
    row = lambda b, c: (b * nc + c, 0)
    blk = pl.BlockSpec((L, M_WIDTH), row)
    return pl.pallas_call(
        _mlstm_kernel,
        grid=(batch, nc),
        in_specs=[blk, blk, blk, blk, pl.BlockSpec((L, LANES), row), _const_spec((1, M_WIDTH))],
        out_specs=blk,
        out_shape=jax.ShapeDtypeStruct((T, M_WIDTH), BF16),
        scratch_shapes=[pltpu.VMEM((M_HEADS, M_HEAD_DIM, M_HEAD_DIM), F32),
                        pltpu.VMEM((SUBLANES, M_HEAD_DIM), F32),
                        pltpu.VMEM((1, LANES), F32)],
        compiler_params=pltpu.CompilerParams(dimension_semantics=("arbitrary", "arbitrary"),
                                             vmem_limit_bytes=VMEM_LIMIT),
        name="mlstm",
    )(mq, mk, mv, mo, ifg, g)


def _t5_bucket(n):
    max_exact = REL_BUCKETS // 2
    nf = jnp.maximum(n, 1).astype(F32)
    large = max_exact + (jnp.log(nf / max_exact) / math.log(REL_MAX_DIST / max_exact)
                         * (REL_BUCKETS - max_exact)).astype(jnp.int32)
    large = jnp.minimum(large, REL_BUCKETS - 1)
    return jnp.where(n < max_exact, n, large)


def _attn_kernel(rb_ref, lam_ref, q_ref, k_ref, v_ref, g_ref, out_ref,
                 bias_ref, m_ref, l_ref, acc_ref, *, lam_init):
    tq = q_ref.shape[0]
    tk = tq
    h = pl.program_id(1)
    qi = pl.program_id(2)

    kk = lax.broadcasted_iota(jnp.int32, (tk, tq), 0)
    qq = lax.broadcasted_iota(jnp.int32, (tk, tq), 1)

    @pl.when((pl.program_id(0) == 0) & (h == 0) & (qi == 0))
    def _():
        for delta in range(2):
            bucket = _t5_bucket(jnp.maximum(qq + delta * tq - kk, 0))
            for hh in range(DA_HEADS):
                t = jnp.zeros((tk, tq), F32)
                for j in range(REL_BUCKETS):
                    t = jnp.where(bucket == j, rb_ref[j, hh], t)
                bias_ref[hh, delta] = t

    lv = lam_ref[...]
    lam = (jnp.exp(jnp.sum(lv[0:1] * lv[1:2], axis=1, keepdims=True))
           - jnp.exp(jnp.sum(lv[2:3] * lv[3:4], axis=1, keepdims=True)) + lam_init)

    q = q_ref[...]
    lane = lax.broadcasted_iota(jnp.int32, q.shape, 1)
    zero = jnp.zeros_like(q)
    qcat = jnp.concatenate([jnp.where(lane < DA_HEAD_DIM, q, zero),
                            jnp.where(lane >= DA_HEAD_DIM, q, zero)], axis=0)

    m_ref[...] = jnp.full(m_ref.shape, -jnp.inf, F32)
    l_ref[...] = jnp.zeros(l_ref.shape, F32)
    acc_ref[...] = jnp.zeros(acc_ref.shape, F32)

    def update(j, add_bias):
        start = pl.multiple_of(j * tk, tk)
        k = k_ref[pl.ds(start, tk), :]
        v = v_ref[pl.ds(start, tk), :]
        s = add_bias(_dot_nt(k, qcat))
        m_prev = m_ref[...]
        m_new = jnp.maximum(m_prev, jnp.max(s, axis=0, keepdims=True))
        alpha = jnp.exp(m_prev - m_new)
        p = jnp.exp(s - m_new)
        l_ref[...] = alpha * l_ref[...] + jnp.sum(p, axis=0, keepdims=True)
        acc_ref[...] = alpha * acc_ref[...] + _dot_tn(v, p.astype(BF16))
        m_ref[...] = m_new

    far = rb_ref[REL_BUCKETS - 1, h]

    def far_body(j, carry):
        update(j, lambda s: s + far)
        return carry

    lax.fori_loop(0, qi - 1, far_body, 0)

    @pl.when(qi > 0)
    def _():
        t = bias_ref[h, 1]
        update(qi - 1, lambda s: s + jnp.concatenate([t, t], axis=1))

    t0 = bias_ref[h, 0]
    mask2 = jnp.concatenate([qq >= kk, qq >= kk], axis=1)
    update(qi, lambda s: jnp.where(mask2, s + jnp.concatenate([t0, t0], axis=1), NEG_BIG))

    o = acc_ref[...] / l_ref[...]
    out = (o[:, :tq] - lam * o[:, tq:]).T
    out_ref[...] = (_rms(out, g_ref[...]) * (1.0 - lam_init)).astype(BF16)


def _attention(rel_bias, lam_vec, dq, dk, dv, g, batch, seq, lam_init):
    T = dq.shape[0]
    tq = ATT_BLOCK
    nq = seq // tq
    return pl.pallas_call(
        functools.partial(_attn_kernel, lam_init=lam_init),
        grid=(batch, DA_HEADS, nq),
        in_specs=[pl.BlockSpec(memory_space=pltpu.SMEM),
                  pl.BlockSpec((4, DA_HEAD_DIM), lambda b, h, i: (0, 0)),
                  pl.BlockSpec((tq, DA_V_DIM), lambda b, h, i: (b * nq + i, h)),
                  pl.BlockSpec((seq, DA_V_DIM), lambda b, h, i: (b, h)),
                  pl.BlockSpec((seq, DA_V_DIM), lambda b, h, i: (b, h)),
                  pl.BlockSpec((1, DA_V_DIM), lambda b, h, i: (0, h))],
        out_specs=pl.BlockSpec((tq, DA_V_DIM), lambda b, h, i: (b * nq + i, h)),
        out_shape=jax.ShapeDtypeStruct((T, DA_WIDTH), BF16),
        scratch_shapes=[pltpu.VMEM((DA_HEADS, 2, tq, tq), F32),
                        pltpu.VMEM((1, 2 * tq), F32),
                        pltpu.VMEM((1, 2 * tq), F32),
                        pltpu.VMEM((DA_V_DIM, 2 * tq), F32)],
        compiler_params=pltpu.CompilerParams(dimension_semantics=("arbitrary",) * 3,
                                             vmem_limit_bytes=VMEM_LIMIT),
        name="diffattn",
    )(rel_bias, lam_vec, dq, dk, dv, g)


def _merge_kernel(gm_ref, hd_ref, gates_ref, x_ref, wm_ref, wd_ref, wo_ref, out_ref):
    ya = _dot(gm_ref[...], wm_ref[...])
    yb = _dot(hd_ref[...], wd_ref[...])
    ga = _sigmoid(gates_ref[:, :D_MODEL].astype(F32))
    gb = _sigmoid(gates_ref[:, D_MODEL:].astype(F32))
    mixed = (ga * ya + gb * yb).astype(BF16)
    out_ref[...] = x_ref[...] + _dot(mixed, wo_ref[...])


def _merge(gm, hd, gates, x2, wm, wd, wo):
    T = x2.shape[0]
    tm = TOKEN_TILE
    row = lambda i: (i, 0)
    return pl.pallas_call(
        _merge_kernel,
        grid=(T // tm,),
        in_specs=[pl.BlockSpec((tm, M_WIDTH), row), pl.BlockSpec((tm, DA_WIDTH), row),
                  pl.BlockSpec((tm, 2 * D_MODEL), row), pl.BlockSpec((tm, D_MODEL), row),
                  _const_spec((M_WIDTH, D_MODEL)), _const_spec((DA_WIDTH, D_MODEL)),
                  _const_spec((D_MODEL, D_MODEL))],
        out_specs=pl.BlockSpec((tm, D_MODEL), row),
        out_shape=jax.ShapeDtypeStruct((T, D_MODEL), F32),
        compiler_params=pltpu.CompilerParams(dimension_semantics=("arbitrary",), vmem_limit_bytes=VMEM_LIMIT),
        name="merge",
    )(gm, hd, gates, x2, wm, wd, wo)


def _gelu_tanh(x):
    return 0.5 * x * (1.0 + jnp.tanh(math.sqrt(2.0 / math.pi) * (x + 0.044715 * (x * x * x))))


def _ffn_kernel(x_ref, p_ref, gf_ref, wup_ref, cw_ref, cb_ref, wdn_ref, gp_ref, wpg_ref, wp_ref, gl_ref,
                out_ref, ext_ref, carry_ref, acc_ref, *, tiles_per_seq, final_norm):
    tm = x_ref.shape[0]
    i = pl.program_id(0)
    x1 = x_ref[...]
    hb = _rms(x1, gf_ref[...]).astype(BF16)

    @pl.when(i % tiles_per_seq == 0)
    def _():
        carry_ref[...] = jnp.zeros(carry_ref.shape, F32)

    for c in range(N_FF_CHUNKS):
        ext_ref[0:SUBLANES, :] = carry_ref[c]
        ext_ref[SUBLANES:SUBLANES + tm, :] = _dot(hb, wup_ref[c])
        conv = cb_ref[c]
        for j in range(FFN_CONV):
            off = SUBLANES - (FFN_CONV - 1) + j
            conv = conv + cw_ref[c, j:j + 1, :] * ext_ref[off:off + tm, :]
        carry_ref[c] = ext_ref[tm:tm + SUBLANES, :]
        act = (_gelu_tanh(conv[:, FF_CHUNK:]) * conv[:, :FF_CHUNK]).astype(BF16)
        d = _dot(act, wdn_ref[c])
        if c == 0:
            acc_ref[...] = d
        else:
            acc_ref[...] += d

    x2 = x1 + acc_ref[...]
    hg = _rms(x2, gp_ref[...]).astype(BF16)
    gate = _sigmoid(_dot(hg, wpg_ref[...]))
    pe = _dot(p_ref[...].astype(BF16), wp_ref[...])
    x3 = x2 + gate * pe
    out_ref[...] = _rms(x3, gl_ref[...]) if final_norm else x3


def _ffn(x1, p2, gf, wup, cw, cb, wdn, gp, wpg, wp, gl, seq, final_norm):
    T = x1.shape[0]
    tm = TOKEN_TILE
    row = lambda i: (i, 0)
    return pl.pallas_call(
        functools.partial(_ffn_kernel, tiles_per_seq=seq // tm, final_norm=final_norm),
        grid=(T // tm,),
        in_specs=[pl.BlockSpec((tm, D_MODEL), row), pl.BlockSpec((tm, PLE_DIM), row),
                  _const_spec((1, D_MODEL)),
                  _const_spec((N_FF_CHUNKS, D_MODEL, 2 * FF_CHUNK)),
                  _const_spec((N_FF_CHUNKS, FFN_CONV, 2 * FF_CHUNK)),
                  _const_spec((N_FF_CHUNKS, 1, 2 * FF_CHUNK)),
                  _const_spec((N_FF_CHUNKS, FF_CHUNK, D_MODEL)),
                  _const_spec((1, D_MODEL)), _const_spec((D_MODEL, D_MODEL)),
                  _const_spec((PLE_DIM, D_MODEL)), _const_spec((1, D_MODEL))],
        out_specs=pl.BlockSpec((tm, D_MODEL), row),
        out_shape=jax.ShapeDtypeStruct((T, D_MODEL), F32),
        scratch_shapes=[pltpu.VMEM((tm + SUBLANES, 2 * FF_CHUNK), F32),
                        pltpu.VMEM((N_FF_CHUNKS, SUBLANES, 2 * FF_CHUNK), F32),
                        pltpu.VMEM((tm, D_MODEL), F32)],
        compiler_params=pltpu.CompilerParams(dimension_semantics=("arbitrary",), vmem_limit_bytes=VMEM_LIMIT),
        name="convffn",
    )(x1, p2, gf, wup, cw, cb, wdn, gp, wpg, wp, gl)


def _chunk_cols(a):
    lead = a.shape[:-1]
    a = a.reshape(lead + (2, N_FF_CHUNKS, FF_CHUNK))
    a = jnp.moveaxis(a, -2, 0)
    return a.reshape((N_FF_CHUNKS,) + lead + (2 * FF_CHUNK,))


def kernel(x, p, rel_bias, norm_mix_g, w_in, b_if, m_conv_w, m_conv_b, m_norm_g, da_lambda, da_norm_g,
           w_br_m, w_br_d, w_out, norm_ffn_g, w_up, ffn_conv_w, ffn_conv_b, w_down, norm_ple_g, w_ple_gate,
           w_ple, norm_final_g):
    batch, seq, _ = x.shape
    depth = w_in.shape[0]
    T = batch * seq
    assert seq % TOKEN_TILE == 0 and seq % M_CHUNK == 0 and seq % ATT_BLOCK == 0
    xt = x.reshape(T, D_MODEL)
    row = lambda v: v.reshape(1, -1).astype(F32)

    for l in range(depth):
        wi = w_in[l]
        n_if = 2 * M_HEADS
        c_if = 4 * M_WIDTH
        w_all = jnp.concatenate(
            [wi[:, :c_if], wi[:, c_if + n_if:], wi[:, c_if:c_if + n_if],
             jnp.zeros((D_MODEL, LANES - n_if), wi.dtype)], axis=1).astype(BF16)
        bif = jnp.concatenate([b_if[l].astype(F32), jnp.zeros((LANES - n_if,), F32)]).reshape(1, LANES)

        mq, mk, mv, mo, dq, dk, dv, gates, ifg = _inproj(
            xt, row(norm_mix_g[l]), w_all, m_conv_w[l].astype(F32), row(m_conv_b[l]), bif, seq)

        gm = _mlstm(mq, mk, mv, mo, ifg, row(m_norm_g[l]), batch, seq)

        lam_init = 0.8 - 0.6 * math.exp(-0.3 * l)
        hd = _attention(rel_bias.astype(F32), da_lambda[l].astype(F32), dq, dk, dv, row(da_norm_g[l]),
                        batch, seq, lam_init)

        xt = _merge(gm, hd, gates, xt, w_br_m[l].astype(BF16), w_br_d[l].astype(BF16), w_out[l].astype(BF16))

        wup = _chunk_cols(w_up[l]).astype(BF16)
        cw = _chunk_cols(ffn_conv_w[l].astype(F32))
        cb = _chunk_cols(ffn_conv_b[l].astype(F32).reshape(1, -1))
        wdn = w_down[l].astype(BF16).reshape(N_FF_CHUNKS, FF_CHUNK, D_MODEL)
        xt = _ffn(xt, p[l].reshape(T, PLE_DIM), row(norm_ffn_g[l]), wup, cw, cb, wdn, row(norm_ple_g[l]),
                  w_ple_gate[l].astype(BF16), w_ple[l].astype(BF16), row(norm_final_g), seq,
                  final_norm=(l == depth - 1))

    return xt.reshape(batch, seq, D_MODEL)
```

```python
import functools
import math

import jax
import jax.numpy as jnp
from jax import lax
from jax.experimental import pallas as pl
from jax.experimental.pallas import tpu as pltpu

D_MODEL = 1024
PLE_DIM = 256
M_HEADS = 4
M_HEAD_DIM = 128
M_WIDTH = M_HEADS * M_HEAD_DIM
M_CONV = 4
DA_HEADS = 4
DA_HEAD_DIM = 64
DA_V_DIM = 2 * DA_HEAD_DIM
DA_WIDTH = DA_HEADS * DA_V_DIM
REL_BUCKETS = 32
REL_MAX_DIST = 128
D_FF = 2816
FFN_CONV = 3
EPS = 1e-6
NEG_BIG = -1e30
LOG2E = 1.4426950408889634

LANES = 128
SUBLANES = 8
VMEM_LIMIT = 56 * 1024 * 1024

TOKEN_TILE = 512
M_CHUNK = 256
ATT_BLOCK = 512
ATT_SUB = ATT_BLOCK // 2
FF_CHUNK = 256
N_FF_CHUNKS = D_FF // FF_CHUNK

F32 = jnp.float32
BF16 = jnp.bfloat16


def _dot(a, b):
    return jnp.dot(a, b, preferred_element_type=F32)


def _dot_nt(a, b):
    return lax.dot_general(a, b, (((1,), (1,)), ((), ())), preferred_element_type=F32)


def _dot_tn(a, b):
    return lax.dot_general(a, b, (((0,), (0,)), ((), ())), preferred_element_type=F32)


def _rms(x, g):
    return x * lax.rsqrt(jnp.mean(x * x, -1, keepdims=True) + EPS) * g


def _sigmoid(x):
    return 1.0 / (1.0 + jnp.exp(-x))


def _const_spec(shape):
    nd = len(shape)
    return pl.BlockSpec(shape, lambda *_: (0,) * nd, pipeline_mode=pl.Buffered(1))


def _inproj_kernel(x_ref, g_ref, w_ref, cw_ref, cb_ref, bif_ref,
                   mq_ref, mk_ref, mv_ref, mo_ref, dq_ref, dk_ref, dv_ref, gates_ref, ifg_ref,
                   ext_ref, *, tiles_per_seq):
    tm = x_ref.shape[0]
    i = pl.program_id(0)
    hb = _rms(x_ref[...], g_ref[...]).astype(BF16)

    @pl.when(i % tiles_per_seq == 0)
    def _():
        ext_ref[0:SUBLANES, :] = jnp.zeros((SUBLANES, 2 * M_WIDTH), F32)

    @pl.when(i % tiles_per_seq != 0)
    def _():
        ext_ref[0:SUBLANES, :] = ext_ref[tm:tm + SUBLANES, :]

    ext_ref[SUBLANES:SUBLANES + tm, :] = _dot(hb, w_ref[:, 0:2 * M_WIDTH])
    acc = cb_ref[...]
    for j in range(M_CONV):
        off = SUBLANES - (M_CONV - 1) + j
        acc = acc + cw_ref[j:j + 1, :] * ext_ref[off:off + tm, :]
    y = acc * _sigmoid(acc)
    mq_ref[...] = y[:, :M_WIDTH].astype(BF16)
    mk_ref[...] = (y[:, M_WIDTH:] * (M_HEAD_DIM ** -0.5)).astype(BF16)

    c = 2 * M_WIDTH
    mv_ref[...] = _dot(hb, w_ref[:, c:c + M_WIDTH]).astype(BF16); c += M_WIDTH
    mo_ref[...] = _dot(hb, w_ref[:, c:c + M_WIDTH]).astype(BF16); c += M_WIDTH
    dq_ref[...] = (_dot(hb, w_ref[:, c:c + DA_WIDTH]) * (LOG2E * DA_HEAD_DIM ** -0.5)).astype(BF16); c += DA_WIDTH
    dk_ref[...] = _dot(hb, w_ref[:, c:c + DA_WIDTH]).astype(BF16); c += DA_WIDTH
    dv_ref[...] = _dot(hb, w_ref[:, c:c + DA_WIDTH]).astype(BF16); c += DA_WIDTH
    for j in range(4):
        gates_ref[:, j * 512:(j + 1) * 512] = _dot(hb, w_ref[:, c:c + 512]).astype(BF16); c += 512
    ifg_ref[...] = _dot(hb, w_ref[:, c:c + LANES]) + bif_ref[...]


def _inproj(x2, g, w_all, cw, cb, bif, seq):
    T = x2.shape[0]
    tm = TOKEN_TILE
    n_cols = w_all.shape[1]
    row = lambda i: (i, 0)
    bf = lambda n: jax.ShapeDtypeStruct((T, n), BF16)
    out_shapes = [bf(M_WIDTH)] * 4 + [bf(DA_WIDTH)] * 3 + [bf(2 * D_MODEL), jax.ShapeDtypeStruct((T, LANES), F32)]
    out_specs = [pl.BlockSpec((tm, M_WIDTH), row)] * 7 + [pl.BlockSpec((tm, 2 * D_MODEL), row),
                                                          pl.BlockSpec((tm, LANES), row)]
    return pl.pallas_call(
        functools.partial(_inproj_kernel, tiles_per_seq=seq // tm),
        grid=(T // tm,),
        in_specs=[pl.BlockSpec((tm, D_MODEL), row), _const_spec((1, D_MODEL)), _const_spec((D_MODEL, n_cols)),
                  _const_spec((M_CONV, 2 * M_WIDTH)), _const_spec((1, 2 * M_WIDTH)), _const_spec((1, LANES))],
        out_specs=out_specs,
        out_shape=out_shapes,
        scratch_shapes=[pltpu.VMEM((tm + SUBLANES, 2 * M_WIDTH), F32)],
        compiler_params=pltpu.CompilerParams(dimension_semantics=("arbitrary",), vmem_limit_bytes=VMEM_LIMIT),
        name="inproj",
    )(x2, g, w_all, cw, cb, bif)


def _cumsum_rows(tril, x):
    hi = x.astype(BF16)
    r1 = x - hi.astype(F32)
    mid = r1.astype(BF16)
    lo = (r1 - mid.astype(F32)).astype(BF16)
    return _dot(tril, hi) + _dot(tril, mid) + _dot(tril, lo)


def _mlstm_kernel(mq_ref, mk_ref, mv_ref, mo_ref, ifg_ref, g_ref, out_ref, c_ref, n_ref, m_ref):
    L = mq_ref.shape[0]
    d = M_HEAD_DIM

    @pl.when(pl.program_id(1) == 0)
    def _():
        c_ref[...] = jnp.zeros(c_ref.shape, F32)
        n_ref[...] = jnp.zeros(n_ref.shape, F32)
        m_ref[...] = jnp.zeros(m_ref.shape, F32)

    ig = ifg_ref[...]
    fg = pltpu.roll(ig, LANES - M_HEADS, axis=1)
    logf = jnp.minimum(fg, 0.0) - jnp.log1p(jnp.exp(-jnp.abs(fg)))
    rows = lax.broadcasted_iota(jnp.int32, (L, L), 0)
    cols = lax.broadcasted_iota(jnp.int32, (L, L), 1)
    causal = rows >= cols
    b = _cumsum_rows(jnp.where(causal, 1.0, 0.0).astype(BF16), logf)
    b_tot = b[L - 1:L, :]
    u = ig - b
    w_end = b_tot + u
    m_loc = jnp.max(w_end, axis=0, keepdims=True)
    a = jnp.exp(w_end - m_loc)
    m_prev = m_ref[...]
    log_inter = b + m_prev
    m_new = jnp.maximum(b_tot + m_prev, m_loc)
    s_old = jnp.exp(b_tot + m_prev - m_new)
    s_loc = jnp.exp(m_loc - m_new)
    u_t = u.T

    for h in range(M_HEADS):
        hs = slice(h * d, (h + 1) * d)
        q = mq_ref[:, hs]
        k = mk_ref[:, hs]
        v = mv_ref[:, hs]
        log_d = jnp.where(causal, b[:, h:h + 1] + u_t[h:h + 1, :], -jnp.inf)
        li = log_inter[:, h:h + 1]
        m_t = jnp.maximum(li, jnp.max(log_d, axis=1, keepdims=True))
        dmat = jnp.exp(log_d - m_t)
        s_inter = jnp.exp(li - m_t)
        sqk = _dot_nt(q, k) * dmat
        c_prev = c_ref[h]
        n_prev = n_ref[h:h + 1, :]
        num = s_inter * _dot(q, c_prev.astype(BF16)) + _dot(sqk.astype(BF16), v)
        den = (s_inter * jnp.sum(q.astype(F32) * n_prev, axis=1, keepdims=True)
               + jnp.sum(sqk, axis=1, keepdims=True))
        hh = num / jnp.maximum(jnp.abs(den), jnp.exp(-m_t))
        y = _rms(hh, g_ref[:, hs])
        out_ref[:, hs] = (_sigmoid(mo_ref[:, hs].astype(F32)) * y).astype(BF16)

        ak = a[:, h:h + 1] * k.astype(F32)
        c_loc = _dot_tn(ak.astype(BF16), v)
        n_loc = jnp.sum(ak, axis=0, keepdims=True)
        so = s_old[:, h:h + 1]
        sl = s_loc[:, h:h + 1]
        c_ref[h] = so * c_prev + sl * c_loc
        n_ref[h:h + 1, :] = so * n_prev + sl * n_loc

    m_ref[...] = m_new


def _mlstm(mq, mk, mv, mo, ifg, g, batch, seq):
    T = mq.shape[0]
    L = M_CHUNK
    nc = seq // L
    row = lambda b, c: (b * nc + c, 0)
    blk = pl.BlockSpec((L, M_WIDTH), row)
    return pl.pallas_call(
        _mlstm_kernel,
        grid=(batch, nc),
        in_specs=[blk, blk, blk, blk, pl.BlockSpec((L, LANES), row), _const_spec((1, M_WIDTH))],
        out_specs=blk,
        out_shape=jax.ShapeDtypeStruct((T, M_WIDTH), BF16),
        scratch_shapes=[pltpu.VMEM((M_HEADS, M_HEAD_DIM, M_HEAD_DIM), F32),
                        pltpu.VMEM((SUBLANES, M_HEAD_DIM), F32),
                        pltpu.VMEM((1, LANES), F32)],
        compiler_params=pltpu.CompilerParams(dimension_semantics=("arbitrary", "arbitrary"),
                                             vmem_limit_bytes=VMEM_LIMIT),
        name="mlstm",
    )(mq, mk, mv, mo, ifg, g)


def _t5_bucket(n):
    max_exact = REL_BUCKETS // 2
    nf = jnp.maximum(n, 1).astype(F32)
    large = max_exact + (jnp.log(nf / max_exact) / math.log(REL_MAX_DIST / max_exact)
                         * (REL_BUCKETS - max_exact)).astype(jnp.int32)
    large = jnp.minimum(large, REL_BUCKETS - 1)
    return jnp.where(n < max_exact, n, large)


def _attn_kernel(rb_ref, lam_ref, q_ref, k_ref, v_ref, g_ref, out_ref,
                 bias_ref, m_ref, l_ref, acc_ref, *, lam_init):
    tq = q_ref.shape[0]
    tk = tq
    sb = ATT_SUB
    h = pl.program_id(1)
    qi = pl.program_id(2)

    kk = lax.broadcasted_iota(jnp.int32, (sb, sb), 0)
    qq = lax.broadcasted_iota(jnp.int32, (sb, sb), 1)

    @pl.when((pl.program_id(0) == 0) & (h == 0) & (qi == 0))
    def _():
        for delta in range(2):
            bucket = _t5_bucket(jnp.maximum(qq + delta * sb - kk, 0))
            for hh in range(DA_HEADS):
                t = jnp.zeros((sb, sb), F32)
                for j in range(REL_BUCKETS):
                    t = jnp.where(bucket == j, rb_ref[j, hh], t)
                bias_ref[hh, delta] = (t - rb_ref[REL_BUCKETS - 1, hh]) * LOG2E

    lv = lam_ref[...]
    lam = (jnp.exp(jnp.sum(lv[0:1] * lv[1:2], axis=1, keepdims=True))
           - jnp.exp(jnp.sum(lv[2:3] * lv[3:4], axis=1, keepdims=True)) + lam_init)

    q = q_ref[...]
    lane = lax.broadcasted_iota(jnp.int32, q.shape, 1)
    zero = jnp.zeros_like(q)
    qcat = jnp.concatenate([jnp.where(lane < DA_HEAD_DIM, q, zero),
                            jnp.where(lane >= DA_HEAD_DIM, q, zero)], axis=0)

    m_ref[...] = jnp.full(m_ref.shape, -jnp.inf, F32)
    l_ref[...] = jnp.zeros(l_ref.shape, F32)
    acc_ref[...] = jnp.zeros(acc_ref.shape, F32)

    def update(j, fix):
        start = pl.multiple_of(j * tk, tk)
        k = k_ref[pl.ds(start, tk), :]
        v = v_ref[pl.ds(start, tk), :]
        s = fix(_dot_nt(k, qcat))
        m_prev = m_ref[...]
        m_new = jnp.maximum(m_prev, jnp.max(s, axis=0, keepdims=True))
        alpha = jnp.exp2(m_prev - m_new)
        p = jnp.exp2(s - m_new)
        l_ref[...] = alpha * l_ref[...] + jnp.sum(p, axis=0, keepdims=True)
        acc_ref[...] = alpha * acc_ref[...] + _dot_tn(v, p.astype(BF16))
        m_ref[...] = m_new

    def far_body(j, carry):
        update(j, lambda s: s)
        return carry

    lax.fori_loop(0, qi - 1, far_body, 0)

    d0 = bias_ref[h, 0]
    d1 = bias_ref[h, 1]
    causal = qq >= kk

    def near_fix(s):
        bot = s[sb:]
        bot = jnp.concatenate([bot[:, 0:sb] + d1, bot[:, sb:tq], bot[:, tq:tq + sb] + d1, bot[:, tq + sb:]], axis=1)
        return jnp.concatenate([s[:sb], bot], axis=0)

    @pl.when(qi > 0)
    def _():
        update(qi - 1, near_fix)

    def diag_fix(s):
        top, bot = s[:sb], s[sb:]
        neg = jnp.full((sb, sb), NEG_BIG, F32)
        msk = lambda t: jnp.where(causal, t + d0, NEG_BIG)
        top = jnp.concatenate([msk(top[:, 0:sb]), top[:, sb:tq] + d1,
                               msk(top[:, tq:tq + sb]), top[:, tq + sb:] + d1], axis=1)
        bot = jnp.concatenate([neg, msk(bot[:, sb:tq]), neg, msk(bot[:, tq + sb:])], axis=1)
        return jnp.concatenate([top, bot], axis=0)

    update(qi, diag_fix)

    o = acc_ref[...] * (1.0 / l_ref[...])
    out = (o[:, :tq] - lam * o[:, tq:]).T
    out_ref[...] = (_rms(out, g_ref[...]) * (1.0 - lam_init)).astype(BF16)


def _attention(rel_bias, lam_vec, dq, dk, dv, g, batch, seq, lam_init):
    T = dq.shape[0]
    tq = ATT_BLOCK
    nq = seq // tq
    return pl.pallas_call(
        functools.partial(_attn_kernel, lam_init=lam_init),
        grid=(batch, DA_HEADS, nq),
        in_specs=[pl.BlockSpec(memory_space=pltpu.SMEM),
                  pl.BlockSpec((4, DA_HEAD_DIM), lambda b, h, i: (0, 0)),
                  pl.BlockSpec((tq, DA_V_DIM), lambda b, h, i: (b * nq + i, h)),
                  pl.BlockSpec((seq, DA_V_DIM), lambda b, h, i: (b, h)),
                  pl.BlockSpec((seq, DA_V_DIM), lambda b, h, i: (b, h)),
                  pl.BlockSpec((1, DA_V_DIM), lambda b, h, i: (0, h))],
        out_specs=pl.BlockSpec((tq, DA_V_DIM), lambda b, h, i: (b * nq + i, h)),
        out_shape=jax.ShapeDtypeStruct((T, DA_WIDTH), BF16),
        scratch_shapes=[pltpu.VMEM((DA_HEADS, 2, ATT_SUB, ATT_SUB), F32),
                        pltpu.VMEM((1, 2 * tq), F32),
                        pltpu.VMEM((1, 2 * tq), F32),
                        pltpu.VMEM((DA_V_DIM, 2 * tq), F32)],
        compiler_params=pltpu.CompilerParams(dimension_semantics=("arbitrary",) * 3,
                                             vmem_limit_bytes=VMEM_LIMIT),
        name="diffattn",
    )(rel_bias, lam_vec, dq, dk, dv, g)


def _merge_kernel(gm_ref, hd_ref, gates_ref, x_ref, wm_ref, wd_ref, wo_ref, out_ref):
    ya = _dot(gm_ref[...], wm_ref[...])
    yb = _dot(hd_ref[...], wd_ref[...])
    ga = _sigmoid(gates_ref[:, :D_MODEL].astype(F32))
    gb = _sigmoid(gates_ref[:, D_MODEL:].astype(F32))
    mixed = (ga * ya + gb * yb).astype(BF16)
    out_ref[...] = x_ref[...] + _dot(mixed, wo_ref[...])


def _merge(gm, hd, gates, x2, wm, wd, wo):
    T = x2.shape[0]
    tm = TOKEN_TILE
    row = lambda i: (i, 0)
    return pl.pallas_call(
        _merge_kernel,
        grid=(T // tm,),
        in_specs=[pl.BlockSpec((tm, M_WIDTH), row), pl.BlockSpec((tm, DA_WIDTH), row),
                  pl.BlockSpec((tm, 2 * D_MODEL), row), pl.BlockSpec((tm, D_MODEL), row),
                  _const_spec((M_WIDTH, D_MODEL)), _const_spec((DA_WIDTH, D_MODEL)),
                  _const_spec((D_MODEL, D_MODEL))],
        out_specs=pl.BlockSpec((tm, D_MODEL), row),
        out_shape=jax.ShapeDtypeStruct((T, D_MODEL), F32),
        compiler_params=pltpu.CompilerParams(dimension_semantics=("arbitrary",), vmem_limit_bytes=VMEM_LIMIT),
        name="merge",
    )(gm, hd, gates, x2, wm, wd, wo)


def _gelu_tanh(x):
    return 0.5 * x * (1.0 + jnp.tanh(math.sqrt(2.0 / math.pi) * (x + 0.044715 * (x * x * x))))


def _ffn_kernel(x_ref, p_ref, gf_ref, wup_ref, cw_ref, cb_ref, wdn_ref, gp_ref, wpg_ref, wp_ref, gl_ref,
                out_ref, ext_ref, carry_ref, acc_ref, *, tiles_per_seq, final_norm):
    tm = x_ref.shape[0]
    i = pl.program_id(0)
    x1 = x_ref[...]
    hb = _rms(x1, gf_ref[...]).astype(BF16)

    @pl.when(i % tiles_per_seq == 0)
    def _():
        carry_ref[...] = jnp.zeros(carry_ref.shape, F32)

    for c in range(N_FF_CHUNKS):
        ext_ref[0:SUBLANES, :] = carry_ref[c]
        ext_ref[SUBLANES:SUBLANES + tm, :] = _dot(hb, wup_ref[c])
        conv = cb_ref[c]
        for j in range(FFN_CONV):
            off = SUBLANES - (FFN_CONV - 1) + j
            conv = conv + cw_ref[c, j:j + 1, :] * ext_ref[off:off + tm, :]
        carry_ref[c] = ext_ref[tm:tm + SUBLANES, :]
        act = (_gelu_tanh(conv[:, FF_CHUNK:]) * conv[:, :FF_CHUNK]).astype(BF16)
        d = _dot(act, wdn_ref[c])
        if c == 0:
            acc_ref[...] = d
        else:
            acc_ref[...] += d

    x2 = x1 + acc_ref[...]
    hg = _rms(x2, gp_ref[...]).astype(BF16)
    gate = _sigmoid(_dot(hg, wpg_ref[...]))
    pe = _dot(p_ref[...].astype(BF16), wp_ref[...])
    x3 = x2 + gate * pe
    out_ref[...] = _rms(x3, gl_ref[...]) if final_norm else x3


def _ffn(x1, p2, gf, wup, cw, cb, wdn, gp, wpg, wp, gl, seq, final_norm):
    T = x1.shape[0]
    tm = TOKEN_TILE
    row = lambda i: (i, 0)
    return pl.pallas_call(
        functools.partial(_ffn_kernel, tiles_per_seq=seq // tm, final_norm=final_norm),
        grid=(T // tm,),
        in_specs=[pl.BlockSpec((tm, D_MODEL), row), pl.BlockSpec((tm, PLE_DIM), row),
                  _const_spec((1, D_MODEL)),
                  _const_spec((N_FF_CHUNKS, D_MODEL, 2 * FF_CHUNK)),
                  _const_spec((N_FF_CHUNKS, FFN_CONV, 2 * FF_CHUNK)),
                  _const_spec((N_FF_CHUNKS, 1, 2 * FF_CHUNK)),
                  _const_spec((N_FF_CHUNKS, FF_CHUNK, D_MODEL)),
                  _const_spec((1, D_MODEL)), _const_spec((D_MODEL, D_MODEL)),
                  _const_spec((PLE_DIM, D_MODEL)), _const_spec((1, D_MODEL))],
        out_specs=pl.BlockSpec((tm, D_MODEL), row),
        out_shape=jax.ShapeDtypeStruct((T, D_MODEL), F32),
        scratch_shapes=[pltpu.VMEM((tm + SUBLANES, 2 * FF_CHUNK), F32),
                        pltpu.VMEM((N_FF_CHUNKS, SUBLANES, 2 * FF_CHUNK), F32),
                        pltpu.VMEM((tm, D_MODEL), F32)],
        compiler_params=pltpu.CompilerParams(dimension_semantics=("arbitrary",), vmem_limit_bytes=VMEM_LIMIT),
        name="convffn",
    )(x1, p2, gf, wup, cw, cb, wdn, gp, wpg, wp, gl)


def _chunk_cols(a):
    lead = a.shape[:-1]
    a = a.reshape(lead + (2, N_FF_CHUNKS, FF_CHUNK))
    a = jnp.moveaxis(a, -2, 0)
    return a.reshape((N_FF_CHUNKS,) + lead + (2 * FF_CHUNK,))


def kernel(x, p, rel_bias, norm_mix_g, w_in, b_if, m_conv_w, m_conv_b, m_norm_g, da_lambda, da_norm_g,
           w_br_m, w_br_d, w_out, norm_ffn_g, w_up, ffn_conv_w, ffn_conv_b, w_down, norm_ple_g, w_ple_gate,
           w_ple, norm_final_g):
    batch, seq, _ = x.shape
    depth = w_in.shape[0]
    T = batch * seq
    assert seq % TOKEN_TILE == 0 and seq % M_CHUNK == 0 and seq % ATT_BLOCK == 0
    xt = x.reshape(T, D_MODEL)
    row = lambda v: v.reshape(1, -1).astype(F32)

    for l in range(depth):
        wi = w_in[l]
        n_if = 2 * M_HEADS
        c_if = 4 * M_WIDTH
        w_all = jnp.concatenate(
            [wi[:, :c_if], wi[:, c_if + n_if:], wi[:, c_if:c_if + n_if],
             jnp.zeros((D_MODEL, LANES - n_if), wi.dtype)], axis=1).astype(BF16)
        bif = jnp.concatenate([b_if[l].astype(F32), jnp.zeros((LANES - n_if,), F32)]).reshape(1, LANES)

        mq, mk, mv, mo, dq, dk, dv, gates, ifg = _inproj(
            xt, row(norm_mix_g[l]), w_all, m_conv_w[l].astype(F32), row(m_conv_b[l]), bif, seq)

        gm = _mlstm(mq, mk, mv, mo, ifg, row(m_norm_g[l]), batch, seq)

        lam_init = 0.8 - 0.6 * math.exp(-0.3 * l)
        hd = _attention(rel_bias.astype(F32), da_lambda[l].astype(F32), dq, dk, dv, row(da_norm_g[l]),
                        batch, seq, lam_init)

        xt = _merge(gm, hd, gates, xt, w_br_m[l].astype(BF16), w_br_d[l].astype(BF16), w_out[l].astype(BF16))

        wup = _chunk_cols(w_up[l]).astype(BF16)
        cw = _chunk_cols(ffn_conv_w[l].astype(F32))
        cb = _chunk_cols(ffn_conv_b[l].astype(F32).reshape(1, -1))
        wdn = w_down[l].astype(BF16).reshape(N_FF_CHUNKS, FF_CHUNK, D_MODEL)
        xt = _ffn(xt, p[l].reshape(T, PLE_DIM), row(norm_ffn_g[l]), wup, cw, cb, wdn, row(norm_ple_g[l]),
                  w_ple_gate[l].astype(BF16), w_ple[l].astype(BF16), row(norm_final_g), seq,
                  final_norm=(l == depth - 1))

    return xt.reshape(batch, seq, D_MODEL)
```

```python
import functools
import math

import jax
import jax.numpy as jnp
from jax import lax
from jax.experimental import pallas as pl
from jax.experimental.pallas import tpu as pltpu

D_MODEL = 1024
PLE_DIM = 256
M_HEADS = 4
M_HEAD_DIM = 128
M_WIDTH = M_HEADS * M_HEAD_DIM
M_CONV = 4
DA_HEADS = 4
DA_HEAD_DIM = 64
DA_V_DIM = 2 * DA_HEAD_DIM
DA_WIDTH = DA_HEADS * DA_V_DIM
REL_BUCKETS = 32
REL_MAX_DIST = 128
D_FF = 2816
FFN_CONV = 3
EPS = 1e-6
NEG_BIG = -1e30
LOG2E = 1.4426950408889634

LANES = 128
SUBLANES = 8
VMEM_LIMIT = 56 * 1024 * 1024

TOKEN_TILE = 512
M_CHUNK = 256
ATT_BLOCK = 512
ATT_SUB = ATT_BLOCK // 2
FF_CHUNK = 256
N_FF_CHUNKS = D_FF // FF_CHUNK

F32 = jnp.float32
BF16 = jnp.bfloat16


def _dot(a, b):
    return jnp.dot(a, b, preferred_element_type=F32)


def _dot_nt(a, b):
    return lax.dot_general(a, b, (((1,), (1,)), ((), ())), preferred_element_type=F32)


def _dot_tn(a, b):
    return lax.dot_general(a, b, (((0,), (0,)), ((), ())), preferred_element_type=F32)


def _rms(x, g):
    return x * lax.rsqrt(jnp.mean(x * x, -1, keepdims=True) + EPS) * g


def _sigmoid(x):
    return 1.0 / (1.0 + jnp.exp(-x))


def _shift_rows(u, prev, k):
    sub = lax.broadcasted_iota(jnp.int32, (SUBLANES, u.shape[1]), 0)
    r = pltpu.roll(u, k, axis=0)
    head = jnp.where(sub < k, pltpu.roll(prev, k, axis=0), r[:SUBLANES])
    return jnp.concatenate([head, r[SUBLANES:]], axis=0)


def _const_spec(shape):
    nd = len(shape)
    return pl.BlockSpec(shape, lambda *_: (0,) * nd, pipeline_mode=pl.Buffered(1))


def _inproj_kernel(x_ref, g_ref, w_ref, cw_ref, cb_ref, bif_ref,
                   mq_ref, mk_ref, mv_ref, mo_ref, dq_ref, dk_ref, dv_ref, gates_ref, ifg_ref,
                   carry_ref, *, tiles_per_seq):
    tm = x_ref.shape[0]
    i = pl.program_id(0)
    hb = _rms(x_ref[...], g_ref[...]).astype(BF16)

    @pl.when(i % tiles_per_seq == 0)
    def _():
        carry_ref[...] = jnp.zeros(carry_ref.shape, F32)

    def conv_silu(cols):
        u = _dot(hb, w_ref[:, cols])
        prev = carry_ref[:, cols]
        carry_ref[:, cols] = u[tm - SUBLANES:, :]
        y = cb_ref[:, cols] + cw_ref[M_CONV - 1:M_CONV, cols] * u
        for k in range(1, M_CONV):
            y = y + cw_ref[M_CONV - 1 - k:M_CONV - k, cols] * _shift_rows(u, prev, k)
        return y * _sigmoid(y)

    mq_ref[...] = conv_silu(slice(0, M_WIDTH)).astype(BF16)
    mk_ref[...] = (conv_silu(slice(M_WIDTH, 2 * M_WIDTH)) * (M_HEAD_DIM ** -0.5)).astype(BF16)

    c = 2 * M_WIDTH
    mv_ref[...] = _dot(hb, w_ref[:, c:c + M_WIDTH]).astype(BF16); c += M_WIDTH
    mo_ref[...] = _dot(hb, w_ref[:, c:c + M_WIDTH]).astype(BF16); c += M_WIDTH
    dq_ref[...] = (_dot(hb, w_ref[:, c:c + DA_WIDTH]) * (LOG2E * DA_HEAD_DIM ** -0.5)).astype(BF16); c += DA_WIDTH
    dk_ref[...] = _dot(hb, w_ref[:, c:c + DA_WIDTH]).astype(BF16); c += DA_WIDTH
    dv_ref[...] = _dot(hb, w_ref[:, c:c + DA_WIDTH]).astype(BF16); c += DA_WIDTH
    for j in range(4):
        gates_ref[:, j * 512:(j + 1) * 512] = _dot(hb, w_ref[:, c:c + 512]).astype(BF16); c += 512
    ifg_ref[...] = _dot(hb, w_ref[:, c:c + LANES]) + bif_ref[...]


def _inproj(x2, g, w_all, cw, cb, bif, seq):
    T = x2.shape[0]
    tm = TOKEN_TILE
    n_cols = w_all.shape[1]
    row = lambda i: (i, 0)
    bf = lambda n: jax.ShapeDtypeStruct((T, n), BF16)
    out_shapes = [bf(M_WIDTH)] * 4 + [bf(DA_WIDTH)] * 3 + [bf(2 * D_MODEL), jax.ShapeDtypeStruct((T, LANES), F32)]
    out_specs = [pl.BlockSpec((tm, M_WIDTH), row)] * 7 + [pl.BlockSpec((tm, 2 * D_MODEL), row),
                                                          pl.BlockSpec((tm, LANES), row)]
    return pl.pallas_call(
        functools.partial(_inproj_kernel, tiles_per_seq=seq // tm),
        grid=(T // tm,),
        in_specs=[pl.BlockSpec((tm, D_MODEL), row), _const_spec((1, D_MODEL)), _const_spec((D_MODEL, n_cols)),
                  _const_spec((M_CONV, 2 * M_WIDTH)), _const_spec((1, 2 * M_WIDTH)), _const_spec((1, LANES))],
        out_specs=out_specs,
        out_shape=out_shapes,
        scratch_shapes=[pltpu.VMEM((SUBLANES, 2 * M_WIDTH), F32)],
        compiler_params=pltpu.CompilerParams(dimension_semantics=("arbitrary",), vmem_limit_bytes=VMEM_LIMIT),
        name="inproj",
    )(x2, g, w_all, cw, cb, bif)


def _cumsum_rows(tril, x):
    hi = x.astype(BF16)
    r1 = x - hi.astype(F32)
    mid = r1.astype(BF16)
    lo = (r1 - mid.astype(F32)).astype(BF16)
    return _dot(tril, hi) + _dot(tril, mid) + _dot(tril, lo)


def _mlstm_kernel(mq_ref, mk_ref, mv_ref, mo_ref, ifg_ref, g_ref, out_ref, c_ref, n_ref, m_ref):
    L = mq_ref.shape[0]
    d = M_HEAD_DIM

    @pl.when(pl.program_id(1) == 0)
    def _():
        c_ref[...] = jnp.zeros(c_ref.shape, F32)
        n_ref[...] = jnp.zeros(n_ref.shape, F32)
        m_ref[...] = jnp.zeros(m_ref.shape, F32)

    ig = ifg_ref[...]
    fg = pltpu.roll(ig, LANES - M_HEADS, axis=1)
    logf = jnp.minimum(fg, 0.0) - jnp.log1p(jnp.exp(-jnp.abs(fg)))
    rows = lax.broadcasted_iota(jnp.int32, (L, L), 0)
    cols = lax.broadcasted_iota(jnp.int32, (L, L), 1)
    causal = rows >= cols
    b = _cumsum_rows(jnp.where(causal, 1.0, 0.0).astype(BF16), logf)
    b_tot = b[L - 1:L, :]
    u = ig - b
    w_end = b_tot + u
    m_loc = jnp.max(w_end, axis=0, keepdims=True)
    a = jnp.exp(w_end - m_loc)
    m_prev = m_ref[...]
    log_inter = b + m_prev
    m_new = jnp.maximum(b_tot + m_prev, m_loc)
    s_old = jnp.exp(b_tot + m_prev - m_new)
    s_loc = jnp.exp(m_loc - m_new)
    u_t = u.T

    for h in range(M_HEADS):
        hs = slice(h * d, (h + 1) * d)
        q = mq_ref[:, hs]
        k = mk_ref[:, hs]
        v = mv_ref[:, hs]
        log_d = jnp.where(causal, b[:, h:h + 1] + u_t[h:h + 1, :], -jnp.inf)
        li = log_inter[:, h:h + 1]
        m_t = jnp.maximum(li, jnp.max(log_d, axis=1, keepdims=True))
        dmat = jnp.exp(log_d - m_t)
        s_inter = jnp.exp(li - m_t)
        sqk = _dot_nt(q, k) * dmat
        c_prev = c_ref[h]
        n_prev = n_ref[h:h + 1, :]
        num = s_inter * _dot(q, c_prev.astype(BF16)) + _dot(sqk.astype(BF16), v)
        den = (s_inter * jnp.sum(q.astype(F32) * n_prev, axis=1, keepdims=True)
               + jnp.sum(sqk, axis=1, keepdims=True))
        hh = num / jnp.maximum(jnp.abs(den), jnp.exp(-m_t))
        y = _rms(hh, g_ref[:, hs])
        out_ref[:, hs] = (_sigmoid(mo_ref[:, hs].astype(F32)) * y).astype(BF16)

        ak = a[:, h:h + 1] * k.astype(F32)
        c_loc = _dot_tn(ak.astype(BF16), v)
        n_loc = jnp.sum(ak, axis=0, keepdims=True)
        so = s_old[:, h:h + 1]
        sl = s_loc[:, h:h + 1]
        c_ref[h] = so * c_prev + sl * c_loc
        n_ref[h:h + 1, :] = so * n_prev + sl * n_loc

    m_ref[...] = m_new


def _mlstm(mq, mk, mv, mo, ifg, g, batch, seq):
    T = mq.shape[0]
    L = M_CHUNK
    nc = seq // L
    row = lambda b, c: (b * nc + c, 0)
    blk = pl.BlockSpec((L, M_WIDTH), row)
    return pl.pallas_call(
        _mlstm_kernel,
        grid=(batch, nc),
        in_specs=[blk, blk, blk, blk, pl.BlockSpec((L, LANES), row), _const_spec((1, M_WIDTH))],
        out_specs=blk,
        out_shape=jax.ShapeDtypeStruct((T, M_WIDTH), BF16),
        scratch_shapes=[pltpu.VMEM((M_HEADS, M_HEAD_DIM, M_HEAD_DIM), F32),
                        pltpu.VMEM((SUBLANES, M_HEAD_DIM), F32),
                        pltpu.VMEM((1, LANES), F32)],
        compiler_params=pltpu.CompilerParams(dimension_semantics=("arbitrary", "arbitrary"),
                                             vmem_limit_bytes=VMEM_LIMIT),
        name="mlstm",
    )(mq, mk, mv, mo, ifg, g)


def _t5_bucket(n):
    max_exact = REL_BUCKETS // 2
    nf = jnp.maximum(n, 1).astype(F32)
    large = max_exact + (jnp.log(nf / max_exact) / math.log(REL_MAX_DIST / max_exact)
                         * (REL_BUCKETS - max_exact)).astype(jnp.int32)
    large = jnp.minimum(large, REL_BUCKETS - 1)
    return jnp.where(n < max_exact, n, large)


def _attn_kernel(rb_ref, lam_ref, q_ref, k_ref, v_ref, g_ref, out_ref,
                 bias_ref, m_ref, l_ref, acc_ref, *, lam_init):
    tq = q_ref.shape[0]
    tk = tq
    sb = ATT_SUB
    h = pl.program_id(1)
    qi = pl.program_id(2)

    kk = lax.broadcasted_iota(jnp.int32, (sb, sb), 0)
    qq = lax.broadcasted_iota(jnp.int32, (sb, sb), 1)

    @pl.when((pl.program_id(0) == 0) & (h == 0) & (qi == 0))
    def _():
        for delta in range(2):
            bucket = _t5_bucket(jnp.maximum(qq + delta * sb - kk, 0))
            for hh in range(DA_HEADS):
                t = jnp.zeros((sb, sb), F32)
                for j in range(REL_BUCKETS):
                    t = jnp.where(bucket == j, rb_ref[j, hh], t)
                bias_ref[hh, delta] = (t - rb_ref[REL_BUCKETS - 1, hh]) * LOG2E

    lv = lam_ref[...]
    lam = (jnp.exp(jnp.sum(lv[0:1] * lv[1:2], axis=1, keepdims=True))
           - jnp.exp(jnp.sum(lv[2:3] * lv[3:4], axis=1, keepdims=True)) + lam_init)

    q = q_ref[...]
    lane = lax.broadcasted_iota(jnp.int32, q.shape, 1)
    zero = jnp.zeros_like(q)
    qcat = jnp.concatenate([jnp.where(lane < DA_HEAD_DIM, q, zero),
                            jnp.where(lane >= DA_HEAD_DIM, q, zero)], axis=0)

    m_ref[...] = jnp.full(m_ref.shape, -jnp.inf, F32)
    l_ref[...] = jnp.zeros(l_ref.shape, F32)
    acc_ref[...] = jnp.zeros(acc_ref.shape, F32)

    def update(j, fix):
        start = pl.multiple_of(j * tk, tk)
        k = k_ref[pl.ds(start, tk), :]
        v = v_ref[pl.ds(start, tk), :]
        s = fix(_dot_nt(k, qcat))
        m_prev = m_ref[...]
        m_new = jnp.maximum(m_prev, jnp.max(s, axis=0, keepdims=True))
        alpha = jnp.exp2(m_prev - m_new)
        p = jnp.exp2(s - m_new)
        l_ref[...] = alpha * l_ref[...] + jnp.sum(p, axis=0, keepdims=True)
        acc_ref[...] = alpha * acc_ref[...] + _dot_tn(v, p.astype(BF16))
        m_ref[...] = m_new

    def far_body(j, carry):
        update(j, lambda s: s)
        return carry

    lax.fori_loop(0, qi - 1, far_body, 0)

    d0 = bias_ref[h, 0]
    d1 = bias_ref[h, 1]
    causal = qq >= kk

    def near_fix(s):
        bot = s[sb:]
        bot = jnp.concatenate([bot[:, 0:sb] + d1, bot[:, sb:tq], bot[:, tq:tq + sb] + d1, bot[:, tq + sb:]], axis=1)
        return jnp.concatenate([s[:sb], bot], axis=0)

    @pl.when(qi > 0)
    def _():
        update(qi - 1, near_fix)

    def diag_fix(s):
        top, bot = s[:sb], s[sb:]
        neg = jnp.full((sb, sb), NEG_BIG, F32)
        msk = lambda t: jnp.where(causal, t + d0, NEG_BIG)
        top = jnp.concatenate([msk(top[:, 0:sb]), top[:, sb:tq] + d1,
                               msk(top[:, tq:tq + sb]), top[:, tq + sb:] + d1], axis=1)
        bot = jnp.concatenate([neg, msk(bot[:, sb:tq]), neg, msk(bot[:, tq + sb:])], axis=1)
        return jnp.concatenate([top, bot], axis=0)

    update(qi, diag_fix)

    o = acc_ref[...] * (1.0 / l_ref[...])
    out = (o[:, :tq] - lam * o[:, tq:]).T
    out_ref[...] = (_rms(out, g_ref[...]) * (1.0 - lam_init)).astype(BF16)


def _attention(rel_bias, lam_vec, dq, dk, dv, g, batch, seq, lam_init):
    T = dq.shape[0]
    tq = ATT_BLOCK
    nq = seq // tq
    return pl.pallas_call(
        functools.partial(_attn_kernel, lam_init=lam_init),
        grid=(batch, DA_HEADS, nq),
        in_specs=[pl.BlockSpec(memory_space=pltpu.SMEM),
                  pl.BlockSpec((4, DA_HEAD_DIM), lambda b, h, i: (0, 0)),
                  pl.BlockSpec((tq, DA_V_DIM), lambda b, h, i: (b * nq + i, h)),
                  pl.BlockSpec((seq, DA_V_DIM), lambda b, h, i: (b, h)),
                  pl.BlockSpec((seq, DA_V_DIM), lambda b, h, i: (b, h)),
                  pl.BlockSpec((1, DA_V_DIM), lambda b, h, i: (0, h))],
        out_specs=pl.BlockSpec((tq, DA_V_DIM), lambda b, h, i: (b * nq + i, h)),
        out_shape=jax.ShapeDtypeStruct((T, DA_WIDTH), BF16),
        scratch_shapes=[pltpu.VMEM((DA_HEADS, 2, ATT_SUB, ATT_SUB), F32),
                        pltpu.VMEM((1, 2 * tq), F32),
                        pltpu.VMEM((1, 2 * tq), F32),
                        pltpu.VMEM((DA_V_DIM, 2 * tq), F32)],
        compiler_params=pltpu.CompilerParams(dimension_semantics=("arbitrary",) * 3,
                                             vmem_limit_bytes=VMEM_LIMIT),
        name="diffattn",
    )(rel_bias, lam_vec, dq, dk, dv, g)


def _merge_kernel(gm_ref, hd_ref, gates_ref, x_ref, wm_ref, wd_ref, wo_ref, out_ref):
    ya = _dot(gm_ref[...], wm_ref[...])
    yb = _dot(hd_ref[...], wd_ref[...])
    ga = _sigmoid(gates_ref[:, :D_MODEL].astype(F32))
    gb = _sigmoid(gates_ref[:, D_MODEL:].astype(F32))
    mixed = (ga * ya + gb * yb).astype(BF16)
    out_ref[...] = x_ref[...] + _dot(mixed, wo_ref[...])


def _merge(gm, hd, gates, x2, wm, wd, wo):
    T = x2.shape[0]
    tm = TOKEN_TILE
    row = lambda i: (i, 0)
    return pl.pallas_call(
        _merge_kernel,
        grid=(T // tm,),
        in_specs=[pl.BlockSpec((tm, M_WIDTH), row), pl.BlockSpec((tm, DA_WIDTH), row),
                  pl.BlockSpec((tm, 2 * D_MODEL), row), pl.BlockSpec((tm, D_MODEL), row),
                  _const_spec((M_WIDTH, D_MODEL)), _const_spec((DA_WIDTH, D_MODEL)),
                  _const_spec((D_MODEL, D_MODEL))],
        out_specs=pl.BlockSpec((tm, D_MODEL), row),
        out_shape=jax.ShapeDtypeStruct((T, D_MODEL), F32),
        compiler_params=pltpu.CompilerParams(dimension_semantics=("arbitrary",), vmem_limit_bytes=VMEM_LIMIT),
        name="merge",
    )(gm, hd, gates, x2, wm, wd, wo)


def _gelu_tanh(x):
    return 0.5 * x * (1.0 + jnp.tanh(math.sqrt(2.0 / math.pi) * (x + 0.044715 * (x * x * x))))


def _ffn_kernel(x_ref, p_ref, gf_ref, wup_ref, cw_ref, cb_ref, wdn_ref, gp_ref, wpg_ref, wp_ref, gl_ref,
                out_ref, carry_ref, acc_ref, *, tiles_per_seq, final_norm):
    tm = x_ref.shape[0]
    i = pl.program_id(0)
    x1 = x_ref[...]
    hb = _rms(x1, gf_ref[...]).astype(BF16)

    @pl.when(i % tiles_per_seq == 0)
    def _():
        carry_ref[...] = jnp.zeros(carry_ref.shape, F32)

    def up(c):
        lo = c * FF_CHUNK
        return [_dot(hb, wup_ref[:, off + lo:off + lo + FF_CHUNK]) for off in (0, D_FF)]

    def conv(c, u, off):
        cols = slice(off + c * FF_CHUNK, off + (c + 1) * FF_CHUNK)
        prev = carry_ref[:, cols]
        carry_ref[:, cols] = u[tm - SUBLANES:, :]
        y = cb_ref[:, cols] + cw_ref[FFN_CONV - 1:FFN_CONV, cols] * u
        for k in range(1, FFN_CONV):
            y = y + cw_ref[FFN_CONV - 1 - k:FFN_CONV - k, cols] * _shift_rows(u, prev, k)
        return y

    u_next = up(0)
    for c in range(N_FF_CHUNKS):
        u_val, u_gate = u_next
        if c + 1 < N_FF_CHUNKS:
            u_next = up(c + 1)
        act = (_gelu_tanh(conv(c, u_gate, D_FF)) * conv(c, u_val, 0)).astype(BF16)
        d = _dot(act, wdn_ref[c * FF_CHUNK:(c + 1) * FF_CHUNK, :])
        if c == 0:
            acc_ref[...] = d
        else:
            acc_ref[...] += d

    x2 = x1 + acc_ref[...]
    hg = _rms(x2, gp_ref[...]).astype(BF16)
    gate = _sigmoid(_dot(hg, wpg_ref[...]))
    pe = _dot(p_ref[...].astype(BF16), wp_ref[...])
    x3 = x2 + gate * pe
    out_ref[...] = _rms(x3, gl_ref[...]) if final_norm else x3


def _ffn(x1, p2, gf, wup, cw, cb, wdn, gp, wpg, wp, gl, seq, final_norm):
    T = x1.shape[0]
    tm = TOKEN_TILE
    row = lambda i: (i, 0)
    return pl.pallas_call(
        functools.partial(_ffn_kernel, tiles_per_seq=seq // tm, final_norm=final_norm),
        grid=(T // tm,),
        in_specs=[pl.BlockSpec((tm, D_MODEL), row), pl.BlockSpec((tm, PLE_DIM), row),
                  _const_spec((1, D_MODEL)),
                  _const_spec((D_MODEL, 2 * D_FF)), _const_spec((FFN_CONV, 2 * D_FF)),
                  _const_spec((1, 2 * D_FF)), _const_spec((D_FF, D_MODEL)),
                  _const_spec((1, D_MODEL)), _const_spec((D_MODEL, D_MODEL)),
                  _const_spec((PLE_DIM, D_MODEL)), _const_spec((1, D_MODEL))],
        out_specs=pl.BlockSpec((tm, D_MODEL), row),
        out_shape=jax.ShapeDtypeStruct((T, D_MODEL), F32),
        scratch_shapes=[pltpu.VMEM((SUBLANES, 2 * D_FF), F32),
                        pltpu.VMEM((tm, D_MODEL), F32)],
        compiler_params=pltpu.CompilerParams(dimension_semantics=("arbitrary",), vmem_limit_bytes=VMEM_LIMIT),
        name="convffn",
    )(x1, p2, gf, wup, cw, cb, wdn, gp, wpg, wp, gl)


def kernel(x, p, rel_bias, norm_mix_g, w_in, b_if, m_conv_w, m_conv_b, m_norm_g, da_lambda, da_norm_g,
           w_br_m, w_br_d, w_out, norm_ffn_g, w_up, ffn_conv_w, ffn_conv_b, w_down, norm_ple_g, w_ple_gate,
           w_ple, norm_final_g):
    batch, seq, _ = x.shape
    depth = w_in.shape[0]
    T = batch * seq
    assert seq % TOKEN_TILE == 0 and seq % M_CHUNK == 0 and seq % ATT_BLOCK == 0
    xt = x.reshape(T, D_MODEL)
    row = lambda v: v.reshape(1, -1).astype(F32)

    for l in range(depth):
        wi = w_in[l].astype(BF16)
        n_if = 2 * M_HEADS
        c_if = 4 * M_WIDTH
        w_all = jnp.concatenate(
            [wi[:, :c_if], wi[:, c_if + n_if:], wi[:, c_if:c_if + n_if],
             jnp.zeros((D_MODEL, LANES - n_if), BF16)], axis=1)
        bif = jnp.concatenate([b_if[l].astype(F32), jnp.zeros((LANES - n_if,), F32)]).reshape(1, LANES)

        mq, mk, mv, mo, dq, dk, dv, gates, ifg = _inproj(
            xt, row(norm_mix_g[l]), w_all, m_conv_w[l].astype(F32), row(m_conv_b[l]), bif, seq)

        gm = _mlstm(mq, mk, mv, mo, ifg, row(m_norm_g[l]), batch, seq)

        lam_init = 0.8 - 0.6 * math.exp(-0.3 * l)
        hd = _attention(rel_bias.astype(F32), da_lambda[l].astype(F32), dq, dk, dv, row(da_norm_g[l]),
                        batch, seq, lam_init)

        xt = _merge(gm, hd, gates, xt, w_br_m[l].astype(BF16), w_br_d[l].astype(BF16), w_out[l].astype(BF16))

        xt = _ffn(xt, p[l].reshape(T, PLE_DIM), row(norm_ffn_g[l]), w_up[l].astype(BF16),
                  ffn_conv_w[l].astype(F32), row(ffn_conv_b[l]), w_down[l].astype(BF16), row(norm_ple_g[l]),
                  w_ple_gate[l].astype(BF16), w_ple[l].astype(BF16), row(norm_final_g), seq,
                  final_norm=(l == depth - 1))

    return xt.reshape(batch, seq, D_MODEL)
```

```python
import functools
import math

import jax
import jax.numpy as jnp
from jax import lax
from jax.experimental import pallas as pl
from jax.experimental.pallas import tpu as pltpu

D_MODEL = 1024
PLE_DIM = 256
M_HEADS = 4
M_HEAD_DIM = 128
M_WIDTH = M_HEADS * M_HEAD_DIM
M_CONV = 4
DA_HEADS = 4
DA_HEAD_DIM = 64
DA_V_DIM = 2 * DA_HEAD_DIM
DA_WIDTH = DA_HEADS * DA_V_DIM
REL_BUCKETS = 32
REL_MAX_DIST = 128
D_FF = 2816
FFN_CONV = 3
EPS = 1e-6
NEG_BIG = -1e30
LOG2E = 1.4426950408889634

LANES = 128
SUBLANES = 8
VMEM_LIMIT = 56 * 1024 * 1024

TOKEN_TILE = 512
M_CHUNK = 256
ATT_BLOCK = 512
ATT_SUB = ATT_BLOCK // 2
FF_CHUNK = 256
N_FF_CHUNKS = D_FF // FF_CHUNK

F32 = jnp.float32
BF16 = jnp.bfloat16


def _dot(a, b):
    return jnp.dot(a, b, preferred_element_type=F32)


def _dot_nt(a, b):
    return lax.dot_general(a, b, (((1,), (1,)), ((), ())), preferred_element_type=F32)


def _dot_tn(a, b):
    return lax.dot_general(a, b, (((0,), (0,)), ((), ())), preferred_element_type=F32)


def _rms(x, g):
    return x * lax.rsqrt(jnp.mean(x * x, -1, keepdims=True) + EPS) * g


def _sigmoid(x):
    return 1.0 / (1.0 + jnp.exp(-x))


def _shift_rows(u, prev, k):
    sub = lax.broadcasted_iota(jnp.int32, (SUBLANES, u.shape[1]), 0)
    r = pltpu.roll(u, k, axis=0)
    head = jnp.where(sub < k, pltpu.roll(prev, k, axis=0), r[:SUBLANES])
    return jnp.concatenate([head, r[SUBLANES:]], axis=0)


def _const_spec(shape):
    nd = len(shape)
    return pl.BlockSpec(shape, lambda *_: (0,) * nd, pipeline_mode=pl.Buffered(1))


def _inproj_kernel(x_ref, g_ref, w_ref, cw_ref, cb_ref, bif_ref,
                   mq_ref, mk_ref, mv_ref, mo_ref, dq_ref, dk_ref, dv_ref, gates_ref, ifg_ref,
                   carry_ref, *, tiles_per_seq):
    tm = x_ref.shape[0]
    i = pl.program_id(0)
    hb = _rms(x_ref[...], g_ref[...]).astype(BF16)

    @pl.when(i % tiles_per_seq == 0)
    def _():
        carry_ref[...] = jnp.zeros(carry_ref.shape, F32)

    def conv_silu(cols):
        u = _dot(hb, w_ref[:, cols])
        prev = carry_ref[:, cols]
        carry_ref[:, cols] = u[tm - SUBLANES:, :]
        y = cb_ref[:, cols] + cw_ref[M_CONV - 1:M_CONV, cols] * u
        for k in range(1, M_CONV):
            y = y + cw_ref[M_CONV - 1 - k:M_CONV - k, cols] * _shift_rows(u, prev, k)
        return y * _sigmoid(y)

    mq_ref[...] = conv_silu(slice(0, M_WIDTH)).astype(BF16)
    mk_ref[...] = (conv_silu(slice(M_WIDTH, 2 * M_WIDTH)) * (M_HEAD_DIM ** -0.5)).astype(BF16)

    c = 2 * M_WIDTH
    mv_ref[...] = _dot(hb, w_ref[:, c:c + M_WIDTH]).astype(BF16); c += M_WIDTH
    mo_ref[...] = _dot(hb, w_ref[:, c:c + M_WIDTH]).astype(BF16); c += M_WIDTH
    dq_ref[...] = (_dot(hb, w_ref[:, c:c + DA_WIDTH]) * (LOG2E * DA_HEAD_DIM ** -0.5)).astype(BF16); c += DA_WIDTH
    dk_ref[...] = _dot(hb, w_ref[:, c:c + DA_WIDTH]).astype(BF16); c += DA_WIDTH
    dv_ref[...] = _dot(hb, w_ref[:, c:c + DA_WIDTH]).astype(BF16); c += DA_WIDTH
    for j in range(4):
        gates_ref[:, j * 512:(j + 1) * 512] = _dot(hb, w_ref[:, c:c + 512]).astype(BF16); c += 512
    ifg_ref[...] = _dot(hb, w_ref[:, c:c + LANES]) + bif_ref[...]


def _inproj(x2, g, w_all, cw, cb, bif, seq):
    T = x2.shape[0]
    tm = TOKEN_TILE
    n_cols = w_all.shape[1]
    row = lambda i: (i, 0)
    bf = lambda n: jax.ShapeDtypeStruct((T, n), BF16)
    out_shapes = [bf(M_WIDTH)] * 4 + [bf(DA_WIDTH)] * 3 + [bf(2 * D_MODEL), jax.ShapeDtypeStruct((T, LANES), F32)]
    out_specs = [pl.BlockSpec((tm, M_WIDTH), row)] * 7 + [pl.BlockSpec((tm, 2 * D_MODEL), row),
                                                          pl.BlockSpec((tm, LANES), row)]
    return pl.pallas_call(
        functools.partial(_inproj_kernel, tiles_per_seq=seq // tm),
        grid=(T // tm,),
        in_specs=[pl.BlockSpec((tm, D_MODEL), row), _const_spec((1, D_MODEL)), _const_spec((D_MODEL, n_cols)),
                  _const_spec((M_CONV, 2 * M_WIDTH)), _const_spec((1, 2 * M_WIDTH)), _const_spec((1, LANES))],
        out_specs=out_specs,
        out_shape=out_shapes,
        scratch_shapes=[pltpu.VMEM((SUBLANES, 2 * M_WIDTH), F32)],
        compiler_params=pltpu.CompilerParams(dimension_semantics=("arbitrary",), vmem_limit_bytes=VMEM_LIMIT),
        name="inproj",
    )(x2, g, w_all, cw, cb, bif)


def _cumsum_rows(tril, x):
    hi = x.astype(BF16)
    r1 = x - hi.astype(F32)
    mid = r1.astype(BF16)
    lo = (r1 - mid.astype(F32)).astype(BF16)
    return _dot(tril, hi) + _dot(tril, mid) + _dot(tril, lo)


def _mlstm_kernel(mq_ref, mk_ref, mv_ref, mo_ref, ifg_ref, g_ref, out_ref, c_ref, n_ref, m_ref):
    L = mq_ref.shape[0]
    d = M_HEAD_DIM

    @pl.when(pl.program_id(1) == 0)
    def _():
        c_ref[...] = jnp.zeros(c_ref.shape, F32)
        n_ref[...] = jnp.zeros(n_ref.shape, F32)
        m_ref[...] = jnp.zeros(m_ref.shape, F32)

    ig = ifg_ref[...]
    fg = pltpu.roll(ig, LANES - M_HEADS, axis=1)
    logf = jnp.minimum(fg, 0.0) - jnp.log1p(jnp.exp(-jnp.abs(fg)))
    rows = lax.broadcasted_iota(jnp.int32, (L, L), 0)
    cols = lax.broadcasted_iota(jnp.int32, (L, L), 1)
    causal = rows >= cols
    b = _cumsum_rows(jnp.where(causal, 1.0, 0.0).astype(BF16), logf)
    b_tot = b[L - 1:L, :]
    u = ig - b
    w_end = b_tot + u
    m_loc = jnp.max(w_end, axis=0, keepdims=True)
    a = jnp.exp(w_end - m_loc)
    m_prev = m_ref[...]
    log_inter = b + m_prev
    m_new = jnp.maximum(b_tot + m_prev, m_loc)
    s_old = jnp.exp(b_tot + m_prev - m_new)
    s_loc = jnp.exp(m_loc - m_new)
    u_t = u.T

    for h in range(M_HEADS):
        hs = slice(h * d, (h + 1) * d)
        q = mq_ref[:, hs]
        k = mk_ref[:, hs]
        v = mv_ref[:, hs]
        log_d = jnp.where(causal, b[:, h:h + 1] + u_t[h:h + 1, :], -jnp.inf)
        li = log_inter[:, h:h + 1]
        m_t = jnp.maximum(li, jnp.max(log_d, axis=1, keepdims=True))
        dmat = jnp.exp(log_d - m_t)
        s_inter = jnp.exp(li - m_t)
        sqk = _dot_nt(q, k) * dmat
        c_prev = c_ref[h]
        n_prev = n_ref[h:h + 1, :]
        num = s_inter * _dot(q, c_prev.astype(BF16)) + _dot(sqk.astype(BF16), v)
        den = (s_inter * jnp.sum(q.astype(F32) * n_prev, axis=1, keepdims=True)
               + jnp.sum(sqk, axis=1, keepdims=True))
        hh = num / jnp.maximum(jnp.abs(den), jnp.exp(-m_t))
        y = _rms(hh, g_ref[:, hs])
        out_ref[:, hs] = (_sigmoid(mo_ref[:, hs].astype(F32)) * y).astype(BF16)

        ak = a[:, h:h + 1] * k.astype(F32)
        c_loc = _dot_tn(ak.astype(BF16), v)
        n_loc = jnp.sum(ak, axis=0, keepdims=True)
        so = s_old[:, h:h + 1]
        sl = s_loc[:, h:h + 1]
        c_ref[h] = so * c_prev + sl * c_loc
        n_ref[h:h + 1, :] = so * n_prev + sl * n_loc

    m_ref[...] = m_new


def _mlstm(mq, mk, mv, mo, ifg, g, batch, seq):
    T = mq.shape[0]
    L = M_CHUNK
    nc = seq // L
    row = lambda b, c: (b * nc + c, 0)
    blk = pl.BlockSpec((L, M_WIDTH), row)
    return pl.pallas_call(
        _mlstm_kernel,
        grid=(batch, nc),
        in_specs=[blk, blk, blk, blk, pl.BlockSpec((L, LANES), row), _const_spec((1, M_WIDTH))],
        out_specs=blk,
        out_shape=jax.ShapeDtypeStruct((T, M_WIDTH), BF16),
        scratch_shapes=[pltpu.VMEM((M_HEADS, M_HEAD_DIM, M_HEAD_DIM), F32),
                        pltpu.VMEM((SUBLANES, M_HEAD_DIM), F32),
                        pltpu.VMEM((1, LANES), F32)],
        compiler_params=pltpu.CompilerParams(dimension_semantics=("arbitrary", "arbitrary"),
                                             vmem_limit_bytes=VMEM_LIMIT),
        name="mlstm",
    )(mq, mk, mv, mo, ifg, g)


def _t5_bucket(n):
    max_exact = REL_BUCKETS // 2
    nf = jnp.maximum(n, 1).astype(F32)
    large = max_exact + (jnp.log(nf / max_exact) / math.log(REL_MAX_DIST / max_exact)
                         * (REL_BUCKETS - max_exact)).astype(jnp.int32)
    large = jnp.minimum(large, REL_BUCKETS - 1)
    return jnp.where(n < max_exact, n, large)


def _attn_kernel(rb_ref, lam_ref, q_ref, k_ref, v_ref, g_ref, out_ref,
                 bias_ref, m_ref, l_ref, acc_ref, *, lam_init):
    tq = q_ref.shape[0]
    tk = tq
    sb = ATT_SUB
    h = pl.program_id(1)
    qi = pl.program_id(2)

    kk = lax.broadcasted_iota(jnp.int32, (sb, sb), 0)
    qq = lax.broadcasted_iota(jnp.int32, (sb, sb), 1)

    @pl.when((pl.program_id(0) == 0) & (h == 0) & (qi == 0))
    def _():
        for delta in range(2):
            bucket = _t5_bucket(jnp.maximum(qq + delta * sb - kk, 0))
            for hh in range(DA_HEADS):
                t = jnp.zeros((sb, sb), F32)
                for j in range(REL_BUCKETS):
                    t = jnp.where(bucket == j, rb_ref[j, hh], t)
                bias_ref[hh, delta] = (t - rb_ref[REL_BUCKETS - 1, hh]) * LOG2E

    lv = lam_ref[...]
    lam = (jnp.exp(jnp.sum(lv[0:1] * lv[1:2], axis=1, keepdims=True))
           - jnp.exp(jnp.sum(lv[2:3] * lv[3:4], axis=1, keepdims=True)) + lam_init)

    q = q_ref[...]
    lane = lax.broadcasted_iota(jnp.int32, q.shape, 1)
    zero = jnp.zeros_like(q)
    qcat = jnp.concatenate([jnp.where(lane < DA_HEAD_DIM, q, zero),
                            jnp.where(lane >= DA_HEAD_DIM, q, zero)], axis=0)

    m_ref[...] = jnp.full(m_ref.shape, -jnp.inf, F32)
    l_ref[...] = jnp.zeros(l_ref.shape, F32)
    acc_ref[...] = jnp.zeros(acc_ref.shape, F32)

    n_groups = 2 * tq // sb

    d0 = bias_ref[h, 0]
    d1 = bias_ref[h, 1]
    causal = qq >= kk
    ident = lambda s: s
    near = lambda s: jnp.concatenate([s[:sb], s[sb:] + d1], axis=0)
    early = lambda s: jnp.where(causal, s + d0, NEG_BIG)
    late = lambda s: jnp.concatenate([s[:sb] + d1, jnp.where(causal, s[sb:] + d0, NEG_BIG)], axis=0)

    def plan(j, n):
        if j == n - 1:
            return [(sb, early), (tk, late)] * (n_groups // 2)
        if j == n - 2:
            return [(tk, near), (tk, ident)] * (n_groups // 2)
        return [(tk, ident)] * n_groups

    def scores(j, n):
        k = k_ref[j * tk:(j + 1) * tk, :]
        return [_dot_nt(k[:rows], qcat[g * sb:(g + 1) * sb]) for g, (rows, _) in enumerate(plan(j, n))]

    def softmax_pv(j, n, sc):
        v = v_ref[j * tk:(j + 1) * tk, :]
        for g, (rows, fix) in enumerate(plan(j, n)):
            cols = slice(g * sb, (g + 1) * sb)
            s = fix(sc[g])
            m_prev = m_ref[:, cols]
            m_new = jnp.maximum(m_prev, jnp.max(s, axis=0, keepdims=True))
            alpha = jnp.exp2(m_prev - m_new)
            p = jnp.exp2(s - m_new)
            l_ref[:, cols] = alpha * l_ref[:, cols] + jnp.sum(p, axis=0, keepdims=True)
            acc_ref[:, cols] = alpha * acc_ref[:, cols] + _dot_tn(v[:rows], p.astype(BF16))
            m_ref[:, cols] = m_new

    def sweep(n):
        sc = scores(0, n)
        for j in range(n):
            nxt = scores(j + 1, n) if j + 1 < n else None
            softmax_pv(j, n, sc)
            sc = nxt

    for n in range(1, k_ref.shape[0] // tk + 1):
        pl.when(qi == n - 1)(functools.partial(sweep, n))

    o = acc_ref[...] * (1.0 / l_ref[...])
    out = (o[:, :tq] - lam * o[:, tq:]).T
    out_ref[...] = (_rms(out, g_ref[...]) * (1.0 - lam_init)).astype(BF16)


def _attention(rel_bias, lam_vec, dq, dk, dv, g, batch, seq, lam_init):
    T = dq.shape[0]
    tq = ATT_BLOCK
    nq = seq // tq
    return pl.pallas_call(
        functools.partial(_attn_kernel, lam_init=lam_init),
        grid=(batch, DA_HEADS, nq),
        in_specs=[pl.BlockSpec(memory_space=pltpu.SMEM),
                  pl.BlockSpec((4, DA_HEAD_DIM), lambda b, h, i: (0, 0)),
                  pl.BlockSpec((tq, DA_V_DIM), lambda b, h, i: (b * nq + i, h)),
                  pl.BlockSpec((seq, DA_V_DIM), lambda b, h, i: (b, h)),
                  pl.BlockSpec((seq, DA_V_DIM), lambda b, h, i: (b, h)),
                  pl.BlockSpec((1, DA_V_DIM), lambda b, h, i: (0, h))],
        out_specs=pl.BlockSpec((tq, DA_V_DIM), lambda b, h, i: (b * nq + i, h)),
        out_shape=jax.ShapeDtypeStruct((T, DA_WIDTH), BF16),
        scratch_shapes=[pltpu.VMEM((DA_HEADS, 2, ATT_SUB, ATT_SUB), F32),
                        pltpu.VMEM((1, 2 * tq), F32),
                        pltpu.VMEM((1, 2 * tq), F32),
                        pltpu.VMEM((DA_V_DIM, 2 * tq), F32)],
        compiler_params=pltpu.CompilerParams(dimension_semantics=("arbitrary",) * 3,
                                             vmem_limit_bytes=VMEM_LIMIT),
        name="diffattn",
    )(rel_bias, lam_vec, dq, dk, dv, g)


def _merge_kernel(gm_ref, hd_ref, gates_ref, x_ref, wm_ref, wd_ref, wo_ref, out_ref):
    ya = _dot(gm_ref[...], wm_ref[...])
    yb = _dot(hd_ref[...], wd_ref[...])
    ga = _sigmoid(gates_ref[:, :D_MODEL].astype(F32))
    gb = _sigmoid(gates_ref[:, D_MODEL:].astype(F32))
    mixed = (ga * ya + gb * yb).astype(BF16)
    out_ref[...] = x_ref[...] + _dot(mixed, wo_ref[...])


def _merge(gm, hd, gates, x2, wm, wd, wo):
    T = x2.shape[0]
    tm = TOKEN_TILE
    row = lambda i: (i, 0)
    return pl.pallas_call(
        _merge_kernel,
        grid=(T // tm,),
        in_specs=[pl.BlockSpec((tm, M_WIDTH), row), pl.BlockSpec((tm, DA_WIDTH), row),
                  pl.BlockSpec((tm, 2 * D_MODEL), row), pl.BlockSpec((tm, D_MODEL), row),
                  _const_spec((M_WIDTH, D_MODEL)), _const_spec((DA_WIDTH, D_MODEL)),
                  _const_spec((D_MODEL, D_MODEL))],
        out_specs=pl.BlockSpec((tm, D_MODEL), row),
        out_shape=jax.ShapeDtypeStruct((T, D_MODEL), F32),
        compiler_params=pltpu.CompilerParams(dimension_semantics=("arbitrary",), vmem_limit_bytes=VMEM_LIMIT),
        name="merge",
    )(gm, hd, gates, x2, wm, wd, wo)


def _gelu_tanh(x):
    return 0.5 * x * (1.0 + jnp.tanh(math.sqrt(2.0 / math.pi) * (x + 0.044715 * (x * x * x))))


def _ffn_kernel(x_ref, p_ref, gf_ref, wup_ref, cw_ref, cb_ref, wdn_ref, gp_ref, wpg_ref, wp_ref, gl_ref,
                out_ref, carry_ref, acc_ref, *, tiles_per_seq, final_norm):
    tm = x_ref.shape[0]
    i = pl.program_id(0)
    x1 = x_ref[...]
    hb = _rms(x1, gf_ref[...]).astype(BF16)

    @pl.when(i % tiles_per_seq == 0)
    def _():
        carry_ref[...] = jnp.zeros(carry_ref.shape, F32)

    def up(c):
        lo = c * FF_CHUNK
        return [_dot(hb, wup_ref[:, off + lo:off + lo + FF_CHUNK]) for off in (0, D_FF)]

    def conv(c, u, off):
        cols = slice(off + c * FF_CHUNK, off + (c + 1) * FF_CHUNK)
        prev = carry_ref[:, cols]
        carry_ref[:, cols] = u[tm - SUBLANES:, :]
        y = cb_ref[:, cols] + cw_ref[FFN_CONV - 1:FFN_CONV, cols] * u
        for k in range(1, FFN_CONV):
            y = y + cw_ref[FFN_CONV - 1 - k:FFN_CONV - k, cols] * _shift_rows(u, prev, k)
        return y

    u_next = up(0)
    for c in range(N_FF_CHUNKS):
        u_val, u_gate = u_next
        if c + 1 < N_FF_CHUNKS:
            u_next = up(c + 1)
        act = (_gelu_tanh(conv(c, u_gate, D_FF)) * conv(c, u_val, 0)).astype(BF16)
        d = _dot(act, wdn_ref[c * FF_CHUNK:(c + 1) * FF_CHUNK, :])
        if c == 0:
            acc_ref[...] = d
        else:
            acc_ref[...] += d

    x2 = x1 + acc_ref[...]
    hg = _rms(x2, gp_ref[...]).astype(BF16)
    gate = _sigmoid(_dot(hg, wpg_ref[...]))
    pe = _dot(p_ref[...].astype(BF16), wp_ref[...])
    x3 = x2 + gate * pe
    out_ref[...] = _rms(x3, gl_ref[...]) if final_norm else x3


def _ffn(x1, p2, gf, wup, cw, cb, wdn, gp, wpg, wp, gl, seq, final_norm):
    T = x1.shape[0]
    tm = TOKEN_TILE
    row = lambda i: (i, 0)
    return pl.pallas_call(
        functools.partial(_ffn_kernel, tiles_per_seq=seq // tm, final_norm=final_norm),
        grid=(T // tm,),
        in_specs=[pl.BlockSpec((tm, D_MODEL), row), pl.BlockSpec((tm, PLE_DIM), row),
                  _const_spec((1, D_MODEL)),
                  _const_spec((D_MODEL, 2 * D_FF)), _const_spec((FFN_CONV, 2 * D_FF)),
                  _const_spec((1, 2 * D_FF)), _const_spec((D_FF, D_MODEL)),
                  _const_spec((1, D_MODEL)), _const_spec((D_MODEL, D_MODEL)),
                  _const_spec((PLE_DIM, D_MODEL)), _const_spec((1, D_MODEL))],
        out_specs=pl.BlockSpec((tm, D_MODEL), row),
        out_shape=jax.ShapeDtypeStruct((T, D_MODEL), F32),
        scratch_shapes=[pltpu.VMEM((SUBLANES, 2 * D_FF), F32),
                        pltpu.VMEM((tm, D_MODEL), F32)],
        compiler_params=pltpu.CompilerParams(dimension_semantics=("arbitrary",), vmem_limit_bytes=VMEM_LIMIT),
        name="convffn",
    )(x1, p2, gf, wup, cw, cb, wdn, gp, wpg, wp, gl)


def kernel(x, p, rel_bias, norm_mix_g, w_in, b_if, m_conv_w, m_conv_b, m_norm_g, da_lambda, da_norm_g,
           w_br_m, w_br_d, w_out, norm_ffn_g, w_up, ffn_conv_w, ffn_conv_b, w_down, norm_ple_g, w_ple_gate,
           w_ple, norm_final_g):
    batch, seq, _ = x.shape
    depth = w_in.shape[0]
    T = batch * seq
    assert seq % TOKEN_TILE == 0 and seq % M_CHUNK == 0 and seq % ATT_BLOCK == 0
    xt = x.reshape(T, D_MODEL)
    row = lambda v: v.reshape(1, -1).astype(F32)

    for l in range(depth):
        wi = w_in[l].astype(BF16)
        n_if = 2 * M_HEADS
        c_if = 4 * M_WIDTH
        w_all = jnp.concatenate(
            [wi[:, :c_if], wi[:, c_if + n_if:], wi[:, c_if:c_if + n_if],
             jnp.zeros((D_MODEL, LANES - n_if), BF16)], axis=1)
        bif = jnp.concatenate([b_if[l].astype(F32), jnp.zeros((LANES - n_if,), F32)]).reshape(1, LANES)

        mq, mk, mv, mo, dq, dk, dv, gates, ifg = _inproj(
            xt, row(norm_mix_g[l]), w_all, m_conv_w[l].astype(F32), row(m_conv_b[l]), bif, seq)

        gm = _mlstm(mq, mk, mv, mo, ifg, row(m_norm_g[l]), batch, seq)

        lam_init = 0.8 - 0.6 * math.exp(-0.3 * l)
        hd = _attention(rel_bias.astype(F32), da_lambda[l].astype(F32), dq, dk, dv, row(da_norm_g[l]),
                        batch, seq, lam_init)

        xt = _merge(gm, hd, gates, xt, w_br_m[l].astype(BF16), w_br_d[l].astype(BF16), w_out[l].astype(BF16))

        xt = _ffn(xt, p[l].reshape(T, PLE_DIM), row(norm_ffn_g[l]), w_up[l].astype(BF16),
                  ffn_conv_w[l].astype(F32), row(ffn_conv_b[l]), w_down[l].astype(BF16), row(norm_ple_g[l]),
                  w_ple_gate[l].astype(BF16), w_ple[l].astype(BF16), row(norm_final_g), seq,
                  final_norm=(l == depth - 1))

    return xt.reshape(batch, seq, D_MODEL)
```

```python
import functools
import math

import jax
import jax.numpy as jnp
from jax import lax
from jax.experimental import pallas as pl
from jax.experimental.pallas import tpu as pltpu

D_MODEL = 1024
PLE_DIM = 256
M_HEADS = 4
M_HEAD_DIM = 128
M_WIDTH = M_HEADS * M_HEAD_DIM
M_CONV = 4
DA_HEADS = 4
DA_HEAD_DIM = 64
DA_V_DIM = 2 * DA_HEAD_DIM
DA_WIDTH = DA_HEADS * DA_V_DIM
REL_BUCKETS = 32
REL_MAX_DIST = 128
D_FF = 2816
FFN_CONV = 3
EPS = 1e-6
NEG_BIG = -1e30
LOG2E = 1.4426950408889634

LANES = 128
SUBLANES = 8
VMEM_LIMIT = 56 * 1024 * 1024

TOKEN_TILE = 512
M_CHUNK = 256
ATT_BLOCK = 512
ATT_SUB = ATT_BLOCK // 2
FF_CHUNK = 256
N_FF_CHUNKS = D_FF // FF_CHUNK

F32 = jnp.float32
BF16 = jnp.bfloat16


def _dot(a, b):
    return jnp.dot(a, b, preferred_element_type=F32)


def _dot_nt(a, b):
    return lax.dot_general(a, b, (((1,), (1,)), ((), ())), preferred_element_type=F32)


def _dot_tn(a, b):
    return lax.dot_general(a, b, (((0,), (0,)), ((), ())), preferred_element_type=F32)


def _rms(x, g):
    return x * lax.rsqrt(jnp.mean(x * x, -1, keepdims=True) + EPS) * g


def _sigmoid(x):
    return 1.0 / (1.0 + jnp.exp(-x))


def _shift_rows(u, prev, k):
    sub = lax.broadcasted_iota(jnp.int32, (SUBLANES, u.shape[1]), 0)
    r = pltpu.roll(u, k, axis=0)
    head = jnp.where(sub < k, pltpu.roll(prev, k, axis=0), r[:SUBLANES])
    return jnp.concatenate([head, r[SUBLANES:]], axis=0)


def _const_spec(shape):
    nd = len(shape)
    return pl.BlockSpec(shape, lambda *_: (0,) * nd, pipeline_mode=pl.Buffered(1))


def _inproj_kernel(x_ref, g_ref, w_ref, cw_ref, cb_ref, bif_ref,
                   mq_ref, mk_ref, mv_ref, mo_ref, dq_ref, dk_ref, dv_ref, gates_ref, ifg_ref,
                   carry_ref, *, tiles_per_seq):
    tm = x_ref.shape[0]
    i = pl.program_id(0)
    hb = _rms(x_ref[...], g_ref[...]).astype(BF16)

    @pl.when(i % tiles_per_seq == 0)
    def _():
        carry_ref[...] = jnp.zeros(carry_ref.shape, F32)

    def conv_silu(cols):
        u = _dot(hb, w_ref[:, cols])
        prev = carry_ref[:, cols]
        carry_ref[:, cols] = u[tm - SUBLANES:, :]
        y = cb_ref[:, cols] + cw_ref[M_CONV - 1:M_CONV, cols] * u
        for k in range(1, M_CONV):
            y = y + cw_ref[M_CONV - 1 - k:M_CONV - k, cols] * _shift_rows(u, prev, k)
        return y * _sigmoid(y)

    mq_ref[...] = conv_silu(slice(0, M_WIDTH)).astype(BF16)
    mk_ref[...] = (conv_silu(slice(M_WIDTH, 2 * M_WIDTH)) * (M_HEAD_DIM ** -0.5)).astype(BF16)

    c = 2 * M_WIDTH
    mv_ref[...] = _dot(hb, w_ref[:, c:c + M_WIDTH]).astype(BF16); c += M_WIDTH
    mo_ref[...] = _dot(hb, w_ref[:, c:c + M_WIDTH]).astype(BF16); c += M_WIDTH
    dq_ref[...] = (_dot(hb, w_ref[:, c:c + DA_WIDTH]) * (LOG2E * DA_HEAD_DIM ** -0.5)).astype(BF16); c += DA_WIDTH
    dk_ref[...] = _dot(hb, w_ref[:, c:c + DA_WIDTH]).astype(BF16); c += DA_WIDTH
    dv_ref[...] = _dot(hb, w_ref[:, c:c + DA_WIDTH]).astype(BF16); c += DA_WIDTH
    for j in range(4):
        gates_ref[:, j * 512:(j + 1) * 512] = _dot(hb, w_ref[:, c:c + 512]).astype(BF16); c += 512
    ifg_ref[...] = _dot(hb, w_ref[:, c:c + LANES]) + bif_ref[...]


def _inproj(x2, g, w_all, cw, cb, bif, seq):
    T = x2.shape[0]
    tm = TOKEN_TILE
    n_cols = w_all.shape[1]
    row = lambda i: (i, 0)
    bf = lambda n: jax.ShapeDtypeStruct((T, n), BF16)
    out_shapes = [bf(M_WIDTH)] * 4 + [bf(DA_WIDTH)] * 3 + [bf(2 * D_MODEL), jax.ShapeDtypeStruct((T, LANES), F32)]
    out_specs = [pl.BlockSpec((tm, M_WIDTH), row)] * 7 + [pl.BlockSpec((tm, 2 * D_MODEL), row),
                                                          pl.BlockSpec((tm, LANES), row)]
    return pl.pallas_call(
        functools.partial(_inproj_kernel, tiles_per_seq=seq // tm),
        grid=(T // tm,),
        in_specs=[pl.BlockSpec((tm, D_MODEL), row), _const_spec((1, D_MODEL)), _const_spec((D_MODEL, n_cols)),
                  _const_spec((M_CONV, 2 * M_WIDTH)), _const_spec((1, 2 * M_WIDTH)), _const_spec((1, LANES))],
        out_specs=out_specs,
        out_shape=out_shapes,
        scratch_shapes=[pltpu.VMEM((SUBLANES, 2 * M_WIDTH), F32)],
        compiler_params=pltpu.CompilerParams(dimension_semantics=("arbitrary",), vmem_limit_bytes=VMEM_LIMIT),
        name="inproj",
    )(x2, g, w_all, cw, cb, bif)


def _cumsum_rows(tril, x):
    hi = x.astype(BF16)
    r1 = x - hi.astype(F32)
    mid = r1.astype(BF16)
    lo = (r1 - mid.astype(F32)).astype(BF16)
    return _dot(tril, hi) + _dot(tril, mid) + _dot(tril, lo)


def _mlstm_kernel(mq_ref, mk_ref, mv_ref, mo_ref, ifg_ref, g_ref, out_ref, ct_ref, m_ref):
    L = mq_ref.shape[0]
    d = M_HEAD_DIM

    @pl.when(pl.program_id(1) == 0)
    def _():
        ct_ref[...] = jnp.zeros(ct_ref.shape, F32)
        m_ref[...] = jnp.zeros(m_ref.shape, F32)

    ig = ifg_ref[...]
    fg = pltpu.roll(ig, LANES - M_HEADS, axis=1)
    logf = jnp.minimum(fg, 0.0) - jnp.log1p(jnp.exp(-jnp.abs(fg)))
    kidx = lax.broadcasted_iota(jnp.int32, (L, L), 0)
    qidx = lax.broadcasted_iota(jnp.int32, (L, L), 1)
    b = _cumsum_rows(jnp.where(kidx >= qidx, 1.0, 0.0).astype(BF16), logf)
    b_tot = b[L - 1:L, :]
    u = ig - b
    w_end = b_tot + u
    m_loc = jnp.max(w_end, axis=0, keepdims=True)
    a = jnp.exp(w_end - m_loc)
    m_prev = m_ref[...]
    m_new = jnp.maximum(b_tot + m_prev, m_loc)
    s_old = jnp.exp(b_tot + m_prev - m_new)
    s_loc = jnp.exp(m_loc - m_new)
    li_t = (b + m_prev).T
    b_t = b.T
    causal = kidx <= qidx

    for h in range(M_HEADS):
        hs = slice(h * d, (h + 1) * d)
        q = mq_ref[:, hs]
        k = mk_ref[:, hs]
        v = mv_ref[:, hs]
        ct_prev = ct_ref[h]
        s_t = _dot_nt(k, q)
        inter = _dot_nt(ct_prev.astype(BF16), q)
        log_d = jnp.where(causal, u[:, h:h + 1] + b_t[h:h + 1, :], -jnp.inf)
        li = li_t[h:h + 1, :]
        m_t = jnp.maximum(li, jnp.max(log_d, axis=0, keepdims=True))
        p_t = s_t * jnp.exp(log_d - m_t)
        s_inter = jnp.exp(li - m_t)
        num = s_inter * inter[:d] + _dot_tn(v, p_t.astype(BF16))
        den = s_inter * inter[d:d + 1] + jnp.sum(p_t, axis=0, keepdims=True)
        hh = num / jnp.maximum(jnp.abs(den), jnp.exp(-m_t))
        y = (hh * lax.rsqrt(jnp.mean(hh * hh, axis=0, keepdims=True) + EPS)).T
        out_ref[:, hs] = (_sigmoid(mo_ref[:, hs].astype(F32)) * (y * g_ref[:, hs])).astype(BF16)

        ak = a[:, h:h + 1] * k.astype(F32)
        so = s_old[:, h:h + 1]
        sl = s_loc[:, h:h + 1]
        ct_ref[h, 0:d, :] = so * ct_prev[:d] + sl * _dot_tn(v, ak.astype(BF16))
        ct_ref[h, d:d + 1, :] = so * ct_prev[d:d + 1] + sl * jnp.sum(ak, axis=0, keepdims=True)

    m_ref[...] = m_new


def _mlstm(mq, mk, mv, mo, ifg, g, batch, seq):
    T = mq.shape[0]
    L = M_CHUNK
    nc = seq // L
    row = lambda b, c: (b * nc + c, 0)
    blk = pl.BlockSpec((L, M_WIDTH), row)
    return pl.pallas_call(
        _mlstm_kernel,
        grid=(batch, nc),
        in_specs=[blk, blk, blk, blk, pl.BlockSpec((L, LANES), row), _const_spec((1, M_WIDTH))],
        out_specs=blk,
        out_shape=jax.ShapeDtypeStruct((T, M_WIDTH), BF16),
        scratch_shapes=[pltpu.VMEM((M_HEADS, M_HEAD_DIM + 2 * SUBLANES, M_HEAD_DIM), F32),
                        pltpu.VMEM((1, LANES), F32)],
        compiler_params=pltpu.CompilerParams(dimension_semantics=("arbitrary", "arbitrary"),
                                             vmem_limit_bytes=VMEM_LIMIT),
        name="mlstm",
    )(mq, mk, mv, mo, ifg, g)


def _t5_bucket(n):
    max_exact = REL_BUCKETS // 2
    nf = jnp.maximum(n, 1).astype(F32)
    large = max_exact + (jnp.log(nf / max_exact) / math.log(REL_MAX_DIST / max_exact)
                         * (REL_BUCKETS - max_exact)).astype(jnp.int32)
    large = jnp.minimum(large, REL_BUCKETS - 1)
    return jnp.where(n < max_exact, n, large)


def _attn_kernel(rb_ref, lam_ref, q_ref, k_ref, v_ref, g_ref, out_ref,
                 bias_ref, m_ref, l_ref, acc_ref, *, lam_init):
    tq = q_ref.shape[0]
    tk = tq
    sb = ATT_SUB
    h = pl.program_id(1)
    qi = pl.program_id(2)

    kk = lax.broadcasted_iota(jnp.int32, (sb, sb), 0)
    qq = lax.broadcasted_iota(jnp.int32, (sb, sb), 1)

    @pl.when((pl.program_id(0) == 0) & (h == 0) & (qi == 0))
    def _():
        for delta in range(2):
            bucket = _t5_bucket(jnp.maximum(qq + delta * sb - kk, 0))
            for hh in range(DA_HEADS):
                t = jnp.zeros((sb, sb), F32)
                for j in range(REL_BUCKETS):
                    t = jnp.where(bucket == j, rb_ref[j, hh], t)
                bias_ref[hh, delta] = (t - rb_ref[REL_BUCKETS - 1, hh]) * LOG2E

    lv = lam_ref[...]
    lam = (jnp.exp(jnp.sum(lv[0:1] * lv[1:2], axis=1, keepdims=True))
           - jnp.exp(jnp.sum(lv[2:3] * lv[3:4], axis=1, keepdims=True)) + lam_init)

    q = q_ref[...]
    lane = lax.broadcasted_iota(jnp.int32, q.shape, 1)
    zero = jnp.zeros_like(q)
    qcat = jnp.concatenate([jnp.where(lane < DA_HEAD_DIM, q, zero),
                            jnp.where(lane >= DA_HEAD_DIM, q, zero)], axis=0)

    m_ref[...] = jnp.full(m_ref.shape, -jnp.inf, F32)
    l_ref[...] = jnp.zeros(l_ref.shape, F32)
    acc_ref[...] = jnp.zeros(acc_ref.shape, F32)

    n_groups = 2 * tq // sb

    d0 = bias_ref[h, 0]
    d1 = bias_ref[h, 1]
    causal = qq >= kk
    ident = lambda s: s
    near = lambda s: jnp.concatenate([s[:sb], s[sb:] + d1], axis=0)
    early = lambda s: jnp.where(causal, s + d0, NEG_BIG)
    late = lambda s: jnp.concatenate([s[:sb] + d1, jnp.where(causal, s[sb:] + d0, NEG_BIG)], axis=0)

    def plan(j, n):
        if j == n - 1:
            return [(sb, early), (tk, late)] * (n_groups // 2)
        if j == n - 2:
            return [(tk, near), (tk, ident)] * (n_groups // 2)
        return [(tk, ident)] * n_groups

    def scores(j, n):
        k = k_ref[j * tk:(j + 1) * tk, :]
        return [_dot_nt(k[:rows], qcat[g * sb:(g + 1) * sb]) for g, (rows, _) in enumerate(plan(j, n))]

    def softmax_pv(j, n, sc):
        v = v_ref[j * tk:(j + 1) * tk, :]
        for g, (rows, fix) in enumerate(plan(j, n)):
            cols = slice(g * sb, (g + 1) * sb)
            s = fix(sc[g])
            m_prev = m_ref[:, cols]
            m_new = jnp.maximum(m_prev, jnp.max(s, axis=0, keepdims=True))
            alpha = jnp.exp2(m_prev - m_new)
            p = jnp.exp2(s - m_new)
            l_ref[:, cols] = alpha * l_ref[:, cols] + jnp.sum(p, axis=0, keepdims=True)
            acc_ref[:, cols] = alpha * acc_ref[:, cols] + _dot_tn(v[:rows], p.astype(BF16))
            m_ref[:, cols] = m_new

    def sweep(n):
        sc = scores(0, n)
        for j in range(n):
            nxt = scores(j + 1, n) if j + 1 < n else None
            softmax_pv(j, n, sc)
            sc = nxt

    for n in range(1, k_ref.shape[0] // tk + 1):
        pl.when(qi == n - 1)(functools.partial(sweep, n))

    o = acc_ref[...] * (1.0 / l_ref[...])
    out = (o[:, :tq] - lam * o[:, tq:]).T
    out_ref[...] = (_rms(out, g_ref[...]) * (1.0 - lam_init)).astype(BF16)


def _attention(rel_bias, lam_vec, dq, dk, dv, g, batch, seq, lam_init):
    T = dq.shape[0]
    tq = ATT_BLOCK
    nq = seq // tq
    return pl.pallas_call(
        functools.partial(_attn_kernel, lam_init=lam_init),
        grid=(batch, DA_HEADS, nq),
        in_specs=[pl.BlockSpec(memory_space=pltpu.SMEM),
                  pl.BlockSpec((4, DA_HEAD_DIM), lambda b, h, i: (0, 0)),
                  pl.BlockSpec((tq, DA_V_DIM), lambda b, h, i: (b * nq + i, h)),
                  pl.BlockSpec((seq, DA_V_DIM), lambda b, h, i: (b, h)),
                  pl.BlockSpec((seq, DA_V_DIM), lambda b, h, i: (b, h)),
                  pl.BlockSpec((1, DA_V_DIM), lambda b, h, i: (0, h))],
        out_specs=pl.BlockSpec((tq, DA_V_DIM), lambda b, h, i: (b * nq + i, h)),
        out_shape=jax.ShapeDtypeStruct((T, DA_WIDTH), BF16),
        scratch_shapes=[pltpu.VMEM((DA_HEADS, 2, ATT_SUB, ATT_SUB), F32),
                        pltpu.VMEM((1, 2 * tq), F32),
                        pltpu.VMEM((1, 2 * tq), F32),
                        pltpu.VMEM((DA_V_DIM, 2 * tq), F32)],
        compiler_params=pltpu.CompilerParams(dimension_semantics=("arbitrary",) * 3,
                                             vmem_limit_bytes=VMEM_LIMIT),
        name="diffattn",
    )(rel_bias, lam_vec, dq, dk, dv, g)


def _merge_kernel(gm_ref, hd_ref, gates_ref, x_ref, wm_ref, wd_ref, wo_ref, out_ref):
    ya = _dot(gm_ref[...], wm_ref[...])
    yb = _dot(hd_ref[...], wd_ref[...])
    ga = _sigmoid(gates_ref[:, :D_MODEL].astype(F32))
    gb = _sigmoid(gates_ref[:, D_MODEL:].astype(F32))
    mixed = (ga * ya + gb * yb).astype(BF16)
    out_ref[...] = x_ref[...] + _dot(mixed, wo_ref[...])


def _merge(gm, hd, gates, x2, wm, wd, wo):
    T = x2.shape[0]
    tm = TOKEN_TILE
    row = lambda i: (i, 0)
    return pl.pallas_call(
        _merge_kernel,
        grid=(T // tm,),
        in_specs=[pl.BlockSpec((tm, M_WIDTH), row), pl.BlockSpec((tm, DA_WIDTH), row),
                  pl.BlockSpec((tm, 2 * D_MODEL), row), pl.BlockSpec((tm, D_MODEL), row),
                  _const_spec((M_WIDTH, D_MODEL)), _const_spec((DA_WIDTH, D_MODEL)),
                  _const_spec((D_MODEL, D_MODEL))],
        out_specs=pl.BlockSpec((tm, D_MODEL), row),
        out_shape=jax.ShapeDtypeStruct((T, D_MODEL), F32),
        compiler_params=pltpu.CompilerParams(dimension_semantics=("arbitrary",), vmem_limit_bytes=VMEM_LIMIT),
        name="merge",
    )(gm, hd, gates, x2, wm, wd, wo)


def _gelu_tanh(x):
    return 0.5 * x * (1.0 + jnp.tanh(math.sqrt(2.0 / math.pi) * (x + 0.044715 * (x * x * x))))


def _ffn_kernel(x_ref, p_ref, gf_ref, wup_ref, cw_ref, cb_ref, wdn_ref, gp_ref, wpg_ref, wp_ref, gl_ref,
                out_ref, carry_ref, acc_ref, *, tiles_per_seq, final_norm):
    tm = x_ref.shape[0]
    i = pl.program_id(0)
    x1 = x_ref[...]
    hb = _rms(x1, gf_ref[...]).astype(BF16)

    @pl.when(i % tiles_per_seq == 0)
    def _():
        carry_ref[...] = jnp.zeros(carry_ref.shape, F32)

    def up(c):
        lo = c * FF_CHUNK
        return [_dot(hb, wup_ref[:, off + lo:off + lo + FF_CHUNK]) for off in (0, D_FF)]

    def conv(c, u, off):
        cols = slice(off + c * FF_CHUNK, off + (c + 1) * FF_CHUNK)
        prev = carry_ref[:, cols]
        carry_ref[:, cols] = u[tm - SUBLANES:, :]
        y = cb_ref[:, cols] + cw_ref[FFN_CONV - 1:FFN_CONV, cols] * u
        for k in range(1, FFN_CONV):
            y = y + cw_ref[FFN_CONV - 1 - k:FFN_CONV - k, cols] * _shift_rows(u, prev, k)
        return y

    u_next = up(0)
    for c in range(N_FF_CHUNKS):
        u_val, u_gate = u_next
        if c + 1 < N_FF_CHUNKS:
            u_next = up(c + 1)
        act = (_gelu_tanh(conv(c, u_gate, D_FF)) * conv(c, u_val, 0)).astype(BF16)
        d = _dot(act, wdn_ref[c * FF_CHUNK:(c + 1) * FF_CHUNK, :])
        if c == 0:
            acc_ref[...] = d
        else:
            acc_ref[...] += d

    x2 = x1 + acc_ref[...]
    hg = _rms(x2, gp_ref[...]).astype(BF16)
    gate = _sigmoid(_dot(hg, wpg_ref[...]))
    pe = _dot(p_ref[...].astype(BF16), wp_ref[...])
    x3 = x2 + gate * pe
    out_ref[...] = _rms(x3, gl_ref[...]) if final_norm else x3


def _ffn(x1, p2, gf, wup, cw, cb, wdn, gp, wpg, wp, gl, seq, final_norm):
    T = x1.shape[0]
    tm = TOKEN_TILE
    row = lambda i: (i, 0)
    return pl.pallas_call(
        functools.partial(_ffn_kernel, tiles_per_seq=seq // tm, final_norm=final_norm),
        grid=(T // tm,),
        in_specs=[pl.BlockSpec((tm, D_MODEL), row), pl.BlockSpec((tm, PLE_DIM), row),
                  _const_spec((1, D_MODEL)),
                  _const_spec((D_MODEL, 2 * D_FF)), _const_spec((FFN_CONV, 2 * D_FF)),
                  _const_spec((1, 2 * D_FF)), _const_spec((D_FF, D_MODEL)),
                  _const_spec((1, D_MODEL)), _const_spec((D_MODEL, D_MODEL)),
                  _const_spec((PLE_DIM, D_MODEL)), _const_spec((1, D_MODEL))],
        out_specs=pl.BlockSpec((tm, D_MODEL), row),
        out_shape=jax.ShapeDtypeStruct((T, D_MODEL), F32),
        scratch_shapes=[pltpu.VMEM((SUBLANES, 2 * D_FF), F32),
                        pltpu.VMEM((tm, D_MODEL), F32)],
        compiler_params=pltpu.CompilerParams(dimension_semantics=("arbitrary",), vmem_limit_bytes=VMEM_LIMIT),
        name="convffn",
    )(x1, p2, gf, wup, cw, cb, wdn, gp, wpg, wp, gl)


def kernel(x, p, rel_bias, norm_mix_g, w_in, b_if, m_conv_w, m_conv_b, m_norm_g, da_lambda, da_norm_g,
           w_br_m, w_br_d, w_out, norm_ffn_g, w_up, ffn_conv_w, ffn_conv_b, w_down, norm_ple_g, w_ple_gate,
           w_ple, norm_final_g):
    batch, seq, _ = x.shape
    depth = w_in.shape[0]
    T = batch * seq
    assert seq % TOKEN_TILE == 0 and seq % M_CHUNK == 0 and seq % ATT_BLOCK == 0
    xt = x.reshape(T, D_MODEL)
    row = lambda v: v.reshape(1, -1).astype(F32)

    for l in range(depth):
        wi = w_in[l].astype(BF16)
        n_if = 2 * M_HEADS
        c_if = 4 * M_WIDTH
        w_all = jnp.concatenate(
            [wi[:, :c_if], wi[:, c_if + n_if:], wi[:, c_if:c_if + n_if],
             jnp.zeros((D_MODEL, LANES - n_if), BF16)], axis=1)
        bif = jnp.concatenate([b_if[l].astype(F32), jnp.zeros((LANES - n_if,), F32)]).reshape(1, LANES)

        mq, mk, mv, mo, dq, dk, dv, gates, ifg = _inproj(
            xt, row(norm_mix_g[l]), w_all, m_conv_w[l].astype(F32), row(m_conv_b[l]), bif, seq)

        gm = _mlstm(mq, mk, mv, mo, ifg, row(m_norm_g[l]), batch, seq)

        lam_init = 0.8 - 0.6 * math.exp(-0.3 * l)
        hd = _attention(rel_bias.astype(F32), da_lambda[l].astype(F32), dq, dk, dv, row(da_norm_g[l]),
                        batch, seq, lam_init)

        xt = _merge(gm, hd, gates, xt, w_br_m[l].astype(BF16), w_br_d[l].astype(BF16), w_out[l].astype(BF16))

        xt = _ffn(xt, p[l].reshape(T, PLE_DIM), row(norm_ffn_g[l]), w_up[l].astype(BF16),
                  ffn_conv_w[l].astype(F32), row(ffn_conv_b[l]), w_down[l].astype(BF16), row(norm_ple_g[l]),
                  w_ple_gate[l].astype(BF16), w_ple[l].astype(BF16), row(norm_final_g), seq,
                  final_norm=(l == depth - 1))

    return xt.reshape(batch, seq, D_MODEL)
```

```python
import functools
import math

import jax
import jax.numpy as jnp
from jax import lax
from jax.experimental import pallas as pl
from jax.experimental.pallas import tpu as pltpu

D_MODEL = 1024
PLE_DIM = 256
M_HEADS = 4
M_HEAD_DIM = 128
M_WIDTH = M_HEADS * M_HEAD_DIM
M_CONV = 4
DA_HEADS = 4
DA_HEAD_DIM = 64
DA_V_DIM = 2 * DA_HEAD_DIM
DA_WIDTH = DA_HEADS * DA_V_DIM
REL_BUCKETS = 32
REL_MAX_DIST = 128
D_FF = 2816
FFN_CONV = 3
EPS = 1e-6
NEG_BIG = -1e30
LOG2E = 1.4426950408889634

LANES = 128
SUBLANES = 8
VMEM_LIMIT = 56 * 1024 * 1024

TOKEN_TILE = 512
M_CHUNK = 256
ATT_BLOCK = 512
ATT_SUB = ATT_BLOCK // 2
FF_CHUNK = 256
N_FF_CHUNKS = D_FF // FF_CHUNK
FF_SLABS = 2 * FF_CHUNK // LANES
FF_LOOKAHEAD = 3
FF_SLAB_SETS = FF_LOOKAHEAD + 1

F32 = jnp.float32
BF16 = jnp.bfloat16


def _dot(a, b):
    return jnp.dot(a, b, preferred_element_type=F32)


def _dot_nt(a, b):
    return lax.dot_general(a, b, (((1,), (1,)), ((), ())), preferred_element_type=F32)


def _dot_tn(a, b):
    return lax.dot_general(a, b, (((0,), (0,)), ((), ())), preferred_element_type=F32)


def _rms(x, g):
    return x * lax.rsqrt(jnp.mean(x * x, -1, keepdims=True) + EPS) * g


def _sigmoid(x):
    return 1.0 / (1.0 + jnp.exp(-x))


def _slab_store(u, slab_ref, slab0):
    tm, n = u.shape
    for j in range(n // LANES):
        slab_ref[slab0 + j, SUBLANES:SUBLANES + tm, :] = u[:, j * LANES:(j + 1) * LANES]


def _slab_conv(tm, n, col0, slab_ref, slab0, carry_ref, cw_ref, cb_ref):
    taps = cw_ref.shape[0]
    outs = []
    for j in range(n // LANES):
        cols = slice(col0 + j * LANES, col0 + (j + 1) * LANES)
        s = slab0 + j
        slab_ref[s, 0:SUBLANES, :] = carry_ref[:, cols]
        carry_ref[:, cols] = slab_ref[s, tm:tm + SUBLANES, :]
        y = cb_ref[:, cols]
        for k in range(taps):
            y = y + cw_ref[taps - 1 - k:taps - k, cols] * slab_ref[s, pl.ds(SUBLANES - k, tm), :]
        outs.append(y)
    return jnp.concatenate(outs, axis=1)


def _const_spec(shape):
    nd = len(shape)
    return pl.BlockSpec(shape, lambda *_: (0,) * nd, pipeline_mode=pl.Buffered(1))


def _inproj_kernel(x_ref, g_ref, w_ref, cw_ref, cb_ref, bif_ref,
                   mq_ref, mk_ref, mv_ref, mo_ref, dq_ref, dk_ref, dv_ref, gates_ref, ifg_ref,
                   carry_ref, slab_ref, *, tiles_per_seq):
    tm = x_ref.shape[0]
    i = pl.program_id(0)
    hb = _rms(x_ref[...], g_ref[...]).astype(BF16)

    @pl.when(i % tiles_per_seq == 0)
    def _():
        carry_ref[...] = jnp.zeros(carry_ref.shape, F32)

    def conv_silu(col0):
        _slab_store(_dot(hb, w_ref[:, col0:col0 + M_WIDTH]), slab_ref, col0 // LANES)
        y = _slab_conv(tm, M_WIDTH, col0, slab_ref, col0 // LANES, carry_ref, cw_ref, cb_ref)
        return y * _sigmoid(y)

    mq_ref[...] = conv_silu(0).astype(BF16)
    mk_ref[...] = (conv_silu(M_WIDTH) * (M_HEAD_DIM ** -0.5)).astype(BF16)

    c = 2 * M_WIDTH
    mv_ref[...] = _dot(hb, w_ref[:, c:c + M_WIDTH]).astype(BF16); c += M_WIDTH
    mo_ref[...] = _dot(hb, w_ref[:, c:c + M_WIDTH]).astype(BF16); c += M_WIDTH
    dq_ref[...] = (_dot(hb, w_ref[:, c:c + DA_WIDTH]) * (LOG2E * DA_HEAD_DIM ** -0.5)).astype(BF16); c += DA_WIDTH
    dk_ref[...] = _dot(hb, w_ref[:, c:c + DA_WIDTH]).astype(BF16); c += DA_WIDTH
    dv_ref[...] = _dot(hb, w_ref[:, c:c + DA_WIDTH]).astype(BF16); c += DA_WIDTH
    for j in range(4):
        gates_ref[:, j * 512:(j + 1) * 512] = _dot(hb, w_ref[:, c:c + 512]).astype(BF16); c += 512
    ifg_ref[...] = _dot(hb, w_ref[:, c:c + LANES]) + bif_ref[...]


def _inproj(x2, g, w_all, cw, cb, bif, seq):
    T = x2.shape[0]
    tm = TOKEN_TILE
    n_cols = w_all.shape[1]
    row = lambda i: (i, 0)
    bf = lambda n: jax.ShapeDtypeStruct((T, n), BF16)
    out_shapes = [bf(M_WIDTH)] * 4 + [bf(DA_WIDTH)] * 3 + [bf(2 * D_MODEL), jax.ShapeDtypeStruct((T, LANES), F32)]
    out_specs = [pl.BlockSpec((tm, M_WIDTH), row)] * 7 + [pl.BlockSpec((tm, 2 * D_MODEL), row),
                                                          pl.BlockSpec((tm, LANES), row)]
    return pl.pallas_call(
        functools.partial(_inproj_kernel, tiles_per_seq=seq // tm),
        grid=(T // tm,),
        in_specs=[pl.BlockSpec((tm, D_MODEL), row), _const_spec((1, D_MODEL)), _const_spec((D_MODEL, n_cols)),
                  _const_spec((M_CONV, 2 * M_WIDTH)), _const_spec((1, 2 * M_WIDTH)), _const_spec((1, LANES))],
        out_specs=out_specs,
        out_shape=out_shapes,
        scratch_shapes=[pltpu.VMEM((SUBLANES, 2 * M_WIDTH), F32),
                        pltpu.VMEM((2 * M_WIDTH // LANES, SUBLANES + tm, LANES), F32)],
        compiler_params=pltpu.CompilerParams(dimension_semantics=("arbitrary",), vmem_limit_bytes=VMEM_LIMIT),
        name="inproj",
    )(x2, g, w_all, cw, cb, bif)


def _cumsum_rows(tril, x):
    hi = x.astype(BF16)
    r1 = x - hi.astype(F32)
    mid = r1.astype(BF16)
    lo = (r1 - mid.astype(F32)).astype(BF16)
    return _dot(tril, hi) + _dot(tril, mid) + _dot(tril, lo)


def _mlstm_kernel(mq_ref, mk_ref, mv_ref, mo_ref, ifg_ref, g_ref, out_ref, ct_ref, m_ref):
    L = mq_ref.shape[0]
    d = M_HEAD_DIM

    @pl.when(pl.program_id(1) == 0)
    def _():
        ct_ref[...] = jnp.zeros(ct_ref.shape, F32)
        m_ref[...] = jnp.zeros(m_ref.shape, F32)

    ig = ifg_ref[...]
    fg = pltpu.roll(ig, LANES - M_HEADS, axis=1)
    logf = jnp.minimum(fg, 0.0) - jnp.log1p(jnp.exp(-jnp.abs(fg)))
    kidx = lax.broadcasted_iota(jnp.int32, (L, L), 0)
    qidx = lax.broadcasted_iota(jnp.int32, (L, L), 1)
    b = _cumsum_rows(jnp.where(kidx >= qidx, 1.0, 0.0).astype(BF16), logf)
    b_tot = b[L - 1:L, :]
    u = ig - b
    w_end = b_tot + u
    m_loc = jnp.max(w_end, axis=0, keepdims=True)
    a = jnp.exp(w_end - m_loc)
    m_prev = m_ref[...]
    m_new = jnp.maximum(b_tot + m_prev, m_loc)
    s_old = jnp.exp(b_tot + m_prev - m_new)
    s_loc = jnp.exp(m_loc - m_new)
    li_t = (b + m_prev).T
    b_t = b.T
    causal = kidx <= qidx

    for h in range(M_HEADS):
        hs = slice(h * d, (h + 1) * d)
        q = mq_ref[:, hs]
        k = mk_ref[:, hs]
        v = mv_ref[:, hs]
        ct_prev = ct_ref[h]
        s_t = _dot_nt(k, q)
        inter = _dot_nt(ct_prev.astype(BF16), q)
        log_d = jnp.where(causal, u[:, h:h + 1] + b_t[h:h + 1, :], -jnp.inf)
        li = li_t[h:h + 1, :]
        m_t = jnp.maximum(li, jnp.max(log_d, axis=0, keepdims=True))
        p_t = s_t * jnp.exp(log_d - m_t)
        s_inter = jnp.exp(li - m_t)
        num = s_inter * inter[:d] + _dot_tn(v, p_t.astype(BF16))
        den = s_inter * inter[d:d + 1] + jnp.sum(p_t, axis=0, keepdims=True)
        hh = num / jnp.maximum(jnp.abs(den), jnp.exp(-m_t))
        y = (hh * lax.rsqrt(jnp.mean(hh * hh, axis=0, keepdims=True) + EPS)).T
        out_ref[:, hs] = (_sigmoid(mo_ref[:, hs].astype(F32)) * (y * g_ref[:, hs])).astype(BF16)

        ak = a[:, h:h + 1] * k.astype(F32)
        so = s_old[:, h:h + 1]
        sl = s_loc[:, h:h + 1]
        ct_ref[h, 0:d, :] = so * ct_prev[:d] + sl * _dot_tn(v, ak.astype(BF16))
        ct_ref[h, d:d + 1, :] = so * ct_prev[d:d + 1] + sl * jnp.sum(ak, axis=0, keepdims=True)

    m_ref[...] = m_new


def _mlstm(mq, mk, mv, mo, ifg, g, batch, seq):
    T = mq.shape[0]
    L = M_CHUNK
    nc = seq // L
    row = lambda b, c: (b * nc + c, 0)
    blk = pl.BlockSpec((L, M_WIDTH), row)
    return pl.pallas_call(
        _mlstm_kernel,
        grid=(batch, nc),
        in_specs=[blk, blk, blk, blk, pl.BlockSpec((L, LANES), row), _const_spec((1, M_WIDTH))],
        out_specs=blk,
        out_shape=jax.ShapeDtypeStruct((T, M_WIDTH), BF16),
        scratch_shapes=[pltpu.VMEM((M_HEADS, M_HEAD_DIM + 2 * SUBLANES, M_HEAD_DIM), F32),
                        pltpu.VMEM((1, LANES), F32)],
        compiler_params=pltpu.CompilerParams(dimension_semantics=("arbitrary", "arbitrary"),
                                             vmem_limit_bytes=VMEM_LIMIT),
        name="mlstm",
    )(mq, mk, mv, mo, ifg, g)


def _t5_bucket(n):
    max_exact = REL_BUCKETS // 2
    nf = jnp.maximum(n, 1).astype(F32)
    large = max_exact + (jnp.log(nf / max_exact) / math.log(REL_MAX_DIST / max_exact)
                         * (REL_BUCKETS - max_exact)).astype(jnp.int32)
    large = jnp.minimum(large, REL_BUCKETS - 1)
    return jnp.where(n < max_exact, n, large)


def _attn_kernel(rb_ref, lam_ref, q_ref, k_ref, v_ref, g_ref, out_ref,
                 bias_ref, m_ref, l_ref, acc_ref, *, lam_init):
    tq = q_ref.shape[0]
    tk = tq
    sb = ATT_SUB
    h = pl.program_id(1)
    qi = pl.program_id(2)

    kk = lax.broadcasted_iota(jnp.int32, (sb, sb), 0)
    qq = lax.broadcasted_iota(jnp.int32, (sb, sb), 1)

    @pl.when((pl.program_id(0) == 0) & (h == 0) & (qi == 0))
    def _():
        for delta in range(2):
            bucket = _t5_bucket(jnp.maximum(qq + delta * sb - kk, 0))
            for hh in range(DA_HEADS):
                t = jnp.zeros((sb, sb), F32)
                for j in range(REL_BUCKETS):
                    t = jnp.where(bucket == j, rb_ref[j, hh], t)
                bias_ref[hh, delta] = (t - rb_ref[REL_BUCKETS - 1, hh]) * LOG2E

    lv = lam_ref[...]
    lam = (jnp.exp(jnp.sum(lv[0:1] * lv[1:2], axis=1, keepdims=True))
           - jnp.exp(jnp.sum(lv[2:3] * lv[3:4], axis=1, keepdims=True)) + lam_init)

    q = q_ref[...]
    lane = lax.broadcasted_iota(jnp.int32, q.shape, 1)
    zero = jnp.zeros_like(q)
    qcat = jnp.concatenate([jnp.where(lane < DA_HEAD_DIM, q, zero),
                            jnp.where(lane >= DA_HEAD_DIM, q, zero)], axis=0)

    m_ref[...] = jnp.full(m_ref.shape, -jnp.inf, F32)
    l_ref[...] = jnp.zeros(l_ref.shape, F32)
    acc_ref[...] = jnp.zeros(acc_ref.shape, F32)

    n_groups = 2 * tq // sb

    d0 = bias_ref[h, 0]
    d1 = bias_ref[h, 1]
    causal = qq >= kk
    ident = lambda s: s
    near = lambda s: jnp.concatenate([s[:sb], s[sb:] + d1], axis=0)
    early = lambda s: jnp.where(causal, s + d0, NEG_BIG)
    late = lambda s: jnp.concatenate([s[:sb] + d1, jnp.where(causal, s[sb:] + d0, NEG_BIG)], axis=0)

    def plan(j, n):
        if j == n - 1:
            return [(sb, early), (tk, late)] * (n_groups // 2)
        if j == n - 2:
            return [(tk, near), (tk, ident)] * (n_groups // 2)
        return [(tk, ident)] * n_groups

    def scores(j, n):
        k = k_ref[j * tk:(j + 1) * tk, :]
        return [_dot_nt(k[:rows], qcat[g * sb:(g + 1) * sb]) for g, (rows, _) in enumerate(plan(j, n))]

    def softmax_pv(j, n, sc):
        v = v_ref[j * tk:(j + 1) * tk, :]
        for g, (rows, fix) in enumerate(plan(j, n)):
            cols = slice(g * sb, (g + 1) * sb)
            s = fix(sc[g])
            m_prev = m_ref[:, cols]
            m_new = jnp.maximum(m_prev, jnp.max(s, axis=0, keepdims=True))
            alpha = jnp.exp2(m_prev - m_new)
            p = jnp.exp2(s - m_new)
            l_ref[:, cols] = alpha * l_ref[:, cols] + jnp.sum(p, axis=0, keepdims=True)
            acc_ref[:, cols] = alpha * acc_ref[:, cols] + _dot_tn(v[:rows], p.astype(BF16))
            m_ref[:, cols] = m_new

    def sweep(n):
        sc = scores(0, n)
        for j in range(n):
            nxt = scores(j + 1, n) if j + 1 < n else None
            softmax_pv(j, n, sc)
            sc = nxt

    for n in range(1, k_ref.shape[0] // tk + 1):
        pl.when(qi == n - 1)(functools.partial(sweep, n))

    o = acc_ref[...] * (1.0 / l_ref[...])
    out = (o[:, :tq] - lam * o[:, tq:]).T
    out_ref[...] = (_rms(out, g_ref[...]) * (1.0 - lam_init)).astype(BF16)


def _attention(rel_bias, lam_vec, dq, dk, dv, g, batch, seq, lam_init):
    T = dq.shape[0]
    tq = ATT_BLOCK
    nq = seq // tq
    return pl.pallas_call(
        functools.partial(_attn_kernel, lam_init=lam_init),
        grid=(batch, DA_HEADS, nq),
        in_specs=[pl.BlockSpec(memory_space=pltpu.SMEM),
                  pl.BlockSpec((4, DA_HEAD_DIM), lambda b, h, i: (0, 0)),
                  pl.BlockSpec((tq, DA_V_DIM), lambda b, h, i: (b * nq + i, h)),
                  pl.BlockSpec((seq, DA_V_DIM), lambda b, h, i: (b, h)),
                  pl.BlockSpec((seq, DA_V_DIM), lambda b, h, i: (b, h)),
                  pl.BlockSpec((1, DA_V_DIM), lambda b, h, i: (0, h))],
        out_specs=pl.BlockSpec((tq, DA_V_DIM), lambda b, h, i: (b * nq + i, h)),
        out_shape=jax.ShapeDtypeStruct((T, DA_WIDTH), BF16),
        scratch_shapes=[pltpu.VMEM((DA_HEADS, 2, ATT_SUB, ATT_SUB), F32),
                        pltpu.VMEM((1, 2 * tq), F32),
                        pltpu.VMEM((1, 2 * tq), F32),
                        pltpu.VMEM((DA_V_DIM, 2 * tq), F32)],
        compiler_params=pltpu.CompilerParams(dimension_semantics=("arbitrary",) * 3,
                                             vmem_limit_bytes=VMEM_LIMIT),
        name="diffattn",
    )(rel_bias, lam_vec, dq, dk, dv, g)


def _merge_kernel(gm_ref, hd_ref, gates_ref, x_ref, wm_ref, wd_ref, wo_ref, out_ref):
    ya = _dot(gm_ref[...], wm_ref[...])
    yb = _dot(hd_ref[...], wd_ref[...])
    ga = _sigmoid(gates_ref[:, :D_MODEL].astype(F32))
    gb = _sigmoid(gates_ref[:, D_MODEL:].astype(F32))
    mixed = (ga * ya + gb * yb).astype(BF16)
    out_ref[...] = x_ref[...] + _dot(mixed, wo_ref[...])


def _merge(gm, hd, gates, x2, wm, wd, wo):
    T = x2.shape[0]
    tm = TOKEN_TILE
    row = lambda i: (i, 0)
    return pl.pallas_call(
        _merge_kernel,
        grid=(T // tm,),
        in_specs=[pl.BlockSpec((tm, M_WIDTH), row), pl.BlockSpec((tm, DA_WIDTH), row),
                  pl.BlockSpec((tm, 2 * D_MODEL), row), pl.BlockSpec((tm, D_MODEL), row),
                  _const_spec((M_WIDTH, D_MODEL)), _const_spec((DA_WIDTH, D_MODEL)),
                  _const_spec((D_MODEL, D_MODEL))],
        out_specs=pl.BlockSpec((tm, D_MODEL), row),
        out_shape=jax.ShapeDtypeStruct((T, D_MODEL), F32),
        compiler_params=pltpu.CompilerParams(dimension_semantics=("arbitrary",), vmem_limit_bytes=VMEM_LIMIT),
        name="merge",
    )(gm, hd, gates, x2, wm, wd, wo)


def _gelu_tanh(x):
    return 0.5 * x * (1.0 + jnp.tanh(math.sqrt(2.0 / math.pi) * (x + 0.044715 * (x * x * x))))


def _ffn_kernel(x_ref, p_ref, gf_ref, wup_ref, cw_ref, cb_ref, wdn_ref, gp_ref, wpg_ref, wp_ref, gl_ref,
                out_ref, carry_ref, slab_ref, acc_ref, *, tiles_per_seq, final_norm):
    tm = x_ref.shape[0]
    i = pl.program_id(0)
    x1 = x_ref[...]
    hb = _rms(x1, gf_ref[...]).astype(BF16)

    @pl.when(i % tiles_per_seq == 0)
    def _():
        carry_ref[...] = jnp.zeros(carry_ref.shape, F32)

    def slab0(c, half):
        return (c % FF_SLAB_SETS) * FF_SLABS + half * (FF_CHUNK // LANES)

    def up(c):
        for half, off in enumerate((0, D_FF)):
            lo = off + c * FF_CHUNK
            _slab_store(_dot(hb, wup_ref[:, lo:lo + FF_CHUNK]), slab_ref, slab0(c, half))

    def conv(c, half):
        return _slab_conv(tm, FF_CHUNK, half * D_FF + c * FF_CHUNK, slab_ref, slab0(c, half),
                          carry_ref, cw_ref, cb_ref)

    for c in range(FF_LOOKAHEAD):
        up(c)
    for c in range(N_FF_CHUNKS):
        if c + FF_LOOKAHEAD < N_FF_CHUNKS:
            up(c + FF_LOOKAHEAD)
        act = (_gelu_tanh(conv(c, 1)) * conv(c, 0)).astype(BF16)
        d = _dot(act, wdn_ref[c * FF_CHUNK:(c + 1) * FF_CHUNK, :])
        if c == 0:
            acc_ref[...] = d
        else:
            acc_ref[...] += d

    x2 = x1 + acc_ref[...]
    hg = _rms(x2, gp_ref[...]).astype(BF16)
    gate = _sigmoid(_dot(hg, wpg_ref[...]))
    pe = _dot(p_ref[...].astype(BF16), wp_ref[...])
    x3 = x2 + gate * pe
    out_ref[...] = _rms(x3, gl_ref[...]) if final_norm else x3


def _ffn(x1, p2, gf, wup, cw, cb, wdn, gp, wpg, wp, gl, seq, final_norm):
    T = x1.shape[0]
    tm = TOKEN_TILE
    row = lambda i: (i, 0)
    return pl.pallas_call(
        functools.partial(_ffn_kernel, tiles_per_seq=seq // tm, final_norm=final_norm),
        grid=(T // tm,),
        in_specs=[pl.BlockSpec((tm, D_MODEL), row), pl.BlockSpec((tm, PLE_DIM), row),
                  _const_spec((1, D_MODEL)),
                  _const_spec((D_MODEL, 2 * D_FF)), _const_spec((FFN_CONV, 2 * D_FF)),
                  _const_spec((1, 2 * D_FF)), _const_spec((D_FF, D_MODEL)),
                  _const_spec((1, D_MODEL)), _const_spec((D_MODEL, D_MODEL)),
                  _const_spec((PLE_DIM, D_MODEL)), _const_spec((1, D_MODEL))],
        out_specs=pl.BlockSpec((tm, D_MODEL), row),
        out_shape=jax.ShapeDtypeStruct((T, D_MODEL), F32),
        scratch_shapes=[pltpu.VMEM((SUBLANES, 2 * D_FF), F32),
                        pltpu.VMEM((FF_SLAB_SETS * FF_SLABS, SUBLANES + tm, LANES), F32),
                        pltpu.VMEM((tm, D_MODEL), F32)],
        compiler_params=pltpu.CompilerParams(dimension_semantics=("arbitrary",), vmem_limit_bytes=VMEM_LIMIT),
        name="convffn",
    )(x1, p2, gf, wup, cw, cb, wdn, gp, wpg, wp, gl)


def kernel(x, p, rel_bias, norm_mix_g, w_in, b_if, m_conv_w, m_conv_b, m_norm_g, da_lambda, da_norm_g,
           w_br_m, w_br_d, w_out, norm_ffn_g, w_up, ffn_conv_w, ffn_conv_b, w_down, norm_ple_g, w_ple_gate,
           w_ple, norm_final_g):
    batch, seq, _ = x.shape
    depth = w_in.shape[0]
    T = batch * seq
    assert seq % TOKEN_TILE == 0 and seq % M_CHUNK == 0 and seq % ATT_BLOCK == 0
    xt = x.reshape(T, D_MODEL)
    row = lambda v: v.reshape(1, -1).astype(F32)

    for l in range(depth):
        wi = w_in[l].astype(BF16)
        n_if = 2 * M_HEADS
        c_if = 4 * M_WIDTH
        w_all = jnp.concatenate(
            [wi[:, :c_if], wi[:, c_if + n_if:], wi[:, c_if:c_if + n_if],
             jnp.zeros((D_MODEL, LANES - n_if), BF16)], axis=1)
        bif = jnp.concatenate([b_if[l].astype(F32), jnp.zeros((LANES - n_if,), F32)]).reshape(1, LANES)

        mq, mk, mv, mo, dq, dk, dv, gates, ifg = _inproj(
            xt, row(norm_mix_g[l]), w_all, m_conv_w[l].astype(F32), row(m_conv_b[l]), bif, seq)

        gm = _mlstm(mq, mk, mv, mo, ifg, row(m_norm_g[l]), batch, seq)

        lam_init = 0.8 - 0.6 * math.exp(-0.3 * l)
        hd = _attention(rel_bias.astype(F32), da_lambda[l].astype(F32), dq, dk, dv, row(da_norm_g[l]),
                        batch, seq, lam_init)

        xt = _merge(gm, hd, gates, xt, w_br_m[l].astype(BF16), w_br_d[l].astype(BF16), w_out[l].astype(BF16))

        xt = _ffn(xt, p[l].reshape(T, PLE_DIM), row(norm_ffn_g[l]), w_up[l].astype(BF16),
                  ffn_conv_w[l].astype(F32), row(ffn_conv_b[l]), w_down[l].astype(BF16), row(norm_ple_g[l]),
                  w_ple_gate[l].astype(BF16), w_ple[l].astype(BF16), row(norm_final_g), seq,
                  final_norm=(l == depth - 1))

    return xt.reshape(batch, seq, D_MODEL)
```

```python
import functools
import math

import jax
import jax.numpy as jnp
from jax import lax
from jax.experimental import pallas as pl
from jax.experimental.pallas import tpu as pltpu

D_MODEL = 1024
PLE_DIM = 256
M_HEADS = 4
M_HEAD_DIM = 128
M_WIDTH = M_HEADS * M_HEAD_DIM
M_CONV = 4
DA_HEADS = 4
DA_HEAD_DIM = 64
DA_V_DIM = 2 * DA_HEAD_DIM
DA_WIDTH = DA_HEADS * DA_V_DIM
REL_BUCKETS = 32
REL_MAX_DIST = 128
D_FF = 2816
FFN_CONV = 3
EPS = 1e-6
NEG_BIG = -1e30
LOG2E = 1.4426950408889634

LANES = 128
SUBLANES = 8
VMEM_LIMIT = 56 * 1024 * 1024

TOKEN_TILE = 512
M_CHUNK = 256
ATT_BLOCK = 512
ATT_SUB = ATT_BLOCK // 2
FF_CHUNK = 256
N_FF_CHUNKS = D_FF // FF_CHUNK
FF_SLABS = 2 * FF_CHUNK // LANES
FF_LOOKAHEAD = 3
FF_SLAB_SETS = FF_LOOKAHEAD + 1

F32 = jnp.float32
BF16 = jnp.bfloat16

C_IF = 4 * M_WIDTH
N_IF = 2 * M_HEADS
IN_COLS = C_IF + N_IF + 3 * DA_WIDTH + 2 * D_MODEL
C_TAIL = IN_COLS - N_IF


def _dot(a, b):
    return jnp.dot(a, b, preferred_element_type=F32)


def _dot_nt(a, b):
    return lax.dot_general(a, b, (((1,), (1,)), ((), ())), preferred_element_type=F32)


def _dot_tn(a, b):
    return lax.dot_general(a, b, (((0,), (0,)), ((), ())), preferred_element_type=F32)


def _rms(x, g):
    return x * lax.rsqrt(jnp.mean(x * x, -1, keepdims=True) + EPS) * g


def _sigmoid(x):
    return 1.0 / (1.0 + jnp.exp(-x))


def _slab_store(u, slab_ref, slab0):
    tm, n = u.shape
    for j in range(n // LANES):
        slab_ref[slab0 + j, SUBLANES:SUBLANES + tm, :] = u[:, j * LANES:(j + 1) * LANES]


def _slab_conv(tm, n, col0, slab_ref, slab0, carry_ref, cw_ref, cb_ref):
    taps = cw_ref.shape[0]
    outs = []
    for j in range(n // LANES):
        cols = slice(col0 + j * LANES, col0 + (j + 1) * LANES)
        s = slab0 + j
        slab_ref[s, 0:SUBLANES, :] = carry_ref[:, cols]
        carry_ref[:, cols] = slab_ref[s, tm:tm + SUBLANES, :]
        y = cb_ref[:, cols]
        for k in range(taps):
            y = y + cw_ref[taps - 1 - k:taps - k, cols] * slab_ref[s, pl.ds(SUBLANES - k, tm), :]
        outs.append(y)
    return jnp.concatenate(outs, axis=1)


def _const_spec(shape):
    nd = len(shape)
    return pl.BlockSpec(shape, lambda *_: (0,) * nd, pipeline_mode=pl.Buffered(1))


def _wprep_kernel(w_ref, last_ref, out_ref):
    lane = lax.broadcasted_iota(jnp.int32, (w_ref.shape[0], LANES), 1)
    out_ref[:, :C_IF] = w_ref[:, :C_IF].astype(BF16)
    tail = w_ref[:, C_IF:C_TAIL]
    n = C_TAIL - C_IF
    moved = pltpu.roll(tail, n - N_IF, axis=1)
    out_ref[:, C_IF:C_TAIL - LANES] = moved[:, :n - LANES].astype(BF16)
    end = jnp.where(lane < LANES - N_IF, moved[:, n - LANES:], pltpu.roll(last_ref[...], LANES - N_IF, axis=1))
    out_ref[:, C_TAIL - LANES:C_TAIL] = end.astype(BF16)
    out_ref[:, C_TAIL:] = jnp.where(lane < N_IF, tail[:, :LANES], 0.0).astype(BF16)


def _wprep(w):
    rb = 128
    return pl.pallas_call(
        _wprep_kernel,
        grid=(D_MODEL // rb,),
        in_specs=[pl.BlockSpec((rb, C_TAIL), lambda i: (i, 0)),
                  pl.BlockSpec((rb, LANES), lambda i: (i, C_TAIL // LANES))],
        out_specs=pl.BlockSpec((rb, C_TAIL + LANES), lambda i: (i, 0)),
        out_shape=jax.ShapeDtypeStruct((D_MODEL, C_TAIL + LANES), BF16),
        compiler_params=pltpu.CompilerParams(dimension_semantics=("arbitrary",), vmem_limit_bytes=VMEM_LIMIT),
        name="wprep",
    )(w, w)


def _inproj_kernel(x_ref, g_ref, w_ref, cw_ref, cb_ref, bif_ref,
                   mq_ref, mk_ref, mv_ref, mo_ref, dq_ref, dk_ref, dv_ref, gates_ref, ifg_ref,
                   carry_ref, slab_ref, *, tiles_per_seq):
    tm = x_ref.shape[0]
    i = pl.program_id(0)
    hb = _rms(x_ref[...], g_ref[...]).astype(BF16)

    @pl.when(i % tiles_per_seq == 0)
    def _():
        carry_ref[...] = jnp.zeros(carry_ref.shape, F32)

    def conv_silu(col0):
        _slab_store(_dot(hb, w_ref[:, col0:col0 + M_WIDTH]), slab_ref, col0 // LANES)
        y = _slab_conv(tm, M_WIDTH, col0, slab_ref, col0 // LANES, carry_ref, cw_ref, cb_ref)
        return y * _sigmoid(y)

    mq_ref[...] = conv_silu(0).astype(BF16)
    mk_ref[...] = (conv_silu(M_WIDTH) * (M_HEAD_DIM ** -0.5)).astype(BF16)

    c = 2 * M_WIDTH
    mv_ref[...] = _dot(hb, w_ref[:, c:c + M_WIDTH]).astype(BF16); c += M_WIDTH
    mo_ref[...] = _dot(hb, w_ref[:, c:c + M_WIDTH]).astype(BF16); c += M_WIDTH
    dq_ref[...] = (_dot(hb, w_ref[:, c:c + DA_WIDTH]) * (LOG2E * DA_HEAD_DIM ** -0.5)).astype(BF16); c += DA_WIDTH
    dk_ref[...] = _dot(hb, w_ref[:, c:c + DA_WIDTH]).astype(BF16); c += DA_WIDTH
    dv_ref[...] = _dot(hb, w_ref[:, c:c + DA_WIDTH]).astype(BF16); c += DA_WIDTH
    for j in range(4):
        gates_ref[:, j * 512:(j + 1) * 512] = _dot(hb, w_ref[:, c:c + 512]).astype(BF16); c += 512
    ifg_ref[...] = _dot(hb, w_ref[:, c:c + LANES]) + bif_ref[...]


def _inproj(x2, g, w_all, cw, cb, bif, seq):
    T = x2.shape[0]
    tm = TOKEN_TILE
    n_cols = w_all.shape[1]
    row = lambda i: (i, 0)
    bf = lambda n: jax.ShapeDtypeStruct((T, n), BF16)
    out_shapes = [bf(M_WIDTH)] * 4 + [bf(DA_WIDTH)] * 3 + [bf(2 * D_MODEL), jax.ShapeDtypeStruct((T, LANES), F32)]
    out_specs = [pl.BlockSpec((tm, M_WIDTH), row)] * 7 + [pl.BlockSpec((tm, 2 * D_MODEL), row),
                                                          pl.BlockSpec((tm, LANES), row)]
    return pl.pallas_call(
        functools.partial(_inproj_kernel, tiles_per_seq=seq // tm),
        grid=(T // tm,),
        in_specs=[pl.BlockSpec((tm, D_MODEL), row), _const_spec((1, D_MODEL)), _const_spec((D_MODEL, n_cols)),
                  _const_spec((M_CONV, 2 * M_WIDTH)), _const_spec((1, 2 * M_WIDTH)), _const_spec((1, LANES))],
        out_specs=out_specs,
        out_shape=out_shapes,
        scratch_shapes=[pltpu.VMEM((SUBLANES, 2 * M_WIDTH), F32),
                        pltpu.VMEM((2 * M_WIDTH // LANES, SUBLANES + tm, LANES), F32)],
        compiler_params=pltpu.CompilerParams(dimension_semantics=("arbitrary",), vmem_limit_bytes=VMEM_LIMIT),
        name="inproj",
    )(x2, g, w_all, cw, cb, bif)


def _cumsum_rows(tril, x):
    hi = x.astype(BF16)
    r1 = x - hi.astype(F32)
    mid = r1.astype(BF16)
    lo = (r1 - mid.astype(F32)).astype(BF16)
    return _dot(tril, hi) + _dot(tril, mid) + _dot(tril, lo)


def _mlstm_kernel(mq_ref, mk_ref, mv_ref, mo_ref, ifg_ref, g_ref, out_ref, ct_ref, m_ref):
    L = mq_ref.shape[0]
    d = M_HEAD_DIM

    @pl.when(pl.program_id(1) == 0)
    def _():
        ct_ref[...] = jnp.zeros(ct_ref.shape, F32)
        m_ref[...] = jnp.zeros(m_ref.shape, F32)

    ig = ifg_ref[...]
    fg = pltpu.roll(ig, LANES - M_HEADS, axis=1)
    logf = jnp.minimum(fg, 0.0) - jnp.log1p(jnp.exp(-jnp.abs(fg)))
    kidx = lax.broadcasted_iota(jnp.int32, (L, L), 0)
    qidx = lax.broadcasted_iota(jnp.int32, (L, L), 1)
    b = _cumsum_rows(jnp.where(kidx >= qidx, 1.0, 0.0).astype(BF16), logf)
    b_tot = b[L - 1:L, :]
    u = ig - b
    w_end = b_tot + u
    m_loc = jnp.max(w_end, axis=0, keepdims=True)
    a = jnp.exp(w_end - m_loc)
    m_prev = m_ref[...]
    m_new = jnp.maximum(b_tot + m_prev, m_loc)
    s_old = jnp.exp(b_tot + m_prev - m_new)
    s_loc = jnp.exp(m_loc - m_new)
    li_t = (b + m_prev).T
    b_t = b.T
    causal = kidx <= qidx

    for h in range(M_HEADS):
        hs = slice(h * d, (h + 1) * d)
        q = mq_ref[:, hs]
        k = mk_ref[:, hs]
        v = mv_ref[:, hs]
        ct_prev = ct_ref[h]
        s_t = _dot_nt(k, q)
        inter = _dot_nt(ct_prev.astype(BF16), q)
        log_d = jnp.where(causal, u[:, h:h + 1] + b_t[h:h + 1, :], -jnp.inf)
        li = li_t[h:h + 1, :]
        m_t = jnp.maximum(li, jnp.max(log_d, axis=0, keepdims=True))
        p_t = s_t * jnp.exp(log_d - m_t)
        s_inter = jnp.exp(li - m_t)
        num = s_inter * inter[:d] + _dot_tn(v, p_t.astype(BF16))
        den = s_inter * inter[d:d + 1] + jnp.sum(p_t, axis=0, keepdims=True)
        hh = num / jnp.maximum(jnp.abs(den), jnp.exp(-m_t))
        y = (hh * lax.rsqrt(jnp.mean(hh * hh, axis=0, keepdims=True) + EPS)).T
        out_ref[:, hs] = (_sigmoid(mo_ref[:, hs].astype(F32)) * (y * g_ref[:, hs])).astype(BF16)

        ak = a[:, h:h + 1] * k.astype(F32)
        so = s_old[:, h:h + 1]
        sl = s_loc[:, h:h + 1]
        ct_ref[h, 0:d, :] = so * ct_prev[:d] + sl * _dot_tn(v, ak.astype(BF16))
        ct_ref[h, d:d + 1, :] = so * ct_prev[d:d + 1] + sl * jnp.sum(ak, axis=0, keepdims=True)

    m_ref[...] = m_new


def _mlstm(mq, mk, mv, mo, ifg, g, batch, seq):
    T = mq.shape[0]
    L = M_CHUNK
    nc = seq // L
    row = lambda b, c: (b * nc + c, 0)
    blk = pl.BlockSpec((L, M_WIDTH), row)
    return pl.pallas_call(
        _mlstm_kernel,
        grid=(batch, nc),
        in_specs=[blk, blk, blk, blk, pl.BlockSpec((L, LANES), row), _const_spec((1, M_WIDTH))],
        out_specs=blk,
        out_shape=jax.ShapeDtypeStruct((T, M_WIDTH), BF16),
        scratch_shapes=[pltpu.VMEM((M_HEADS, M_HEAD_DIM + 2 * SUBLANES, M_HEAD_DIM), F32),
                        pltpu.VMEM((1, LANES), F32)],
        compiler_params=pltpu.CompilerParams(dimension_semantics=("arbitrary", "arbitrary"),
                                             vmem_limit_bytes=VMEM_LIMIT),
        name="mlstm",
    )(mq, mk, mv, mo, ifg, g)


def _t5_bucket(n):
    max_exact = REL_BUCKETS // 2
    nf = jnp.maximum(n, 1).astype(F32)
    large = max_exact + (jnp.log(nf / max_exact) / math.log(REL_MAX_DIST / max_exact)
                         * (REL_BUCKETS - max_exact)).astype(jnp.int32)
    large = jnp.minimum(large, REL_BUCKETS - 1)
    return jnp.where(n < max_exact, n, large)


def _attn_kernel(rb_ref, lam_ref, q_ref, k_ref, v_ref, g_ref, out_ref,
                 bias_ref, m_ref, l_ref, acc_ref, *, lam_init):
    tq = q_ref.shape[0]
    tk = tq
    sb = ATT_SUB
    h = pl.program_id(1)
    qi = pl.program_id(2)

    kk = lax.broadcasted_iota(jnp.int32, (sb, sb), 0)
    qq = lax.broadcasted_iota(jnp.int32, (sb, sb), 1)

    @pl.when((pl.program_id(0) == 0) & (h == 0) & (qi == 0))
    def _():
        for delta in range(2):
            bucket = _t5_bucket(jnp.maximum(qq + delta * sb - kk, 0))
            for hh in range(DA_HEADS):
                t = jnp.zeros((sb, sb), F32)
                for j in range(REL_BUCKETS):
                    t = jnp.where(bucket == j, rb_ref[j, hh], t)
                bias_ref[hh, delta] = (t - rb_ref[REL_BUCKETS - 1, hh]) * LOG2E

    lv = lam_ref[...]
    lam = (jnp.exp(jnp.sum(lv[0:1] * lv[1:2], axis=1, keepdims=True))
           - jnp.exp(jnp.sum(lv[2:3] * lv[3:4], axis=1, keepdims=True)) + lam_init)

    q = q_ref[...]
    lane = lax.broadcasted_iota(jnp.int32, q.shape, 1)
    zero = jnp.zeros_like(q)
    qcat = jnp.concatenate([jnp.where(lane < DA_HEAD_DIM, q, zero),
                            jnp.where(lane >= DA_HEAD_DIM, q, zero)], axis=0)

    m_ref[...] = jnp.full(m_ref.shape, -jnp.inf, F32)
    l_ref[...] = jnp.zeros(l_ref.shape, F32)
    acc_ref[...] = jnp.zeros(acc_ref.shape, F32)

    n_groups = 2 * tq // sb

    d0 = bias_ref[h, 0]
    d1 = bias_ref[h, 1]
    causal = qq >= kk
    ident = lambda s: s
    near = lambda s: jnp.concatenate([s[:sb], s[sb:] + d1], axis=0)
    early = lambda s: jnp.where(causal, s + d0, NEG_BIG)
    late = lambda s: jnp.concatenate([s[:sb] + d1, jnp.where(causal, s[sb:] + d0, NEG_BIG)], axis=0)

    def plan(j, n):
        if j == n - 1:
            return [(sb, early), (tk, late)] * (n_groups // 2)
        if j == n - 2:
            return [(tk, near), (tk, ident)] * (n_groups // 2)
        return [(tk, ident)] * n_groups

    def scores(j, n):
        k = k_ref[j * tk:(j + 1) * tk, :]
        return [_dot_nt(k[:rows], qcat[g * sb:(g + 1) * sb]) for g, (rows, _) in enumerate(plan(j, n))]

    def softmax_pv(j, n, sc):
        v = v_ref[j * tk:(j + 1) * tk, :]
        for g, (rows, fix) in enumerate(plan(j, n)):
            cols = slice(g * sb, (g + 1) * sb)
            s = fix(sc[g])
            m_prev = m_ref[:, cols]
            m_new = jnp.maximum(m_prev, jnp.max(s, axis=0, keepdims=True))
            alpha = jnp.exp2(m_prev - m_new)
            p = jnp.exp2(s - m_new)
            l_ref[:, cols] = alpha * l_ref[:, cols] + jnp.sum(p, axis=0, keepdims=True)
            acc_ref[:, cols] = alpha * acc_ref[:, cols] + _dot_tn(v[:rows], p.astype(BF16))
            m_ref[:, cols] = m_new

    def sweep(n):
        sc = scores(0, n)
        for j in range(n):
            nxt = scores(j + 1, n) if j + 1 < n else None
            softmax_pv(j, n, sc)
            sc = nxt

    for n in range(1, k_ref.shape[0] // tk + 1):
        pl.when(qi == n - 1)(functools.partial(sweep, n))

    o = acc_ref[...] * (1.0 / l_ref[...])
    out = (o[:, :tq] - lam * o[:, tq:]).T
    out_ref[...] = (_rms(out, g_ref[...]) * (1.0 - lam_init)).astype(BF16)


def _attention(rel_bias, lam_vec, dq, dk, dv, g, batch, seq, lam_init):
    T = dq.shape[0]
    tq = ATT_BLOCK
    nq = seq // tq
    return pl.pallas_call(
        functools.partial(_attn_kernel, lam_init=lam_init),
        grid=(batch, DA_HEADS, nq),
        in_specs=[pl.BlockSpec(memory_space=pltpu.SMEM),
                  pl.BlockSpec((4, DA_HEAD_DIM), lambda b, h, i: (0, 0)),
                  pl.BlockSpec((tq, DA_V_DIM), lambda b, h, i: (b * nq + i, h)),
                  pl.BlockSpec((seq, DA_V_DIM), lambda b, h, i: (b, h)),
                  pl.BlockSpec((seq, DA_V_DIM), lambda b, h, i: (b, h)),
                  pl.BlockSpec((1, DA_V_DIM), lambda b, h, i: (0, h))],
        out_specs=pl.BlockSpec((tq, DA_V_DIM), lambda b, h, i: (b * nq + i, h)),
        out_shape=jax.ShapeDtypeStruct((T, DA_WIDTH), BF16),
        scratch_shapes=[pltpu.VMEM((DA_HEADS, 2, ATT_SUB, ATT_SUB), F32),
                        pltpu.VMEM((1, 2 * tq), F32),
                        pltpu.VMEM((1, 2 * tq), F32),
                        pltpu.VMEM((DA_V_DIM, 2 * tq), F32)],
        compiler_params=pltpu.CompilerParams(dimension_semantics=("arbitrary",) * 3,
                                             vmem_limit_bytes=VMEM_LIMIT),
        name="diffattn",
    )(rel_bias, lam_vec, dq, dk, dv, g)


def _gelu_tanh(x):
    return 0.5 * x * (1.0 + jnp.tanh(math.sqrt(2.0 / math.pi) * (x + 0.044715 * (x * x * x))))


def _ffn_kernel(gm_ref, hd_ref, gates_ref, x_ref, p_ref, wm_ref, wd_ref, wo_ref,
                gf_ref, wup_ref, cw_ref, cb_ref, wdn_ref, gp_ref, wpg_ref, wp_ref, gl_ref,
                out_ref, carry_ref, slab_ref, acc_ref, *, tiles_per_seq, final_norm):
    tm = x_ref.shape[0]
    i = pl.program_id(0)

    ya = _dot(gm_ref[...], wm_ref[...])
    yb = _dot(hd_ref[...], wd_ref[...])
    ga = _sigmoid(gates_ref[:, :D_MODEL].astype(F32))
    gb = _sigmoid(gates_ref[:, D_MODEL:].astype(F32))
    x1 = x_ref[...] + _dot((ga * ya + gb * yb).astype(BF16), wo_ref[...])

    hb = _rms(x1, gf_ref[...]).astype(BF16)

    @pl.when(i % tiles_per_seq == 0)
    def _():
        carry_ref[...] = jnp.zeros(carry_ref.shape, F32)

    def slab0(c, half):
        return (c % FF_SLAB_SETS) * FF_SLABS + half * (FF_CHUNK // LANES)

    def up(c):
        for half, off in enumerate((0, D_FF)):
            lo = off + c * FF_CHUNK
            _slab_store(_dot(hb, wup_ref[:, lo:lo + FF_CHUNK]), slab_ref, slab0(c, half))

    def conv(c, half):
        return _slab_conv(tm, FF_CHUNK, half * D_FF + c * FF_CHUNK, slab_ref, slab0(c, half),
                          carry_ref, cw_ref, cb_ref)

    for c in range(FF_LOOKAHEAD):
        up(c)
    for c in range(N_FF_CHUNKS):
        if c + FF_LOOKAHEAD < N_FF_CHUNKS:
            up(c + FF_LOOKAHEAD)
        act = (_gelu_tanh(conv(c, 1)) * conv(c, 0)).astype(BF16)
        d = _dot(act, wdn_ref[c * FF_CHUNK:(c + 1) * FF_CHUNK, :])
        if c == 0:
            acc_ref[...] = d
        else:
            acc_ref[...] += d

    x2 = x1 + acc_ref[...]
    hg = _rms(x2, gp_ref[...]).astype(BF16)
    gate = _sigmoid(_dot(hg, wpg_ref[...]))
    pe = _dot(p_ref[...].astype(BF16), wp_ref[...])
    x3 = x2 + gate * pe
    out_ref[...] = _rms(x3, gl_ref[...]) if final_norm else x3


def _ffn(gm, hd, gates, x2, p2, wm, wd, wo, gf, wup, cw, cb, wdn, gp, wpg, wp, gl, seq, final_norm):
    T = x2.shape[0]
    tm = TOKEN_TILE
    row = lambda i: (i, 0)
    return pl.pallas_call(
        functools.partial(_ffn_kernel, tiles_per_seq=seq // tm, final_norm=final_norm),
        grid=(T // tm,),
        in_specs=[pl.BlockSpec((tm, M_WIDTH), row), pl.BlockSpec((tm, DA_WIDTH), row),
                  pl.BlockSpec((tm, 2 * D_MODEL), row), pl.BlockSpec((tm, D_MODEL), row),
                  pl.BlockSpec((tm, PLE_DIM), row),
                  _const_spec((M_WIDTH, D_MODEL)), _const_spec((DA_WIDTH, D_MODEL)),
                  _const_spec((D_MODEL, D_MODEL)), _const_spec((1, D_MODEL)),
                  _const_spec((D_MODEL, 2 * D_FF)), _const_spec((FFN_CONV, 2 * D_FF)),
                  _const_spec((1, 2 * D_FF)), _const_spec((D_FF, D_MODEL)),
                  _const_spec((1, D_MODEL)), _const_spec((D_MODEL, D_MODEL)),
                  _const_spec((PLE_DIM, D_MODEL)), _const_spec((1, D_MODEL))],
        out_specs=pl.BlockSpec((tm, D_MODEL), row),
        out_shape=jax.ShapeDtypeStruct((T, D_MODEL), F32),
        scratch_shapes=[pltpu.VMEM((SUBLANES, 2 * D_FF), F32),
                        pltpu.VMEM((FF_SLAB_SETS * FF_SLABS, SUBLANES + tm, LANES), F32),
                        pltpu.VMEM((tm, D_MODEL), F32)],
        compiler_params=pltpu.CompilerParams(dimension_semantics=("arbitrary",), vmem_limit_bytes=VMEM_LIMIT),
        name="convffn",
    )(gm, hd, gates, x2, p2, wm, wd, wo, gf, wup, cw, cb, wdn, gp, wpg, wp, gl)


def kernel(x, p, rel_bias, norm_mix_g, w_in, b_if, m_conv_w, m_conv_b, m_norm_g, da_lambda, da_norm_g,
           w_br_m, w_br_d, w_out, norm_ffn_g, w_up, ffn_conv_w, ffn_conv_b, w_down, norm_ple_g, w_ple_gate,
           w_ple, norm_final_g):
    batch, seq, _ = x.shape
    depth = w_in.shape[0]
    T = batch * seq
    assert seq % TOKEN_TILE == 0 and seq % M_CHUNK == 0 and seq % ATT_BLOCK == 0
    xt = x.reshape(T, D_MODEL)
    row = lambda v: v.reshape(1, -1).astype(F32)

    for l in range(depth):
        w_all = _wprep(w_in[l].astype(F32))
        bif = jnp.concatenate([b_if[l].astype(F32), jnp.zeros((LANES - N_IF,), F32)]).reshape(1, LANES)

        mq, mk, mv, mo, dq, dk, dv, gates, ifg = _inproj(
            xt, row(norm_mix_g[l]), w_all, m_conv_w[l].astype(F32), row(m_conv_b[l]), bif, seq)

        gm = _mlstm(mq, mk, mv, mo, ifg, row(m_norm_g[l]), batch, seq)

        lam_init = 0.8 - 0.6 * math.exp(-0.3 * l)
        hd = _attention(rel_bias.astype(F32), da_lambda[l].astype(F32), dq, dk, dv, row(da_norm_g[l]),
                        batch, seq, lam_init)

        xt = _ffn(gm, hd, gates, xt, p[l].reshape(T, PLE_DIM),
                  w_br_m[l].astype(BF16), w_br_d[l].astype(BF16), w_out[l].astype(BF16),
                  row(norm_ffn_g[l]), w_up[l].astype(BF16),
                  ffn_conv_w[l].astype(F32), row(ffn_conv_b[l]), w_down[l].astype(BF16), row(norm_ple_g[l]),
                  w_ple_gate[l].astype(BF16), w_ple[l].astype(BF16), row(norm_final_g), seq,
                  final_norm=(l == depth - 1))

    return xt.reshape(batch, seq, D_MODEL)
```

```python
import functools
import math

import jax
import jax.numpy as jnp
from jax import lax
from jax.experimental import pallas as pl
from jax.experimental.pallas import tpu as pltpu

D_MODEL = 1024
PLE_DIM = 256
M_HEADS = 4
M_HEAD_DIM = 128
M_WIDTH = M_HEADS * M_HEAD_DIM
M_CONV = 4
DA_HEADS = 4
DA_HEAD_DIM = 64
DA_V_DIM = 2 * DA_HEAD_DIM
DA_WIDTH = DA_HEADS * DA_V_DIM
REL_BUCKETS = 32
REL_MAX_DIST = 128
D_FF = 2816
FFN_CONV = 3
EPS = 1e-6
NEG_BIG = -1e30
LOG2E = 1.4426950408889634

LANES = 128
SUBLANES = 8
VMEM_LIMIT = 56 * 1024 * 1024

TOKEN_TILE = 512
M_CHUNK = 256
ATT_BLOCK = 512
ATT_SUB = ATT_BLOCK // 2
FF_CHUNK = 256
N_FF_CHUNKS = D_FF // FF_CHUNK
FF_SLABS = 2 * FF_CHUNK // LANES
FF_LOOKAHEAD = 3
FF_SLAB_SETS = FF_LOOKAHEAD + 1

F32 = jnp.float32
BF16 = jnp.bfloat16

C_IF = 4 * M_WIDTH
N_IF = 2 * M_HEADS
IN_COLS = C_IF + N_IF + 3 * DA_WIDTH + 2 * D_MODEL
C_TAIL = IN_COLS - N_IF


def _dot(a, b):
    return jnp.dot(a, b, preferred_element_type=F32)


def _dot_nt(a, b):
    return lax.dot_general(a, b, (((1,), (1,)), ((), ())), preferred_element_type=F32)


def _dot_tn(a, b):
    return lax.dot_general(a, b, (((0,), (0,)), ((), ())), preferred_element_type=F32)


def _rms(x, g):
    return x * lax.rsqrt(jnp.mean(x * x, -1, keepdims=True) + EPS) * g


def _sigmoid(x):
    return 1.0 / (1.0 + jnp.exp(-x))


def _slab_store(u, slab_ref, slab0):
    tm, n = u.shape
    for j in range(n // LANES):
        slab_ref[slab0 + j, SUBLANES:SUBLANES + tm, :] = u[:, j * LANES:(j + 1) * LANES]


def _slab_conv(tm, n, col0, slab_ref, slab0, carry_ref, cw_ref, cb_ref):
    taps = cw_ref.shape[0]
    outs = []
    for j in range(n // LANES):
        cols = slice(col0 + j * LANES, col0 + (j + 1) * LANES)
        s = slab0 + j
        slab_ref[s, 0:SUBLANES, :] = carry_ref[:, cols]
        carry_ref[:, cols] = slab_ref[s, tm:tm + SUBLANES, :]
        y = cb_ref[:, cols]
        for k in range(taps):
            y = y + cw_ref[taps - 1 - k:taps - k, cols] * slab_ref[s, pl.ds(SUBLANES - k, tm), :]
        outs.append(y)
    return jnp.concatenate(outs, axis=1)


def _const_spec(shape):
    nd = len(shape)
    return pl.BlockSpec(shape, lambda *_: (0,) * nd, pipeline_mode=pl.Buffered(1))


def _wprep_kernel(w_ref, last_ref, out_ref):
    lane = lax.broadcasted_iota(jnp.int32, (w_ref.shape[0], LANES), 1)
    out_ref[:, :C_IF] = w_ref[:, :C_IF].astype(BF16)
    tail = w_ref[:, C_IF:C_TAIL]
    n = C_TAIL - C_IF
    moved = pltpu.roll(tail, n - N_IF, axis=1)
    out_ref[:, C_IF:C_TAIL - LANES] = moved[:, :n - LANES].astype(BF16)
    end = jnp.where(lane < LANES - N_IF, moved[:, n - LANES:], pltpu.roll(last_ref[...], LANES - N_IF, axis=1))
    out_ref[:, C_TAIL - LANES:C_TAIL] = end.astype(BF16)
    out_ref[:, C_TAIL:] = jnp.where(lane < N_IF, tail[:, :LANES], 0.0).astype(BF16)


def _wprep(w):
    rb = 128
    return pl.pallas_call(
        _wprep_kernel,
        grid=(D_MODEL // rb,),
        in_specs=[pl.BlockSpec((rb, C_TAIL), lambda i: (i, 0)),
                  pl.BlockSpec((rb, LANES), lambda i: (i, C_TAIL // LANES))],
        out_specs=pl.BlockSpec((rb, C_TAIL + LANES), lambda i: (i, 0)),
        out_shape=jax.ShapeDtypeStruct((D_MODEL, C_TAIL + LANES), BF16),
        compiler_params=pltpu.CompilerParams(dimension_semantics=("arbitrary",), vmem_limit_bytes=VMEM_LIMIT),
        name="wprep",
    )(w, w)


def _inproj_kernel(x_ref, g_ref, w_ref, cw_ref, cb_ref, bif_ref,
                   mq_ref, mk_ref, mv_ref, mo_ref, dq_ref, dk_ref, dv_ref, gates_ref, ifg_ref,
                   carry_ref, slab_ref, *, tiles_per_seq):
    tm = x_ref.shape[0]
    i = pl.program_id(0)
    hb = _rms(x_ref[...], g_ref[...]).astype(BF16)

    @pl.when(i % tiles_per_seq == 0)
    def _():
        carry_ref[...] = jnp.zeros(carry_ref.shape, F32)

    def conv_silu(col0):
        _slab_store(_dot(hb, w_ref[:, col0:col0 + M_WIDTH]), slab_ref, col0 // LANES)
        y = _slab_conv(tm, M_WIDTH, col0, slab_ref, col0 // LANES, carry_ref, cw_ref, cb_ref)
        return y * _sigmoid(y)

    mq_ref[...] = conv_silu(0).astype(BF16)
    mk_ref[...] = (conv_silu(M_WIDTH) * (M_HEAD_DIM ** -0.5)).astype(BF16)

    c = 2 * M_WIDTH
    mv_ref[...] = _dot(hb, w_ref[:, c:c + M_WIDTH]).astype(BF16).T; c += M_WIDTH
    mo_ref[...] = _dot(hb, w_ref[:, c:c + M_WIDTH]).astype(BF16); c += M_WIDTH
    dq_ref[...] = (_dot(hb, w_ref[:, c:c + DA_WIDTH]) * (LOG2E * DA_HEAD_DIM ** -0.5)).astype(BF16); c += DA_WIDTH
    dk_ref[...] = _dot(hb, w_ref[:, c:c + DA_WIDTH]).astype(BF16); c += DA_WIDTH
    dv_ref[...] = _dot(hb, w_ref[:, c:c + DA_WIDTH]).astype(BF16).T; c += DA_WIDTH
    for j in range(4):
        gates_ref[:, j * 512:(j + 1) * 512] = _dot(hb, w_ref[:, c:c + 512]).astype(BF16); c += 512
    ifg_ref[...] = _dot(hb, w_ref[:, c:c + LANES]) + bif_ref[...]


def _inproj(x2, g, w_all, cw, cb, bif, seq):
    T = x2.shape[0]
    tm = TOKEN_TILE
    n_cols = w_all.shape[1]
    tps = seq // tm
    row = lambda i: (i, 0)
    bf = lambda n: jax.ShapeDtypeStruct((T, n), BF16)
    bf_t = lambda n: jax.ShapeDtypeStruct((T // seq, n, seq), BF16)
    blk = pl.BlockSpec((tm, M_WIDTH), row)
    blk_t = pl.BlockSpec((None, M_WIDTH, tm), lambda i: (i // tps, 0, i % tps))
    out_shapes = [bf(M_WIDTH), bf(M_WIDTH), bf_t(M_WIDTH), bf(M_WIDTH), bf(DA_WIDTH), bf(DA_WIDTH), bf_t(DA_WIDTH),
                  bf(2 * D_MODEL), jax.ShapeDtypeStruct((T, LANES), F32)]
    out_specs = [blk, blk, blk_t, blk, blk, blk, blk_t,
                 pl.BlockSpec((tm, 2 * D_MODEL), row), pl.BlockSpec((tm, LANES), row)]
    return pl.pallas_call(
        functools.partial(_inproj_kernel, tiles_per_seq=tps),
        grid=(T // tm,),
        in_specs=[pl.BlockSpec((tm, D_MODEL), row), _const_spec((1, D_MODEL)), _const_spec((D_MODEL, n_cols)),
                  _const_spec((M_CONV, 2 * M_WIDTH)), _const_spec((1, 2 * M_WIDTH)), _const_spec((1, LANES))],
        out_specs=out_specs,
        out_shape=out_shapes,
        scratch_shapes=[pltpu.VMEM((SUBLANES, 2 * M_WIDTH), F32),
                        pltpu.VMEM((2 * M_WIDTH // LANES, SUBLANES + tm, LANES), F32)],
        compiler_params=pltpu.CompilerParams(dimension_semantics=("arbitrary",), vmem_limit_bytes=VMEM_LIMIT),
        name="inproj",
    )(x2, g, w_all, cw, cb, bif)


def _cumsum_rows(tril, x):
    hi = x.astype(BF16)
    r1 = x - hi.astype(F32)
    mid = r1.astype(BF16)
    lo = (r1 - mid.astype(F32)).astype(BF16)
    return _dot(tril, hi) + _dot(tril, mid) + _dot(tril, lo)


def _mlstm_kernel(mq_ref, mk_ref, mv_ref, mo_ref, ifg_ref, g_ref, out_ref, ct_ref, m_ref):
    L = mq_ref.shape[0]
    d = M_HEAD_DIM

    @pl.when(pl.program_id(1) == 0)
    def _():
        ct_ref[...] = jnp.zeros(ct_ref.shape, F32)
        m_ref[...] = jnp.zeros(m_ref.shape, F32)

    ig = ifg_ref[...]
    fg = pltpu.roll(ig, LANES - M_HEADS, axis=1)
    logf = jnp.minimum(fg, 0.0) - jnp.log1p(jnp.exp(-jnp.abs(fg)))
    kidx = lax.broadcasted_iota(jnp.int32, (L, L), 0)
    qidx = lax.broadcasted_iota(jnp.int32, (L, L), 1)
    b = _cumsum_rows(jnp.where(kidx >= qidx, 1.0, 0.0).astype(BF16), logf)
    b_tot = b[L - 1:L, :]
    u = ig - b
    w_end = b_tot + u
    m_loc = jnp.max(w_end, axis=0, keepdims=True)
    a = jnp.exp(w_end - m_loc)
    m_prev = m_ref[...]
    m_new = jnp.maximum(b_tot + m_prev, m_loc)
    s_old = jnp.exp(b_tot + m_prev - m_new)
    s_loc = jnp.exp(m_loc - m_new)
    li_t = (b + m_prev).T
    b_t = b.T
    causal = kidx <= qidx

    for h in range(M_HEADS):
        hs = slice(h * d, (h + 1) * d)
        q = mq_ref[:, hs]
        k = mk_ref[:, hs]
        v_t = mv_ref[hs, :]
        ct_prev = ct_ref[h]
        s_t = _dot_nt(k, q)
        inter = _dot_nt(ct_prev.astype(BF16), q)
        log_d = jnp.where(causal, u[:, h:h + 1] + b_t[h:h + 1, :], -jnp.inf)
        li = li_t[h:h + 1, :]
        m_t = jnp.maximum(li, jnp.max(log_d, axis=0, keepdims=True))
        p_t = s_t * jnp.exp(log_d - m_t)
        s_inter = jnp.exp(li - m_t)
        num = s_inter * inter[:d] + _dot(v_t, p_t.astype(BF16))
        den = s_inter * inter[d:d + 1] + jnp.sum(p_t, axis=0, keepdims=True)
        hh = num / jnp.maximum(jnp.abs(den), jnp.exp(-m_t))
        y = (hh * lax.rsqrt(jnp.mean(hh * hh, axis=0, keepdims=True) + EPS)).T
        out_ref[:, hs] = (_sigmoid(mo_ref[:, hs].astype(F32)) * (y * g_ref[:, hs])).astype(BF16)

        ak = a[:, h:h + 1] * k.astype(F32)
        so = s_old[:, h:h + 1]
        sl = s_loc[:, h:h + 1]
        ct_ref[h, 0:d, :] = so * ct_prev[:d] + sl * _dot(v_t, ak.astype(BF16))
        ct_ref[h, d:d + 1, :] = so * ct_prev[d:d + 1] + sl * jnp.sum(ak, axis=0, keepdims=True)

    m_ref[...] = m_new


def _mlstm(mq, mk, mv, mo, ifg, g, batch, seq):
    T = mq.shape[0]
    L = M_CHUNK
    nc = seq // L
    row = lambda b, c: (b * nc + c, 0)
    blk = pl.BlockSpec((L, M_WIDTH), row)
    blk_t = pl.BlockSpec((None, M_WIDTH, L), lambda b, c: (b, 0, c))
    return pl.pallas_call(
        _mlstm_kernel,
        grid=(batch, nc),
        in_specs=[blk, blk, blk_t, blk, pl.BlockSpec((L, LANES), row), _const_spec((1, M_WIDTH))],
        out_specs=blk,
        out_shape=jax.ShapeDtypeStruct((T, M_WIDTH), BF16),
        scratch_shapes=[pltpu.VMEM((M_HEADS, M_HEAD_DIM + 2 * SUBLANES, M_HEAD_DIM), F32),
                        pltpu.VMEM((1, LANES), F32)],
        compiler_params=pltpu.CompilerParams(dimension_semantics=("arbitrary", "arbitrary"),
                                             vmem_limit_bytes=VMEM_LIMIT),
        name="mlstm",
    )(mq, mk, mv, mo, ifg, g)


def _t5_bucket(n):
    max_exact = REL_BUCKETS // 2
    nf = jnp.maximum(n, 1).astype(F32)
    large = max_exact + (jnp.log(nf / max_exact) / math.log(REL_MAX_DIST / max_exact)
                         * (REL_BUCKETS - max_exact)).astype(jnp.int32)
    large = jnp.minimum(large, REL_BUCKETS - 1)
    return jnp.where(n < max_exact, n, large)


def _attn_kernel(rb_ref, lam_ref, q_ref, k_ref, v_ref, g_ref, out_ref,
                 bias_ref, m_ref, l_ref, acc_ref, s_ref, *, lam_init):
    tq = q_ref.shape[0]
    tk = tq
    sb = ATT_SUB
    h = pl.program_id(1)
    qi = pl.program_id(2)

    kk = lax.broadcasted_iota(jnp.int32, (sb, sb), 0)
    qq = lax.broadcasted_iota(jnp.int32, (sb, sb), 1)

    @pl.when((pl.program_id(0) == 0) & (h == 0) & (qi == 0))
    def _():
        for delta in range(2):
            bucket = _t5_bucket(jnp.maximum(qq + delta * sb - kk, 0))
            for hh in range(DA_HEADS):
                t = jnp.zeros((sb, sb), F32)
                for j in range(REL_BUCKETS):
                    t = jnp.where(bucket == j, rb_ref[j, hh], t)
                bias_ref[hh, delta] = (t - rb_ref[REL_BUCKETS - 1, hh]) * LOG2E

    lv = lam_ref[...]
    lam = (jnp.exp(jnp.sum(lv[0:1] * lv[1:2], axis=1, keepdims=True))
           - jnp.exp(jnp.sum(lv[2:3] * lv[3:4], axis=1, keepdims=True)) + lam_init)

    q = q_ref[...]
    lane = lax.broadcasted_iota(jnp.int32, q.shape, 1)
    zero = jnp.zeros_like(q)
    qcat = jnp.concatenate([jnp.where(lane < DA_HEAD_DIM, q, zero),
                            jnp.where(lane >= DA_HEAD_DIM, q, zero)], axis=0)

    m_ref[...] = jnp.full(m_ref.shape, -jnp.inf, F32)
    l_ref[...] = jnp.zeros(l_ref.shape, F32)
    acc_ref[...] = jnp.zeros(acc_ref.shape, F32)

    n_groups = 2 * tq // sb

    d0 = bias_ref[h, 0]
    d1 = bias_ref[h, 1]
    causal = qq >= kk
    ident = lambda s: s
    near = lambda s: jnp.concatenate([s[:sb], s[sb:] + d1], axis=0)
    early = lambda s: jnp.where(causal, s + d0, NEG_BIG)
    late = lambda s: jnp.concatenate([s[:sb] + d1, jnp.where(causal, s[sb:] + d0, NEG_BIG)], axis=0)

    def plan(j, n):
        if j == n - 1:
            return [(sb, early), (tk, late)] * (n_groups // 2)
        if j == n - 2:
            return [(tk, near), (tk, ident)] * (n_groups // 2)
        return [(tk, ident)] * n_groups

    def scores(j, n):
        k = k_ref[j * tk:(j + 1) * tk, :]
        for g, (rows, _) in enumerate(plan(j, n)):
            s_ref[j % 2, g, 0:rows, :] = _dot_nt(k[:rows], qcat[g * sb:(g + 1) * sb])

    def softmax_pv(j, n):
        v_t = v_ref[:, j * tk:(j + 1) * tk]
        for g, (rows, fix) in enumerate(plan(j, n)):
            cols = slice(g * sb, (g + 1) * sb)
            s = fix(s_ref[j % 2, g, 0:rows, :])
            m_prev = m_ref[:, cols]
            m_new = jnp.maximum(m_prev, jnp.max(s, axis=0, keepdims=True))
            alpha = jnp.exp2(m_prev - m_new)
            p = jnp.exp2(s - m_new)
            l_ref[:, cols] = alpha * l_ref[:, cols] + jnp.sum(p, axis=0, keepdims=True)
            acc_ref[:, cols] = alpha * acc_ref[:, cols] + _dot(v_t[:, :rows], p.astype(BF16))
            m_ref[:, cols] = m_new

    def sweep(n):
        scores(0, n)
        for j in range(n):
            if j + 1 < n:
                scores(j + 1, n)
            softmax_pv(j, n)

    for n in range(1, k_ref.shape[0] // tk + 1):
        pl.when(qi == n - 1)(functools.partial(sweep, n))

    o = acc_ref[...] * (1.0 / l_ref[...])
    out = (o[:, :tq] - lam * o[:, tq:]).T
    out_ref[...] = (_rms(out, g_ref[...]) * (1.0 - lam_init)).astype(BF16)


def _attention(rel_bias, lam_vec, dq, dk, dv, g, batch, seq, lam_init):
    T = dq.shape[0]
    tq = ATT_BLOCK
    nq = seq // tq
    return pl.pallas_call(
        functools.partial(_attn_kernel, lam_init=lam_init),
        grid=(batch, DA_HEADS, nq),
        in_specs=[pl.BlockSpec(memory_space=pltpu.SMEM),
                  pl.BlockSpec((4, DA_HEAD_DIM), lambda b, h, i: (0, 0)),
                  pl.BlockSpec((tq, DA_V_DIM), lambda b, h, i: (b * nq + i, h)),
                  pl.BlockSpec((seq, DA_V_DIM), lambda b, h, i: (b, h)),
                  pl.BlockSpec((None, DA_V_DIM, seq), lambda b, h, i: (b, h, 0)),
                  pl.BlockSpec((1, DA_V_DIM), lambda b, h, i: (0, h))],
        out_specs=pl.BlockSpec((tq, DA_V_DIM), lambda b, h, i: (b * nq + i, h)),
        out_shape=jax.ShapeDtypeStruct((T, DA_WIDTH), BF16),
        scratch_shapes=[pltpu.VMEM((DA_HEADS, 2, ATT_SUB, ATT_SUB), F32),
                        pltpu.VMEM((1, 2 * tq), F32),
                        pltpu.VMEM((1, 2 * tq), F32),
                        pltpu.VMEM((DA_V_DIM, 2 * tq), F32),
                        pltpu.VMEM((2, 2 * tq // ATT_SUB, tq, ATT_SUB), F32)],
        compiler_params=pltpu.CompilerParams(dimension_semantics=("arbitrary",) * 3,
                                             vmem_limit_bytes=VMEM_LIMIT),
        name="diffattn",
    )(rel_bias, lam_vec, dq, dk, dv, g)


def _gelu_tanh(x):
    return 0.5 * x * (1.0 + jnp.tanh(math.sqrt(2.0 / math.pi) * (x + 0.044715 * (x * x * x))))


def _ffn_kernel(gm_ref, hd_ref, gates_ref, x_ref, p_ref, wm_ref, wd_ref, wo_ref,
                gf_ref, wup_ref, cw_ref, cb_ref, wdn_ref, gp_ref, wpg_ref, wp_ref, gl_ref,
                out_ref, carry_ref, slab_ref, acc_ref, *, tiles_per_seq, final_norm):
    tm = x_ref.shape[0]
    i = pl.program_id(0)

    ya = _dot(gm_ref[...], wm_ref[...])
    yb = _dot(hd_ref[...], wd_ref[...])
    ga = _sigmoid(gates_ref[:, :D_MODEL].astype(F32))
    gb = _sigmoid(gates_ref[:, D_MODEL:].astype(F32))
    x1 = x_ref[...] + _dot((ga * ya + gb * yb).astype(BF16), wo_ref[...])

    hb = _rms(x1, gf_ref[...]).astype(BF16)

    @pl.when(i % tiles_per_seq == 0)
    def _():
        carry_ref[...] = jnp.zeros(carry_ref.shape, F32)

    def slab0(c, half):
        return (c % FF_SLAB_SETS) * FF_SLABS + half * (FF_CHUNK // LANES)

    def up(c):
        for half, off in enumerate((0, D_FF)):
            lo = off + c * FF_CHUNK
            _slab_store(_dot(hb, wup_ref[:, lo:lo + FF_CHUNK]), slab_ref, slab0(c, half))

    def conv(c, half):
        return _slab_conv(tm, FF_CHUNK, half * D_FF + c * FF_CHUNK, slab_ref, slab0(c, half),
                          carry_ref, cw_ref, cb_ref)

    for c in range(FF_LOOKAHEAD):
        up(c)
    for c in range(N_FF_CHUNKS):
        if c + FF_LOOKAHEAD < N_FF_CHUNKS:
            up(c + FF_LOOKAHEAD)
        act = (_gelu_tanh(conv(c, 1)) * conv(c, 0)).astype(BF16)
        d = _dot(act, wdn_ref[c * FF_CHUNK:(c + 1) * FF_CHUNK, :])
        if c == 0:
            acc_ref[...] = d
        else:
            acc_ref[...] += d

    x2 = x1 + acc_ref[...]
    hg = _rms(x2, gp_ref[...]).astype(BF16)
    gate = _sigmoid(_dot(hg, wpg_ref[...]))
    pe = _dot(p_ref[...].astype(BF16), wp_ref[...])
    x3 = x2 + gate * pe
    out_ref[...] = _rms(x3, gl_ref[...]) if final_norm else x3


def _ffn(gm, hd, gates, x2, p2, wm, wd, wo, gf, wup, cw, cb, wdn, gp, wpg, wp, gl, seq, final_norm):
    T = x2.shape[0]
    tm = TOKEN_TILE
    row = lambda i: (i, 0)
    return pl.pallas_call(
        functools.partial(_ffn_kernel, tiles_per_seq=seq // tm, final_norm=final_norm),
        grid=(T // tm,),
        in_specs=[pl.BlockSpec((tm, M_WIDTH), row), pl.BlockSpec((tm, DA_WIDTH), row),
                  pl.BlockSpec((tm, 2 * D_MODEL), row), pl.BlockSpec((tm, D_MODEL), row),
                  pl.BlockSpec((tm, PLE_DIM), row),
                  _const_spec((M_WIDTH, D_MODEL)), _const_spec((DA_WIDTH, D_MODEL)),
                  _const_spec((D_MODEL, D_MODEL)), _const_spec((1, D_MODEL)),
                  _const_spec((D_MODEL, 2 * D_FF)), _const_spec((FFN_CONV, 2 * D_FF)),
                  _const_spec((1, 2 * D_FF)), _const_spec((D_FF, D_MODEL)),
                  _const_spec((1, D_MODEL)), _const_spec((D_MODEL, D_MODEL)),
                  _const_spec((PLE_DIM, D_MODEL)), _const_spec((1, D_MODEL))],
        out_specs=pl.BlockSpec((tm, D_MODEL), row),
        out_shape=jax.ShapeDtypeStruct((T, D_MODEL), F32),
        scratch_shapes=[pltpu.VMEM((SUBLANES, 2 * D_FF), F32),
                        pltpu.VMEM((FF_SLAB_SETS * FF_SLABS, SUBLANES + tm, LANES), F32),
                        pltpu.VMEM((tm, D_MODEL), F32)],
        compiler_params=pltpu.CompilerParams(dimension_semantics=("arbitrary",), vmem_limit_bytes=VMEM_LIMIT),
        name="convffn",
    )(gm, hd, gates, x2, p2, wm, wd, wo, gf, wup, cw, cb, wdn, gp, wpg, wp, gl)


def kernel(x, p, rel_bias, norm_mix_g, w_in, b_if, m_conv_w, m_conv_b, m_norm_g, da_lambda, da_norm_g,
           w_br_m, w_br_d, w_out, norm_ffn_g, w_up, ffn_conv_w, ffn_conv_b, w_down, norm_ple_g, w_ple_gate,
           w_ple, norm_final_g):
    batch, seq, _ = x.shape
    depth = w_in.shape[0]
    T = batch * seq
    assert seq % TOKEN_TILE == 0 and seq % M_CHUNK == 0 and seq % ATT_BLOCK == 0
    xt = x.reshape(T, D_MODEL)
    row = lambda v: v.reshape(1, -1).astype(F32)

    for l in range(depth):
        w_all = _wprep(w_in[l].astype(F32))
        bif = jnp.concatenate([b_if[l].astype(F32), jnp.zeros((LANES - N_IF,), F32)]).reshape(1, LANES)

        mq, mk, mv, mo, dq, dk, dv, gates, ifg = _inproj(
            xt, row(norm_mix_g[l]), w_all, m_conv_w[l].astype(F32), row(m_conv_b[l]), bif, seq)

        gm = _mlstm(mq, mk, mv, mo, ifg, row(m_norm_g[l]), batch, seq)

        lam_init = 0.8 - 0.6 * math.exp(-0.3 * l)
        hd = _attention(rel_bias.astype(F32), da_lambda[l].astype(F32), dq, dk, dv, row(da_norm_g[l]),
                        batch, seq, lam_init)

        xt = _ffn(gm, hd, gates, xt, p[l].reshape(T, PLE_DIM),
                  w_br_m[l].astype(BF16), w_br_d[l].astype(BF16), w_out[l].astype(BF16),
                  row(norm_ffn_g[l]), w_up[l].astype(BF16),
                  ffn_conv_w[l].astype(F32), row(ffn_conv_b[l]), w_down[l].astype(BF16), row(norm_ple_g[l]),
                  w_ple_gate[l].astype(BF16), w_ple[l].astype(BF16), row(norm_final_g), seq,
                  final_norm=(l == depth - 1))

    return xt.reshape(batch, seq, D_MODEL)
```

```python
import functools
import math

import jax
import jax.numpy as jnp
from jax import lax
from jax.experimental import pallas as pl
from jax.experimental.pallas import tpu as pltpu

D_MODEL = 1024
PLE_DIM = 256
M_HEADS = 4
M_HEAD_DIM = 128
M_WIDTH = M_HEADS * M_HEAD_DIM
M_CONV = 4
DA_HEADS = 4
DA_HEAD_DIM = 64
DA_V_DIM = 2 * DA_HEAD_DIM
DA_WIDTH = DA_HEADS * DA_V_DIM
REL_BUCKETS = 32
REL_MAX_DIST = 128
D_FF = 2816
FFN_CONV = 3
EPS = 1e-6
NEG_BIG = -1e30
LOG2E = 1.4426950408889634

LANES = 128
SUBLANES = 8
VMEM_LIMIT = 56 * 1024 * 1024

TOKEN_TILE = 512
M_CHUNK = 256
ATT_BLOCK = 512
ATT_SUB = ATT_BLOCK // 2
FF_CHUNK = 256
N_FF_CHUNKS = D_FF // FF_CHUNK
FF_SLABS = 2 * FF_CHUNK // LANES
FF_LOOKAHEAD = 3
FF_SLAB_SETS = FF_LOOKAHEAD + 1

F32 = jnp.float32
BF16 = jnp.bfloat16

C_IF = 4 * M_WIDTH
N_IF = 2 * M_HEADS
IN_COLS = C_IF + N_IF + 3 * DA_WIDTH + 2 * D_MODEL
C_TAIL = IN_COLS - N_IF


def _dot(a, b):
    return jnp.dot(a, b, preferred_element_type=F32)


def _dot_nt(a, b):
    return lax.dot_general(a, b, (((1,), (1,)), ((), ())), preferred_element_type=F32)


def _rms(x, g):
    return x * lax.rsqrt(jnp.mean(x * x, -1, keepdims=True) + EPS) * g


def _sigmoid(x):
    return 1.0 / (1.0 + jnp.exp(-x))


def _slab_store(u, slab_ref, slab0):
    tm, n = u.shape
    for j in range(n // LANES):
        slab_ref[slab0 + j, SUBLANES:SUBLANES + tm, :] = u[:, j * LANES:(j + 1) * LANES]


def _slab_conv(tm, n, col0, slab_ref, slab0, carry_ref, cw_ref, cb_ref):
    taps = cw_ref.shape[0]
    outs = []
    for j in range(n // LANES):
        cols = slice(col0 + j * LANES, col0 + (j + 1) * LANES)
        s = slab0 + j
        slab_ref[s, 0:SUBLANES, :] = carry_ref[:, cols]
        carry_ref[:, cols] = slab_ref[s, tm:tm + SUBLANES, :]
        y = cb_ref[:, cols]
        for k in range(taps):
            y = y + cw_ref[taps - 1 - k:taps - k, cols] * slab_ref[s, pl.ds(SUBLANES - k, tm), :]
        outs.append(y)
    return jnp.concatenate(outs, axis=1)


def _const_spec(shape):
    nd = len(shape)
    return pl.BlockSpec(shape, lambda *_: (0,) * nd, pipeline_mode=pl.Buffered(1))


def _wprep_kernel(w_ref, last_ref, out_ref):
    lane = lax.broadcasted_iota(jnp.int32, (w_ref.shape[0], LANES), 1)
    out_ref[:, :C_IF] = w_ref[:, :C_IF].astype(BF16)
    tail = w_ref[:, C_IF:C_TAIL]
    n = C_TAIL - C_IF
    moved = pltpu.roll(tail, n - N_IF, axis=1)
    out_ref[:, C_IF:C_TAIL - LANES] = moved[:, :n - LANES].astype(BF16)
    end = jnp.where(lane < LANES - N_IF, moved[:, n - LANES:], pltpu.roll(last_ref[...], LANES - N_IF, axis=1))
    out_ref[:, C_TAIL - LANES:C_TAIL] = end.astype(BF16)
    out_ref[:, C_TAIL:] = jnp.where(lane < N_IF, tail[:, :LANES], 0.0).astype(BF16)


def _wprep(w):
    rb = 128
    return pl.pallas_call(
        _wprep_kernel,
        grid=(D_MODEL // rb,),
        in_specs=[pl.BlockSpec((rb, C_TAIL), lambda i: (i, 0)),
                  pl.BlockSpec((rb, LANES), lambda i: (i, C_TAIL // LANES))],
        out_specs=pl.BlockSpec((rb, C_TAIL + LANES), lambda i: (i, 0)),
        out_shape=jax.ShapeDtypeStruct((D_MODEL, C_TAIL + LANES), BF16),
        compiler_params=pltpu.CompilerParams(dimension_semantics=("arbitrary",), vmem_limit_bytes=VMEM_LIMIT),
        name="wprep",
    )(w, w)


def _inproj_kernel(x_ref, g_ref, w_ref, cw_ref, cb_ref, bif_ref,
                   mq_ref, mk_ref, mv_ref, mo_ref, dq_ref, dk_ref, dv_ref, gates_ref, ifg_ref,
                   carry_ref, slab_ref, *, tiles_per_seq):
    tm = x_ref.shape[0]
    i = pl.program_id(0)
    hb = _rms(x_ref[...], g_ref[...]).astype(BF16)

    @pl.when(i % tiles_per_seq == 0)
    def _():
        carry_ref[...] = jnp.zeros(carry_ref.shape, F32)

    def conv_silu(col0):
        _slab_store(_dot(hb, w_ref[:, col0:col0 + M_WIDTH]), slab_ref, col0 // LANES)
        y = _slab_conv(tm, M_WIDTH, col0, slab_ref, col0 // LANES, carry_ref, cw_ref, cb_ref)
        return y * _sigmoid(y)

    mq_ref[...] = conv_silu(0).astype(BF16)
    mk_ref[...] = (conv_silu(M_WIDTH) * (M_HEAD_DIM ** -0.5)).astype(BF16)

    c = 2 * M_WIDTH
    mv_ref[...] = _dot(hb, w_ref[:, c:c + M_WIDTH]).astype(BF16).T; c += M_WIDTH
    mo_ref[...] = _dot(hb, w_ref[:, c:c + M_WIDTH]).astype(BF16); c += M_WIDTH
    dq_ref[...] = (_dot(hb, w_ref[:, c:c + DA_WIDTH]) * (LOG2E * DA_HEAD_DIM ** -0.5)).astype(BF16); c += DA_WIDTH
    dk_ref[...] = _dot(hb, w_ref[:, c:c + DA_WIDTH]).astype(BF16); c += DA_WIDTH
    dv_ref[...] = _dot(hb, w_ref[:, c:c + DA_WIDTH]).astype(BF16).T; c += DA_WIDTH
    for j in range(4):
        gates_ref[:, j * 512:(j + 1) * 512] = _dot(hb, w_ref[:, c:c + 512]).astype(BF16); c += 512
    ifg_ref[...] = _dot(hb, w_ref[:, c:c + LANES]) + bif_ref[...]


def _inproj(x2, g, w_all, cw, cb, bif, seq):
    T = x2.shape[0]
    tm = TOKEN_TILE
    n_cols = w_all.shape[1]
    tps = seq // tm
    row = lambda i: (i, 0)
    bf = lambda n: jax.ShapeDtypeStruct((T, n), BF16)
    bf_t = lambda n: jax.ShapeDtypeStruct((T // seq, n, seq), BF16)
    blk = pl.BlockSpec((tm, M_WIDTH), row)
    blk_t = pl.BlockSpec((None, M_WIDTH, tm), lambda i: (i // tps, 0, i % tps))
    out_shapes = [bf(M_WIDTH), bf(M_WIDTH), bf_t(M_WIDTH), bf(M_WIDTH), bf(DA_WIDTH), bf(DA_WIDTH), bf_t(DA_WIDTH),
                  bf(2 * D_MODEL), jax.ShapeDtypeStruct((T, LANES), F32)]
    out_specs = [blk, blk, blk_t, blk, blk, blk, blk_t,
                 pl.BlockSpec((tm, 2 * D_MODEL), row), pl.BlockSpec((tm, LANES), row)]
    return pl.pallas_call(
        functools.partial(_inproj_kernel, tiles_per_seq=tps),
        grid=(T // tm,),
        in_specs=[pl.BlockSpec((tm, D_MODEL), row), _const_spec((1, D_MODEL)), _const_spec((D_MODEL, n_cols)),
                  _const_spec((M_CONV, 2 * M_WIDTH)), _const_spec((1, 2 * M_WIDTH)), _const_spec((1, LANES))],
        out_specs=out_specs,
        out_shape=out_shapes,
        scratch_shapes=[pltpu.VMEM((SUBLANES, 2 * M_WIDTH), F32),
                        pltpu.VMEM((2 * M_WIDTH // LANES, SUBLANES + tm, LANES), F32)],
        compiler_params=pltpu.CompilerParams(dimension_semantics=("arbitrary",), vmem_limit_bytes=VMEM_LIMIT),
        name="inproj",
    )(x2, g, w_all, cw, cb, bif)


def _cumsum_rows(tril, x):
    hi = x.astype(BF16)
    r1 = x - hi.astype(F32)
    mid = r1.astype(BF16)
    lo = (r1 - mid.astype(F32)).astype(BF16)
    return _dot(tril, hi) + _dot(tril, mid) + _dot(tril, lo)


def _mlstm_kernel(mq_ref, mk_ref, mv_ref, mo_ref, ifg_ref, g_ref, out_ref, ct_ref, m_ref):
    L = mq_ref.shape[0]
    d = M_HEAD_DIM

    @pl.when(pl.program_id(1) == 0)
    def _():
        ct_ref[...] = jnp.zeros(ct_ref.shape, F32)
        m_ref[...] = jnp.zeros(m_ref.shape, F32)

    ig = ifg_ref[...]
    fg = pltpu.roll(ig, LANES - M_HEADS, axis=1)
    logf = jnp.minimum(fg, 0.0) - jnp.log1p(jnp.exp(-jnp.abs(fg)))
    kidx = lax.broadcasted_iota(jnp.int32, (L, L), 0)
    qidx = lax.broadcasted_iota(jnp.int32, (L, L), 1)
    b = _cumsum_rows(jnp.where(kidx >= qidx, 1.0, 0.0).astype(BF16), logf)
    b_tot = b[L - 1:L, :]
    u = ig - b
    w_end = b_tot + u
    m_loc = jnp.max(w_end, axis=0, keepdims=True)
    a = jnp.exp(w_end - m_loc)
    m_prev = m_ref[...]
    m_new = jnp.maximum(b_tot + m_prev, m_loc)
    s_old = jnp.exp(b_tot + m_prev - m_new)
    s_loc = jnp.exp(m_loc - m_new)
    li_t = (b + m_prev).T
    b_t = b.T
    causal = kidx <= qidx

    for h in range(M_HEADS):
        hs = slice(h * d, (h + 1) * d)
        q = mq_ref[:, hs]
        k = mk_ref[:, hs]
        v_t = mv_ref[hs, :]
        ct_prev = ct_ref[h]
        s_t = _dot_nt(k, q)
        inter = _dot_nt(ct_prev.astype(BF16), q)
        log_d = jnp.where(causal, u[:, h:h + 1] + b_t[h:h + 1, :], -jnp.inf)
        li = li_t[h:h + 1, :]
        m_t = jnp.maximum(li, jnp.max(log_d, axis=0, keepdims=True))
        p_t = s_t * jnp.exp(log_d - m_t)
        s_inter = jnp.exp(li - m_t)
        num = s_inter * inter[:d] + _dot(v_t, p_t.astype(BF16))
        den = s_inter * inter[d:d + 1] + jnp.sum(p_t, axis=0, keepdims=True)
        hh = num / jnp.maximum(jnp.abs(den), jnp.exp(-m_t))
        y = (hh * lax.rsqrt(jnp.mean(hh * hh, axis=0, keepdims=True) + EPS)).T
        out_ref[:, hs] = (_sigmoid(mo_ref[:, hs].astype(F32)) * (y * g_ref[:, hs])).astype(BF16)

        ak = a[:, h:h + 1] * k.astype(F32)
        so = s_old[:, h:h + 1]
        sl = s_loc[:, h:h + 1]
        ct_ref[h, 0:d, :] = so * ct_prev[:d] + sl * _dot(v_t, ak.astype(BF16))
        ct_ref[h, d:d + 1, :] = so * ct_prev[d:d + 1] + sl * jnp.sum(ak, axis=0, keepdims=True)

    m_ref[...] = m_new


def _mlstm(mq, mk, mv, mo, ifg, g, batch, seq):
    T = mq.shape[0]
    L = M_CHUNK
    nc = seq // L
    row = lambda b, c: (b * nc + c, 0)
    blk = pl.BlockSpec((L, M_WIDTH), row)
    blk_t = pl.BlockSpec((None, M_WIDTH, L), lambda b, c: (b, 0, c))
    return pl.pallas_call(
        _mlstm_kernel,
        grid=(batch, nc),
        in_specs=[blk, blk, blk_t, blk, pl.BlockSpec((L, LANES), row), _const_spec((1, M_WIDTH))],
        out_specs=blk,
        out_shape=jax.ShapeDtypeStruct((T, M_WIDTH), BF16),
        scratch_shapes=[pltpu.VMEM((M_HEADS, M_HEAD_DIM + 2 * SUBLANES, M_HEAD_DIM), F32),
                        pltpu.VMEM((1, LANES), F32)],
        compiler_params=pltpu.CompilerParams(dimension_semantics=("arbitrary", "arbitrary"),
                                             vmem_limit_bytes=VMEM_LIMIT),
        name="mlstm",
    )(mq, mk, mv, mo, ifg, g)


def _t5_bucket(n):
    max_exact = REL_BUCKETS // 2
    nf = jnp.maximum(n, 1).astype(F32)
    large = max_exact + (jnp.log(nf / max_exact) / math.log(REL_MAX_DIST / max_exact)
                         * (REL_BUCKETS - max_exact)).astype(jnp.int32)
    large = jnp.minimum(large, REL_BUCKETS - 1)
    return jnp.where(n < max_exact, n, large)


def _attn_kernel(rb_ref, lam_ref, q_ref, k_ref, v_ref, g_ref, out_ref,
                 bias_ref, m_ref, l_ref, acc_ref, s_ref, *, lam_init):
    tq = ATT_BLOCK
    tk = tq
    sb = ATT_SUB
    nq = q_ref.shape[0] // tq
    h = pl.program_id(1)

    kk = lax.broadcasted_iota(jnp.int32, (sb, sb), 0)
    qq = lax.broadcasted_iota(jnp.int32, (sb, sb), 1)

    @pl.when((pl.program_id(0) == 0) & (h == 0))
    def _():
        for delta in range(2):
            bucket = _t5_bucket(jnp.maximum(qq + delta * sb - kk, 0))
            for hh in range(DA_HEADS):
                t = jnp.zeros((sb, sb), F32)
                for j in range(REL_BUCKETS):
                    t = jnp.where(bucket == j, rb_ref[j, hh], t)
                bias_ref[hh, delta] = (t - rb_ref[REL_BUCKETS - 1, hh]) * LOG2E

    lv = lam_ref[...]
    lam = (jnp.exp(jnp.sum(lv[0:1] * lv[1:2], axis=1, keepdims=True))
           - jnp.exp(jnp.sum(lv[2:3] * lv[3:4], axis=1, keepdims=True)) + lam_init)

    n_groups = 2 * tq // sb
    d0 = bias_ref[h, 0]
    d1 = bias_ref[h, 1]
    causal = qq >= kk
    ident = lambda s: s
    near = lambda s: jnp.concatenate([s[:sb], s[sb:] + d1], axis=0)
    early = lambda s: jnp.where(causal, s + d0, NEG_BIG)
    late = lambda s: jnp.concatenate([s[:sb] + d1, jnp.where(causal, s[sb:] + d0, NEG_BIG)], axis=0)

    def plan(j, n):
        if j == n - 1:
            return [(sb, early), (tk, late)] * (n_groups // 2)
        if j == n - 2:
            return [(tk, near), (tk, ident)] * (n_groups // 2)
        return [(tk, ident)] * n_groups

    lane = lax.broadcasted_iota(jnp.int32, (tq, DA_V_DIM), 1)
    step = 0
    for qb in range(nq):
        n = qb + 1
        st = qb % 2
        q = q_ref[qb * tq:(qb + 1) * tq, :]
        zero = jnp.zeros_like(q)
        qcat = jnp.concatenate([jnp.where(lane < DA_HEAD_DIM, q, zero),
                                jnp.where(lane >= DA_HEAD_DIM, q, zero)], axis=0)
        m_ref[st] = jnp.full(m_ref.shape[1:], -jnp.inf, F32)
        l_ref[st] = jnp.zeros(l_ref.shape[1:], F32)
        acc_ref[st] = jnp.zeros(acc_ref.shape[1:], F32)

        def scores(j, buf):
            k = k_ref[j * tk:(j + 1) * tk, :]
            for g, (rows, _) in enumerate(plan(j, n)):
                s_ref[buf, g, 0:rows, :] = _dot_nt(k[:rows], qcat[g * sb:(g + 1) * sb])

        def softmax_pv(j, buf):
            v_t = v_ref[:, j * tk:(j + 1) * tk]
            for g, (rows, fix) in enumerate(plan(j, n)):
                cols = slice(g * sb, (g + 1) * sb)
                s = fix(s_ref[buf, g, 0:rows, :])
                m_prev = m_ref[st, :, cols]
                m_new = jnp.maximum(m_prev, jnp.max(s, axis=0, keepdims=True))
                alpha = jnp.exp2(m_prev - m_new)
                p = jnp.exp2(s - m_new)
                l_ref[st, :, cols] = alpha * l_ref[st, :, cols] + jnp.sum(p, axis=0, keepdims=True)
                acc_ref[st, :, cols] = (alpha * acc_ref[st, :, cols]
                                        + _dot(v_t[:, :rows], p.astype(BF16)))
                m_ref[st, :, cols] = m_new

        scores(0, step % 2)
        for j in range(n):
            if j + 1 < n:
                scores(j + 1, (step + 1) % 2)
            softmax_pv(j, step % 2)
            step += 1

        o = acc_ref[st] * (1.0 / l_ref[st])
        out = (o[:, :tq] - lam * o[:, tq:]).T
        out_ref[qb * tq:(qb + 1) * tq, :] = (_rms(out, g_ref[...]) * (1.0 - lam_init)).astype(BF16)


def _attention(rel_bias, lam_vec, dq, dk, dv, g, batch, seq, lam_init):
    T = dq.shape[0]
    tq = ATT_BLOCK
    seq_blk = pl.BlockSpec((seq, DA_V_DIM), lambda b, h: (b, h))
    return pl.pallas_call(
        functools.partial(_attn_kernel, lam_init=lam_init),
        grid=(batch, DA_HEADS),
        in_specs=[pl.BlockSpec(memory_space=pltpu.SMEM),
                  pl.BlockSpec((4, DA_HEAD_DIM), lambda b, h: (0, 0)),
                  seq_blk, seq_blk,
                  pl.BlockSpec((None, DA_V_DIM, seq), lambda b, h: (b, h, 0)),
                  pl.BlockSpec((1, DA_V_DIM), lambda b, h: (0, h))],
        out_specs=seq_blk,
        out_shape=jax.ShapeDtypeStruct((T, DA_WIDTH), BF16),
        scratch_shapes=[pltpu.VMEM((DA_HEADS, 2, ATT_SUB, ATT_SUB), F32),
                        pltpu.VMEM((2, 1, 2 * tq), F32),
                        pltpu.VMEM((2, 1, 2 * tq), F32),
                        pltpu.VMEM((2, DA_V_DIM, 2 * tq), F32),
                        pltpu.VMEM((2, 2 * tq // ATT_SUB, tq, ATT_SUB), F32)],
        compiler_params=pltpu.CompilerParams(dimension_semantics=("arbitrary",) * 2,
                                             vmem_limit_bytes=VMEM_LIMIT),
        name="diffattn",
    )(rel_bias, lam_vec, dq, dk, dv, g)


def _gelu_tanh(x):
    return 0.5 * x * (1.0 + jnp.tanh(math.sqrt(2.0 / math.pi) * (x + 0.044715 * (x * x * x))))


def _ffn_kernel(gm_ref, hd_ref, gates_ref, x_ref, p_ref, wm_ref, wd_ref, wo_ref,
                gf_ref, wup_ref, cw_ref, cb_ref, wdn_ref, gp_ref, wpg_ref, wp_ref, gl_ref,
                out_ref, carry_ref, slab_ref, acc_ref, *, tiles_per_seq, final_norm):
    tm = x_ref.shape[0]
    i = pl.program_id(0)

    ya = _dot(gm_ref[...], wm_ref[...])
    yb = _dot(hd_ref[...], wd_ref[...])
    ga = _sigmoid(gates_ref[:, :D_MODEL].astype(F32))
    gb = _sigmoid(gates_ref[:, D_MODEL:].astype(F32))
    x1 = x_ref[...] + _dot((ga * ya + gb * yb).astype(BF16), wo_ref[...])

    hb = _rms(x1, gf_ref[...]).astype(BF16)

    @pl.when(i % tiles_per_seq == 0)
    def _():
        carry_ref[...] = jnp.zeros(carry_ref.shape, F32)

    def slab0(c, half):
        return (c % FF_SLAB_SETS) * FF_SLABS + half * (FF_CHUNK // LANES)

    def up(c):
        for half, off in enumerate((0, D_FF)):
            lo = off + c * FF_CHUNK
            _slab_store(_dot(hb, wup_ref[:, lo:lo + FF_CHUNK]), slab_ref, slab0(c, half))

    def conv(c, half):
        return _slab_conv(tm, FF_CHUNK, half * D_FF + c * FF_CHUNK, slab_ref, slab0(c, half),
                          carry_ref, cw_ref, cb_ref)

    for c in range(FF_LOOKAHEAD):
        up(c)
    for c in range(N_FF_CHUNKS):
        if c + FF_LOOKAHEAD < N_FF_CHUNKS:
            up(c + FF_LOOKAHEAD)
        act = (_gelu_tanh(conv(c, 1)) * conv(c, 0)).astype(BF16)
        d = _dot(act, wdn_ref[c * FF_CHUNK:(c + 1) * FF_CHUNK, :])
        if c == 0:
            acc_ref[...] = d
        else:
            acc_ref[...] += d

    x2 = x1 + acc_ref[...]
    hg = _rms(x2, gp_ref[...]).astype(BF16)
    gate = _sigmoid(_dot(hg, wpg_ref[...]))
    pe = _dot(p_ref[...].astype(BF16), wp_ref[...])
    x3 = x2 + gate * pe
    out_ref[...] = _rms(x3, gl_ref[...]) if final_norm else x3


def _ffn(gm, hd, gates, x2, p2, wm, wd, wo, gf, wup, cw, cb, wdn, gp, wpg, wp, gl, seq, final_norm):
    T = x2.shape[0]
    tm = TOKEN_TILE
    row = lambda i: (i, 0)
    return pl.pallas_call(
        functools.partial(_ffn_kernel, tiles_per_seq=seq // tm, final_norm=final_norm),
        grid=(T // tm,),
        in_specs=[pl.BlockSpec((tm, M_WIDTH), row), pl.BlockSpec((tm, DA_WIDTH), row),
                  pl.BlockSpec((tm, 2 * D_MODEL), row), pl.BlockSpec((tm, D_MODEL), row),
                  pl.BlockSpec((tm, PLE_DIM), row),
                  _const_spec((M_WIDTH, D_MODEL)), _const_spec((DA_WIDTH, D_MODEL)),
                  _const_spec((D_MODEL, D_MODEL)), _const_spec((1, D_MODEL)),
                  _const_spec((D_MODEL, 2 * D_FF)), _const_spec((FFN_CONV, 2 * D_FF)),
                  _const_spec((1, 2 * D_FF)), _const_spec((D_FF, D_MODEL)),
                  _const_spec((1, D_MODEL)), _const_spec((D_MODEL, D_MODEL)),
                  _const_spec((PLE_DIM, D_MODEL)), _const_spec((1, D_MODEL))],
        out_specs=pl.BlockSpec((tm, D_MODEL), row),
        out_shape=jax.ShapeDtypeStruct((T, D_MODEL), F32),
        scratch_shapes=[pltpu.VMEM((SUBLANES, 2 * D_FF), F32),
                        pltpu.VMEM((FF_SLAB_SETS * FF_SLABS, SUBLANES + tm, LANES), F32),
                        pltpu.VMEM((tm, D_MODEL), F32)],
        compiler_params=pltpu.CompilerParams(dimension_semantics=("arbitrary",), vmem_limit_bytes=VMEM_LIMIT),
        name="convffn",
    )(gm, hd, gates, x2, p2, wm, wd, wo, gf, wup, cw, cb, wdn, gp, wpg, wp, gl)


def kernel(x, p, rel_bias, norm_mix_g, w_in, b_if, m_conv_w, m_conv_b, m_norm_g, da_lambda, da_norm_g,
           w_br_m, w_br_d, w_out, norm_ffn_g, w_up, ffn_conv_w, ffn_conv_b, w_down, norm_ple_g, w_ple_gate,
           w_ple, norm_final_g):
    batch, seq, _ = x.shape
    depth = w_in.shape[0]
    T = batch * seq
    assert seq % TOKEN_TILE == 0 and seq % M_CHUNK == 0 and seq % ATT_BLOCK == 0
    xt = x.reshape(T, D_MODEL)
    row = lambda v: v.reshape(1, -1).astype(F32)

    for l in range(depth):
        w_all = _wprep(w_in[l].astype(F32))
        bif = jnp.concatenate([b_if[l].astype(F32), jnp.zeros((LANES - N_IF,), F32)]).reshape(1, LANES)

        mq, mk, mv, mo, dq, dk, dv, gates, ifg = _inproj(
            xt, row(norm_mix_g[l]), w_all, m_conv_w[l].astype(F32), row(m_conv_b[l]), bif, seq)

        gm = _mlstm(mq, mk, mv, mo, ifg, row(m_norm_g[l]), batch, seq)

        lam_init = 0.8 - 0.6 * math.exp(-0.3 * l)
        hd = _attention(rel_bias.astype(F32), da_lambda[l].astype(F32), dq, dk, dv, row(da_norm_g[l]),
                        batch, seq, lam_init)

        xt = _ffn(gm, hd, gates, xt, p[l].reshape(T, PLE_DIM),
                  w_br_m[l].astype(BF16), w_br_d[l].astype(BF16), w_out[l].astype(BF16),
                  row(norm_ffn_g[l]), w_up[l].astype(BF16),
                  ffn_conv_w[l].astype(F32), row(ffn_conv_b[l]), w_down[l].astype(BF16), row(norm_ple_g[l]),
                  w_ple_gate[l].astype(BF16), w_ple[l].astype(BF16), row(norm_final_g), seq,
                  final_norm=(l == depth - 1))

    return xt.reshape(batch, seq, D_MODEL)
```

```python
import functools
import math

import jax
import jax.numpy as jnp
from jax import lax
from jax.experimental import pallas as pl
from jax.experimental.pallas import tpu as pltpu

D_MODEL = 1024
PLE_DIM = 256
M_HEADS = 4
M_HEAD_DIM = 128
M_WIDTH = M_HEADS * M_HEAD_DIM
M_CONV = 4
DA_HEADS = 4
DA_HEAD_DIM = 64
DA_V_DIM = 2 * DA_HEAD_DIM
DA_WIDTH = DA_HEADS * DA_V_DIM
REL_BUCKETS = 32
REL_MAX_DIST = 128
D_FF = 2816
FFN_CONV = 3
EPS = 1e-6
NEG_BIG = -1e30
LOG2E = 1.4426950408889634

LANES = 128
SUBLANES = 8
VMEM_LIMIT = 56 * 1024 * 1024

TOKEN_TILE = 512
M_CHUNK = 256
M_BATCH = 4
ATT_BLOCK = 512
ATT_SUB = ATT_BLOCK // 2
FF_CHUNK = 256
N_FF_CHUNKS = D_FF // FF_CHUNK
FF_SLABS = 2 * FF_CHUNK // LANES
FF_LOOKAHEAD = 3
FF_SLAB_SETS = FF_LOOKAHEAD + 1

F32 = jnp.float32
BF16 = jnp.bfloat16

C_IF = 4 * M_WIDTH
N_IF = 2 * M_HEADS
IN_COLS = C_IF + N_IF + 3 * DA_WIDTH + 2 * D_MODEL
C_TAIL = IN_COLS - N_IF


def _dot(a, b):
    return jnp.dot(a, b, preferred_element_type=F32)


def _dot_nt(a, b):
    return lax.dot_general(a, b, (((1,), (1,)), ((), ())), preferred_element_type=F32)


def _rms(x, g):
    return x * lax.rsqrt(jnp.mean(x * x, -1, keepdims=True) + EPS) * g


def _sigmoid(x):
    return 1.0 / (1.0 + jnp.exp(-x))


def _slab_store(u, slab_ref, slab0):
    tm, n = u.shape
    for j in range(n // LANES):
        slab_ref[slab0 + j, SUBLANES:SUBLANES + tm, :] = u[:, j * LANES:(j + 1) * LANES]


def _slab_conv(tm, n, col0, slab_ref, slab0, carry_ref, cw_ref, cb_ref):
    taps = cw_ref.shape[0]
    outs = []
    for j in range(n // LANES):
        cols = slice(col0 + j * LANES, col0 + (j + 1) * LANES)
        s = slab0 + j
        slab_ref[s, 0:SUBLANES, :] = carry_ref[:, cols]
        carry_ref[:, cols] = slab_ref[s, tm:tm + SUBLANES, :]
        y = cb_ref[:, cols]
        for k in range(taps):
            y = y + cw_ref[taps - 1 - k:taps - k, cols] * slab_ref[s, pl.ds(SUBLANES - k, tm), :]
        outs.append(y)
    return jnp.concatenate(outs, axis=1)


def _cumsum_rows(tril, x):
    hi = x.astype(BF16)
    r1 = x - hi.astype(F32)
    mid = r1.astype(BF16)
    lo = (r1 - mid.astype(F32)).astype(BF16)
    return _dot(tril, hi) + _dot(tril, mid) + _dot(tril, lo)


def _const_spec(shape):
    nd = len(shape)
    return pl.BlockSpec(shape, lambda *_: (0,) * nd, pipeline_mode=pl.Buffered(1))


W_BLOCK = 128


def _wprep_kernel(w_ref, out_ref):
    j = pl.program_id(0)
    row = lax.broadcasted_iota(jnp.int32, w_ref.shape, 0)
    keep = (j < C_TAIL // W_BLOCK) | (row < N_IF)
    out_ref[...] = jnp.where(keep, w_ref[...], 0.0).astype(BF16)


def _wprep(w_t):
    n_main = C_TAIL // W_BLOCK

    def src(j):
        shifted = j * W_BLOCK + jnp.where(j >= C_IF // W_BLOCK, N_IF, 0)
        return pl.multiple_of(jnp.where(j < n_main, shifted, C_IF), SUBLANES), 0

    return pl.pallas_call(
        _wprep_kernel,
        grid=(n_main + 1,),
        in_specs=[pl.BlockSpec((pl.Element(W_BLOCK), pl.Element(D_MODEL)), src)],
        out_specs=pl.BlockSpec((W_BLOCK, D_MODEL), lambda j: (j, 0)),
        out_shape=jax.ShapeDtypeStruct((C_TAIL + W_BLOCK, D_MODEL), BF16),
        compiler_params=pltpu.CompilerParams(dimension_semantics=("arbitrary",), vmem_limit_bytes=VMEM_LIMIT),
        name="wprep",
    )(w_t)


def _inproj_kernel(x_ref, g_ref, w_ref, cw_ref, cb_ref, bif_ref,
                   mq_ref, mk_ref, mv_ref, mo_ref, dq_ref, dk_ref, dv_ref, gates_ref, ifg_ref,
                   carry_ref, slab_ref, *, tiles_per_seq):
    tm = x_ref.shape[0]
    i = pl.program_id(0)
    hb = _rms(x_ref[...], g_ref[...]).astype(BF16)

    pre = _dot_nt(hb, w_ref[C_TAIL:C_TAIL + LANES, :]) + bif_ref[...]
    lane = lax.broadcasted_iota(jnp.int32, pre.shape, 1)
    ifg_ref[...] = jnp.where(lane < M_HEADS, pre, jnp.minimum(pre, 0.0) - jnp.log1p(jnp.exp(-jnp.abs(pre))))

    @pl.when(i % tiles_per_seq == 0)
    def _():
        carry_ref[...] = jnp.zeros(carry_ref.shape, F32)

    def conv_silu(col0):
        _slab_store(_dot_nt(hb, w_ref[col0:col0 + M_WIDTH, :]), slab_ref, col0 // LANES)
        y = _slab_conv(tm, M_WIDTH, col0, slab_ref, col0 // LANES, carry_ref, cw_ref, cb_ref)
        return y * _sigmoid(y)

    mq_ref[...] = conv_silu(0).astype(BF16)
    mk_ref[...] = (conv_silu(M_WIDTH) * (M_HEAD_DIM ** -0.5)).astype(BF16)

    c = 2 * M_WIDTH
    mv_ref[...] = _dot_nt(hb, w_ref[c:c + M_WIDTH, :]).astype(BF16).T; c += M_WIDTH
    mo_ref[...] = _dot_nt(hb, w_ref[c:c + M_WIDTH, :]).astype(BF16); c += M_WIDTH
    dq_ref[...] = (_dot_nt(hb, w_ref[c:c + DA_WIDTH, :]) * (LOG2E * DA_HEAD_DIM ** -0.5)).astype(BF16); c += DA_WIDTH
    dk_ref[...] = _dot_nt(hb, w_ref[c:c + DA_WIDTH, :]).astype(BF16); c += DA_WIDTH
    dv_ref[...] = _dot_nt(hb, w_ref[c:c + DA_WIDTH, :]).astype(BF16).T; c += DA_WIDTH
    for j in range(4):
        gates_ref[:, j * 512:(j + 1) * 512] = _dot_nt(hb, w_ref[c:c + 512, :]).astype(BF16); c += 512


def _inproj(x2, g, w_all, cw, cb, bif, seq):
    T = x2.shape[0]
    tm = TOKEN_TILE
    tps = seq // tm
    row = lambda i: (i, 0)
    bf = lambda n: jax.ShapeDtypeStruct((T, n), BF16)
    bf_t = lambda n: jax.ShapeDtypeStruct((T // seq, n, seq), BF16)
    blk = pl.BlockSpec((tm, M_WIDTH), row)
    blk_t = pl.BlockSpec((None, M_WIDTH, tm), lambda i: (i // tps, 0, i % tps))
    out_shapes = [bf(M_WIDTH), bf(M_WIDTH), bf_t(M_WIDTH), bf(M_WIDTH), bf(DA_WIDTH), bf(DA_WIDTH), bf_t(DA_WIDTH),
                  bf(2 * D_MODEL), jax.ShapeDtypeStruct((T, LANES), F32)]
    out_specs = [blk, blk, blk_t, blk, blk, blk, blk_t,
                 pl.BlockSpec((tm, 2 * D_MODEL), row), pl.BlockSpec((tm, LANES), row)]
    return pl.pallas_call(
        functools.partial(_inproj_kernel, tiles_per_seq=tps),
        grid=(T // tm,),
        in_specs=[pl.BlockSpec((tm, D_MODEL), row), _const_spec((1, D_MODEL)), _const_spec(w_all.shape),
                  _const_spec((M_CONV, 2 * M_WIDTH)), _const_spec((1, 2 * M_WIDTH)), _const_spec((1, LANES))],
        out_specs=out_specs,
        out_shape=out_shapes,
        scratch_shapes=[pltpu.VMEM((SUBLANES, 2 * M_WIDTH), F32),
                        pltpu.VMEM((2 * M_WIDTH // LANES, SUBLANES + tm, LANES), F32)],
        compiler_params=pltpu.CompilerParams(dimension_semantics=("arbitrary",), vmem_limit_bytes=VMEM_LIMIT),
        name="inproj",
    )(x2, g, w_all, cw, cb, bif)


def _mlstm_kernel(mq_ref, mk_ref, mv_ref, mo_ref, ifg_ref, g_ref, out_ref, ct_ref, m_ref):
    L = mq_ref.shape[1]
    d = M_HEAD_DIM

    @pl.when(pl.program_id(1) == 0)
    def _():
        ct_ref[...] = jnp.zeros(ct_ref.shape, F32)
        m_ref[...] = jnp.zeros(m_ref.shape, F32)

    kidx = lax.broadcasted_iota(jnp.int32, (L, L), 0)
    qidx = lax.broadcasted_iota(jnp.int32, (L, L), 1)
    tril = jnp.where(kidx >= qidx, 1.0, 0.0).astype(BF16)
    causal = kidx <= qidx

    for s in range(mq_ref.shape[0]):
        ig = ifg_ref[s]
        b = _cumsum_rows(tril, pltpu.roll(ig, LANES - M_HEADS, axis=1))
        b_tot = b[L - 1:L, :]
        u = ig - b
        w_end = b_tot + u
        m_loc = jnp.max(w_end, axis=0, keepdims=True)
        a = jnp.exp(w_end - m_loc)
        m_prev = m_ref[s]
        m_new = jnp.maximum(b_tot + m_prev, m_loc)
        s_old = jnp.exp(b_tot + m_prev - m_new)
        s_loc = jnp.exp(m_loc - m_new)
        li_t = (b + m_prev).T
        b_t = b.T
        m_ref[s] = m_new

        for h in range(M_HEADS):
            hs = slice(h * d, (h + 1) * d)
            q = mq_ref[s, :, hs]
            k = mk_ref[s, :, hs]
            v_t = mv_ref[s, hs, :]
            ct_prev = ct_ref[s, h]
            s_t = _dot_nt(k, q)
            inter = _dot_nt(ct_prev.astype(BF16), q)
            log_d = jnp.where(causal, u[:, h:h + 1] + b_t[h:h + 1, :], -jnp.inf)
            li = li_t[h:h + 1, :]
            m_t = jnp.maximum(li, jnp.max(log_d, axis=0, keepdims=True))
            p_t = s_t * jnp.exp(log_d - m_t)
            s_inter = jnp.exp(li - m_t)
            num = s_inter * inter[:d] + _dot(v_t, p_t.astype(BF16))
            den = s_inter * inter[d:d + 1] + jnp.sum(p_t, axis=0, keepdims=True)
            hh = num / jnp.maximum(jnp.abs(den), jnp.exp(-m_t))
            y = (hh * lax.rsqrt(jnp.mean(hh * hh, axis=0, keepdims=True) + EPS)).T
            out_ref[s, :, hs] = (_sigmoid(mo_ref[s, :, hs].astype(F32)) * (y * g_ref[:, hs])).astype(BF16)

            ak = a[:, h:h + 1] * k.astype(F32)
            so = s_old[:, h:h + 1]
            sl = s_loc[:, h:h + 1]
            ct_ref[s, h, 0:d, :] = so * ct_prev[:d] + sl * _dot(v_t, ak.astype(BF16))
            ct_ref[s, h, d:d + 1, :] = so * ct_prev[d:d + 1] + sl * jnp.sum(ak, axis=0, keepdims=True)


def _mlstm(mq, mk, mv, mo, ifg, g, batch, seq):
    L = M_CHUNK
    nb = M_BATCH
    seq3 = lambda t: t.reshape(batch, seq, t.shape[-1])
    blk = pl.BlockSpec((nb, L, M_WIDTH), lambda b, c: (b, c, 0))
    blk_t = pl.BlockSpec((nb, M_WIDTH, L), lambda b, c: (b, 0, c))
    out = pl.pallas_call(
        _mlstm_kernel,
        grid=(batch // nb, seq // L),
        in_specs=[blk, blk, blk_t, blk, pl.BlockSpec((nb, L, LANES), lambda b, c: (b, c, 0)),
                  _const_spec((1, M_WIDTH))],
        out_specs=blk,
        out_shape=jax.ShapeDtypeStruct((batch, seq, M_WIDTH), BF16),
        scratch_shapes=[pltpu.VMEM((nb, M_HEADS, M_HEAD_DIM + 2 * SUBLANES, M_HEAD_DIM), F32),
                        pltpu.VMEM((nb, 1, LANES), F32)],
        compiler_params=pltpu.CompilerParams(dimension_semantics=("arbitrary", "arbitrary"),
                                             vmem_limit_bytes=VMEM_LIMIT),
        name="mlstm",
    )(seq3(mq), seq3(mk), mv, seq3(mo), seq3(ifg), g)
    return out.reshape(batch * seq, M_WIDTH)


def _t5_bucket(n):
    max_exact = REL_BUCKETS // 2
    nf = jnp.maximum(n, 1).astype(F32)
    large = max_exact + (jnp.log(nf / max_exact) / math.log(REL_MAX_DIST / max_exact)
                         * (REL_BUCKETS - max_exact)).astype(jnp.int32)
    large = jnp.minimum(large, REL_BUCKETS - 1)
    return jnp.where(n < max_exact, n, large)


def _attn_kernel(rb_ref, lam_ref, q_ref, k_ref, v_ref, g_ref, out_ref,
                 bias_ref, m_ref, l_ref, acc_ref, s_ref, *, lam_init):
    tq = ATT_BLOCK
    tk = tq
    sb = ATT_SUB
    nq = q_ref.shape[0] // tq
    h = pl.program_id(1)

    kk = lax.broadcasted_iota(jnp.int32, (sb, sb), 0)
    qq = lax.broadcasted_iota(jnp.int32, (sb, sb), 1)

    @pl.when((pl.program_id(0) == 0) & (h == 0))
    def _():
        for delta in range(2):
            bucket = _t5_bucket(jnp.maximum(qq + delta * sb - kk, 0))
            for hh in range(DA_HEADS):
                t = jnp.zeros((sb, sb), F32)
                for j in range(REL_BUCKETS):
                    t = jnp.where(bucket == j, rb_ref[j, hh], t)
                bias_ref[hh, delta] = (t - rb_ref[REL_BUCKETS - 1, hh]) * LOG2E

    lv = lam_ref[...]
    lam = (jnp.exp(jnp.sum(lv[0:1] * lv[1:2], axis=1, keepdims=True))
           - jnp.exp(jnp.sum(lv[2:3] * lv[3:4], axis=1, keepdims=True)) + lam_init)

    n_groups = 2 * tq // sb
    d0 = bias_ref[h, 0]
    d1 = bias_ref[h, 1]
    causal = qq >= kk
    ident = lambda s: s
    near = lambda s: jnp.concatenate([s[:sb], s[sb:] + d1], axis=0)
    early = lambda s: jnp.where(causal, s + d0, NEG_BIG)
    late = lambda s: jnp.concatenate([s[:sb] + d1, jnp.where(causal, s[sb:] + d0, NEG_BIG)], axis=0)

    def plan(j, n):
        if j == n - 1:
            return [(sb, early), (tk, late)] * (n_groups // 2)
        if j == n - 2:
            return [(tk, near), (tk, ident)] * (n_groups // 2)
        return [(tk, ident)] * n_groups

    lane = lax.broadcasted_iota(jnp.int32, (tq, DA_V_DIM), 1)
    step = 0
    for qb in range(nq):
        n = qb + 1
        st = qb % 2
        q = q_ref[qb * tq:(qb + 1) * tq, :]
        zero = jnp.zeros_like(q)
        qcat = jnp.concatenate([jnp.where(lane < DA_HEAD_DIM, q, zero),
                                jnp.where(lane >= DA_HEAD_DIM, q, zero)], axis=0)
        m_ref[st] = jnp.full(m_ref.shape[1:], -jnp.inf, F32)
        l_ref[st] = jnp.zeros(l_ref.shape[1:], F32)
        acc_ref[st] = jnp.zeros(acc_ref.shape[1:], F32)

        def scores(j, buf):
            k = k_ref[j * tk:(j + 1) * tk, :]
            for g, (rows, _) in enumerate(plan(j, n)):
                s_ref[buf, g, 0:rows, :] = _dot_nt(k[:rows], qcat[g * sb:(g + 1) * sb])

        def softmax_pv(j, buf):
            v_t = v_ref[:, j * tk:(j + 1) * tk]
            for g, (rows, fix) in enumerate(plan(j, n)):
                cols = slice(g * sb, (g + 1) * sb)
                s = fix(s_ref[buf, g, 0:rows, :])
                m_prev = m_ref[st, :, cols]
                m_new = jnp.maximum(m_prev, jnp.max(s, axis=0, keepdims=True))
                alpha = jnp.exp2(m_prev - m_new)
                p = jnp.exp2(s - m_new)
                l_ref[st, :, cols] = alpha * l_ref[st, :, cols] + jnp.sum(p, axis=0, keepdims=True)
                acc_ref[st, :, cols] = (alpha * acc_ref[st, :, cols]
                                        + _dot(v_t[:, :rows], p.astype(BF16)))
                m_ref[st, :, cols] = m_new

        scores(0, step % 2)
        for j in range(n):
            if j + 1 < n:
                scores(j + 1, (step + 1) % 2)
            softmax_pv(j, step % 2)
            step += 1

        o = acc_ref[st] * (1.0 / l_ref[st])
        out = (o[:, :tq] - lam * o[:, tq:]).T
        out_ref[qb * tq:(qb + 1) * tq, :] = (_rms(out, g_ref[...]) * (1.0 - lam_init)).astype(BF16)


def _attention(rel_bias, lam_vec, dq, dk, dv, g, batch, seq, lam_init):
    T = dq.shape[0]
    tq = ATT_BLOCK
    seq_blk = pl.BlockSpec((seq, DA_V_DIM), lambda b, h: (b, h))
    return pl.pallas_call(
        functools.partial(_attn_kernel, lam_init=lam_init),
        grid=(batch, DA_HEADS),
        in_specs=[pl.BlockSpec(memory_space=pltpu.SMEM),
                  pl.BlockSpec((4, DA_HEAD_DIM), lambda b, h: (0, 0)),
                  seq_blk, seq_blk,
                  pl.BlockSpec((None, DA_V_DIM, seq), lambda b, h: (b, h, 0)),
                  pl.BlockSpec((1, DA_V_DIM), lambda b, h: (0, h))],
        out_specs=seq_blk,
        out_shape=jax.ShapeDtypeStruct((T, DA_WIDTH), BF16),
        scratch_shapes=[pltpu.VMEM((DA_HEADS, 2, ATT_SUB, ATT_SUB), F32),
                        pltpu.VMEM((2, 1, 2 * tq), F32),
                        pltpu.VMEM((2, 1, 2 * tq), F32),
                        pltpu.VMEM((2, DA_V_DIM, 2 * tq), F32),
                        pltpu.VMEM((2, 2 * tq // ATT_SUB, tq, ATT_SUB), F32)],
        compiler_params=pltpu.CompilerParams(dimension_semantics=("arbitrary",) * 2,
                                             vmem_limit_bytes=VMEM_LIMIT),
        name="diffattn",
    )(rel_bias, lam_vec, dq, dk, dv, g)


def _gelu_tanh(x):
    return 0.5 * x * (1.0 + jnp.tanh(math.sqrt(2.0 / math.pi) * (x + 0.044715 * (x * x * x))))


def _ffn_kernel(gm_ref, hd_ref, gates_ref, x_ref, p_ref, wm_ref, wd_ref, wo_ref,
                gf_ref, wup_ref, cw_ref, cb_ref, wdn_ref, gp_ref, wpg_ref, wp_ref, gl_ref,
                out_ref, carry_ref, slab_ref, acc_ref, *, tiles_per_seq, final_norm):
    tm = x_ref.shape[0]
    i = pl.program_id(0)

    ya = _dot(gm_ref[...], wm_ref[...])
    yb = _dot(hd_ref[...], wd_ref[...])
    ga = _sigmoid(gates_ref[:, :D_MODEL].astype(F32))
    gb = _sigmoid(gates_ref[:, D_MODEL:].astype(F32))
    x1 = x_ref[...] + _dot((ga * ya + gb * yb).astype(BF16), wo_ref[...])

    hb = _rms(x1, gf_ref[...]).astype(BF16)

    @pl.when(i % tiles_per_seq == 0)
    def _():
        carry_ref[...] = jnp.zeros(carry_ref.shape, F32)

    def slab0(c, half):
        return (c % FF_SLAB_SETS) * FF_SLABS + half * (FF_CHUNK // LANES)

    def up(c):
        for half, off in enumerate((0, D_FF)):
            lo = off + c * FF_CHUNK
            _slab_store(_dot(hb, wup_ref[:, lo:lo + FF_CHUNK]), slab_ref, slab0(c, half))

    def conv(c, half):
        return _slab_conv(tm, FF_CHUNK, half * D_FF + c * FF_CHUNK, slab_ref, slab0(c, half),
                          carry_ref, cw_ref, cb_ref)

    for c in range(FF_LOOKAHEAD):
        up(c)
    for c in range(N_FF_CHUNKS):
        if c + FF_LOOKAHEAD < N_FF_CHUNKS:
            up(c + FF_LOOKAHEAD)
        act = (_gelu_tanh(conv(c, 1)) * conv(c, 0)).astype(BF16)
        d = _dot(act, wdn_ref[c * FF_CHUNK:(c + 1) * FF_CHUNK, :])
        if c == 0:
            acc_ref[...] = d
        else:
            acc_ref[...] += d

    x2 = x1 + acc_ref[...]
    hg = _rms(x2, gp_ref[...]).astype(BF16)
    gate = _sigmoid(_dot(hg, wpg_ref[...]))
    pe = _dot(p_ref[...].astype(BF16), wp_ref[...])
    x3 = x2 + gate * pe
    out_ref[...] = _rms(x3, gl_ref[...]) if final_norm else x3


def _ffn(gm, hd, gates, x2, p2, wm, wd, wo, gf, wup, cw, cb, wdn, gp, wpg, wp, gl, seq, final_norm):
    T = x2.shape[0]
    tm = TOKEN_TILE
    row = lambda i: (i, 0)
    return pl.pallas_call(
        functools.partial(_ffn_kernel, tiles_per_seq=seq // tm, final_norm=final_norm),
        grid=(T // tm,),
        in_specs=[pl.BlockSpec((tm, M_WIDTH), row), pl.BlockSpec((tm, DA_WIDTH), row),
                  pl.BlockSpec((tm, 2 * D_MODEL), row), pl.BlockSpec((tm, D_MODEL), row),
                  pl.BlockSpec((tm, PLE_DIM), row),
                  _const_spec((M_WIDTH, D_MODEL)), _const_spec((DA_WIDTH, D_MODEL)),
                  _const_spec((D_MODEL, D_MODEL)), _const_spec((1, D_MODEL)),
                  _const_spec((D_MODEL, 2 * D_FF)), _const_spec((FFN_CONV, 2 * D_FF)),
                  _const_spec((1, 2 * D_FF)), _const_spec((D_FF, D_MODEL)),
                  _const_spec((1, D_MODEL)), _const_spec((D_MODEL, D_MODEL)),
                  _const_spec((PLE_DIM, D_MODEL)), _const_spec((1, D_MODEL))],
        out_specs=pl.BlockSpec((tm, D_MODEL), row),
        out_shape=jax.ShapeDtypeStruct((T, D_MODEL), F32),
        scratch_shapes=[pltpu.VMEM((SUBLANES, 2 * D_FF), F32),
                        pltpu.VMEM((FF_SLAB_SETS * FF_SLABS, SUBLANES + tm, LANES), F32),
                        pltpu.VMEM((tm, D_MODEL), F32)],
        compiler_params=pltpu.CompilerParams(dimension_semantics=("arbitrary",), vmem_limit_bytes=VMEM_LIMIT),
        name="convffn",
    )(gm, hd, gates, x2, p2, wm, wd, wo, gf, wup, cw, cb, wdn, gp, wpg, wp, gl)


def kernel(x, p, rel_bias, norm_mix_g, w_in, b_if, m_conv_w, m_conv_b, m_norm_g, da_lambda, da_norm_g,
           w_br_m, w_br_d, w_out, norm_ffn_g, w_up, ffn_conv_w, ffn_conv_b, w_down, norm_ple_g, w_ple_gate,
           w_ple, norm_final_g):
    batch, seq, _ = x.shape
    depth = w_in.shape[0]
    T = batch * seq
    assert seq % TOKEN_TILE == 0 and seq % M_CHUNK == 0 and seq % ATT_BLOCK == 0 and batch % M_BATCH == 0
    xt = x.reshape(T, D_MODEL)
    row = lambda v: v.reshape(1, -1).astype(F32)

    for l in range(depth):
        w_all = _wprep(jnp.swapaxes(w_in[l].astype(F32), 0, 1))
        bif = jnp.concatenate([b_if[l].astype(F32), jnp.zeros((LANES - N_IF,), F32)]).reshape(1, LANES)

        mq, mk, mv, mo, dq, dk, dv, gates, ifg = _inproj(
            xt, row(norm_mix_g[l]), w_all, m_conv_w[l].astype(F32), row(m_conv_b[l]), bif, seq)

        gm = _mlstm(mq, mk, mv, mo, ifg, row(m_norm_g[l]), batch, seq)

        lam_init = 0.8 - 0.6 * math.exp(-0.3 * l)
        hd = _attention(rel_bias.astype(F32), da_lambda[l].astype(F32), dq, dk, dv, row(da_norm_g[l]),
                        batch, seq, lam_init)

        xt = _ffn(gm, hd, gates, xt, p[l].reshape(T, PLE_DIM),
                  w_br_m[l].astype(BF16), w_br_d[l].astype(BF16), w_out[l].astype(BF16),
                  row(norm_ffn_g[l]), w_up[l].astype(BF16),
                  ffn_conv_w[l].astype(F32), row(ffn_conv_b[l]), w_down[l].astype(BF16), row(norm_ple_g[l]),
                  w_ple_gate[l].astype(BF16), w_ple[l].astype(BF16), row(norm_final_g), seq,
                  final_norm=(l == depth - 1))

    return xt.reshape(batch, seq, D_MODEL)
```

```python
import functools
import math

import jax
import jax.numpy as jnp
from jax import lax
from jax.experimental import pallas as pl
from jax.experimental.pallas import tpu as pltpu

D_MODEL = 1024
PLE_DIM = 256
M_HEADS = 4
M_HEAD_DIM = 128
M_WIDTH = M_HEADS * M_HEAD_DIM
M_CONV = 4
DA_HEADS = 4
DA_HEAD_DIM = 64
DA_V_DIM = 2 * DA_HEAD_DIM
DA_WIDTH = DA_HEADS * DA_V_DIM
REL_BUCKETS = 32
REL_MAX_DIST = 128
D_FF = 2816
FFN_CONV = 3
EPS = 1e-6
NEG_BIG = -1e30
LOG2E = 1.4426950408889634

LANES = 128
SUBLANES = 8
VMEM_LIMIT = 56 * 1024 * 1024

TOKEN_TILE = 512
M_CHUNK = 256
M_BATCH = 4
ATT_BLOCK = 512
ATT_SUB = ATT_BLOCK // 2
FF_CHUNK = 256
N_FF_CHUNKS = D_FF // FF_CHUNK
FF_SLABS = 2 * FF_CHUNK // LANES
FF_LOOKAHEAD = 3
FF_SLAB_SETS = FF_LOOKAHEAD + 1

F32 = jnp.float32
BF16 = jnp.bfloat16

C_IF = 4 * M_WIDTH
N_IF = 2 * M_HEADS
IN_COLS = C_IF + N_IF + 3 * DA_WIDTH + 2 * D_MODEL
C_TAIL = IN_COLS - N_IF


def _dot(a, b):
    return jnp.dot(a, b, preferred_element_type=F32)


def _dot_nt(a, b):
    return lax.dot_general(a, b, (((1,), (1,)), ((), ())), preferred_element_type=F32)


def _rms(x, g):
    return x * lax.rsqrt(jnp.mean(x * x, -1, keepdims=True) + EPS) * g


def _sigmoid(x):
    return 1.0 / (1.0 + jnp.exp(-x))


def _slab_store(u, slab_ref, slab0):
    tm, n = u.shape
    for j in range(n // LANES):
        slab_ref[slab0 + j, SUBLANES:SUBLANES + tm, :] = u[:, j * LANES:(j + 1) * LANES]


def _slab_conv(tm, n, col0, slab_ref, slab0, carry_ref, cw_ref, cb_ref):
    taps = cw_ref.shape[0]
    outs = []
    for j in range(n // LANES):
        cols = slice(col0 + j * LANES, col0 + (j + 1) * LANES)
        s = slab0 + j
        slab_ref[s, 0:SUBLANES, :] = carry_ref[:, cols]
        carry_ref[:, cols] = slab_ref[s, tm:tm + SUBLANES, :]
        y = cb_ref[:, cols]
        for k in range(taps):
            y = y + cw_ref[taps - 1 - k:taps - k, cols] * slab_ref[s, pl.ds(SUBLANES - k, tm), :]
        outs.append(y)
    return jnp.concatenate(outs, axis=1)


def _cumsum_rows(tril, x):
    hi = x.astype(BF16)
    r1 = x - hi.astype(F32)
    mid = r1.astype(BF16)
    lo = (r1 - mid.astype(F32)).astype(BF16)
    return _dot(tril, hi) + _dot(tril, mid) + _dot(tril, lo)


def _const_spec(shape):
    nd = len(shape)
    return pl.BlockSpec(shape, lambda *_: (0,) * nd, pipeline_mode=pl.Buffered(1))


W_BLOCK = 512


def _wprep_kernel(w_ref, out_ref):
    j = pl.program_id(0)
    row = lax.broadcasted_iota(jnp.int32, w_ref.shape, 0)
    keep = (j < C_TAIL // W_BLOCK) | (row < N_IF)
    out_ref[...] = jnp.where(keep, w_ref[...], 0.0).T.astype(BF16)


def _wprep(w_t):
    n_main = C_TAIL // W_BLOCK

    def src(j):
        shifted = j * W_BLOCK + jnp.where(j >= C_IF // W_BLOCK, N_IF, 0)
        return pl.multiple_of(jnp.where(j < n_main, shifted, C_IF), SUBLANES), 0

    return pl.pallas_call(
        _wprep_kernel,
        grid=(n_main + 1,),
        in_specs=[pl.BlockSpec((pl.Element(W_BLOCK), pl.Element(D_MODEL)), src)],
        out_specs=pl.BlockSpec((D_MODEL, W_BLOCK), lambda j: (0, j)),
        out_shape=jax.ShapeDtypeStruct((D_MODEL, C_TAIL + W_BLOCK), BF16),
        compiler_params=pltpu.CompilerParams(dimension_semantics=("arbitrary",), vmem_limit_bytes=VMEM_LIMIT),
        name="wprep",
    )(w_t)


def _inproj_kernel(x_ref, g_ref, w_ref, cw_ref, cb_ref, bif_ref,
                   mq_ref, mk_ref, mv_ref, mo_ref, dq_ref, dk_ref, dv_ref, gates_ref, ifg_ref,
                   carry_ref, slab_ref, *, tiles_per_seq):
    tm = x_ref.shape[0]
    i = pl.program_id(0)
    hb = _rms(x_ref[...], g_ref[...]).astype(BF16)

    @pl.when(i % tiles_per_seq == 0)
    def _():
        carry_ref[...] = jnp.zeros(carry_ref.shape, F32)

    for col0 in (0, M_WIDTH):
        _slab_store(_dot(hb, w_ref[:, col0:col0 + M_WIDTH]), slab_ref, col0 // LANES)

    def conv_silu(col0):
        y = _slab_conv(tm, M_WIDTH, col0, slab_ref, col0 // LANES, carry_ref, cw_ref, cb_ref)
        return y * _sigmoid(y)

    c = 2 * M_WIDTH
    mv_ref[...] = _dot(hb, w_ref[:, c:c + M_WIDTH]).astype(BF16).T; c += M_WIDTH
    mo_ref[...] = _dot(hb, w_ref[:, c:c + M_WIDTH]).astype(BF16); c += M_WIDTH
    dq_ref[...] = (_dot(hb, w_ref[:, c:c + DA_WIDTH]) * (LOG2E * DA_HEAD_DIM ** -0.5)).astype(BF16); c += DA_WIDTH
    dk_ref[...] = _dot(hb, w_ref[:, c:c + DA_WIDTH]).astype(BF16); c += DA_WIDTH
    dv_ref[...] = _dot(hb, w_ref[:, c:c + DA_WIDTH]).astype(BF16).T; c += DA_WIDTH
    for j in range(4):
        gates_ref[:, j * 512:(j + 1) * 512] = _dot(hb, w_ref[:, c:c + 512]).astype(BF16); c += 512

    ifg_ref[...] = _dot(hb, w_ref[:, C_TAIL:C_TAIL + LANES]) + bif_ref[...]

    mq_ref[...] = conv_silu(0).astype(BF16)
    mk_ref[...] = (conv_silu(M_WIDTH) * (M_HEAD_DIM ** -0.5)).astype(BF16)


def _inproj(x2, g, w_all, cw, cb, bif, seq):
    T = x2.shape[0]
    tm = TOKEN_TILE
    tps = seq // tm
    row = lambda i: (i, 0)
    bf = lambda n: jax.ShapeDtypeStruct((T, n), BF16)
    bf_t = lambda n: jax.ShapeDtypeStruct((T // seq, n, seq), BF16)
    blk = pl.BlockSpec((tm, M_WIDTH), row)
    blk_t = pl.BlockSpec((None, M_WIDTH, tm), lambda i: (i // tps, 0, i % tps))
    out_shapes = [bf(M_WIDTH), bf(M_WIDTH), bf_t(M_WIDTH), bf(M_WIDTH), bf(DA_WIDTH), bf(DA_WIDTH), bf_t(DA_WIDTH),
                  bf(2 * D_MODEL), jax.ShapeDtypeStruct((T, LANES), F32)]
    out_specs = [blk, blk, blk_t, blk, blk, blk, blk_t,
                 pl.BlockSpec((tm, 2 * D_MODEL), row), pl.BlockSpec((tm, LANES), row)]
    return pl.pallas_call(
        functools.partial(_inproj_kernel, tiles_per_seq=tps),
        grid=(T // tm,),
        in_specs=[pl.BlockSpec((tm, D_MODEL), row), _const_spec((1, D_MODEL)), _const_spec(w_all.shape),
                  _const_spec((M_CONV, 2 * M_WIDTH)), _const_spec((1, 2 * M_WIDTH)), _const_spec((1, LANES))],
        out_specs=out_specs,
        out_shape=out_shapes,
        scratch_shapes=[pltpu.VMEM((SUBLANES, 2 * M_WIDTH), F32),
                        pltpu.VMEM((2 * M_WIDTH // LANES, SUBLANES + tm, LANES), F32)],
        compiler_params=pltpu.CompilerParams(dimension_semantics=("arbitrary",), vmem_limit_bytes=VMEM_LIMIT),
        name="inproj",
    )(x2, g, w_all, cw, cb, bif)


def _mlstm_kernel(mq_ref, mk_ref, mv_ref, mo_ref, ifg_ref, g_ref, out_ref, ct_ref, m_ref):
    L = mq_ref.shape[1]
    d = M_HEAD_DIM

    @pl.when(pl.program_id(1) == 0)
    def _():
        ct_ref[...] = jnp.zeros(ct_ref.shape, F32)
        m_ref[...] = jnp.zeros(m_ref.shape, F32)

    kidx = lax.broadcasted_iota(jnp.int32, (L, L), 0)
    qidx = lax.broadcasted_iota(jnp.int32, (L, L), 1)
    tril = jnp.where(kidx >= qidx, 1.0, 0.0).astype(BF16)
    causal = kidx <= qidx

    for s in range(mq_ref.shape[0]):
        ig = ifg_ref[s]
        fg = pltpu.roll(ig, LANES - M_HEADS, axis=1)
        logf = jnp.minimum(fg, 0.0) - jnp.log1p(jnp.exp(-jnp.abs(fg)))
        b = _cumsum_rows(tril, logf)
        b_tot = b[L - 1:L, :]
        u = ig - b
        w_end = b_tot + u
        m_loc = jnp.max(w_end, axis=0, keepdims=True)
        a = jnp.exp(w_end - m_loc)
        m_prev = m_ref[s]
        m_new = jnp.maximum(b_tot + m_prev, m_loc)
        s_old = jnp.exp(b_tot + m_prev - m_new)
        s_loc = jnp.exp(m_loc - m_new)
        li_t = (b + m_prev).T
        b_t = b.T
        m_ref[s] = m_new

        for h in range(M_HEADS):
            hs = slice(h * d, (h + 1) * d)
            q = mq_ref[s, :, hs]
            k = mk_ref[s, :, hs]
            v_t = mv_ref[s, hs, :]
            ct_prev = ct_ref[s, h]
            s_t = _dot_nt(k, q)
            inter = _dot_nt(ct_prev.astype(BF16), q)
            log_d = jnp.where(causal, u[:, h:h + 1] + b_t[h:h + 1, :], -jnp.inf)
            li = li_t[h:h + 1, :]
            m_t = jnp.maximum(li, jnp.max(log_d, axis=0, keepdims=True))
            p_t = s_t * jnp.exp(log_d - m_t)
            s_inter = jnp.exp(li - m_t)
            num = s_inter * inter[:d] + _dot(v_t, p_t.astype(BF16))
            den = s_inter * inter[d:d + 1] + jnp.sum(p_t, axis=0, keepdims=True)
            hh = num / jnp.maximum(jnp.abs(den), jnp.exp(-m_t))
            y = (hh * lax.rsqrt(jnp.mean(hh * hh, axis=0, keepdims=True) + EPS)).T
            out_ref[s, :, hs] = (_sigmoid(mo_ref[s, :, hs].astype(F32)) * (y * g_ref[:, hs])).astype(BF16)

            ak = a[:, h:h + 1] * k.astype(F32)
            so = s_old[:, h:h + 1]
            sl = s_loc[:, h:h + 1]
            ct_ref[s, h, 0:d, :] = so * ct_prev[:d] + sl * _dot(v_t, ak.astype(BF16))
            ct_ref[s, h, d:d + 1, :] = so * ct_prev[d:d + 1] + sl * jnp.sum(ak, axis=0, keepdims=True)


def _mlstm(mq, mk, mv, mo, ifg, g, batch, seq):
    L = M_CHUNK
    nb = M_BATCH
    seq3 = lambda t: t.reshape(batch, seq, t.shape[-1])
    blk = pl.BlockSpec((nb, L, M_WIDTH), lambda b, c: (b, c, 0))
    blk_t = pl.BlockSpec((nb, M_WIDTH, L), lambda b, c: (b, 0, c))
    out = pl.pallas_call(
        _mlstm_kernel,
        grid=(batch // nb, seq // L),
        in_specs=[blk, blk, blk_t, blk, pl.BlockSpec((nb, L, LANES), lambda b, c: (b, c, 0)),
                  _const_spec((1, M_WIDTH))],
        out_specs=blk,
        out_shape=jax.ShapeDtypeStruct((batch, seq, M_WIDTH), BF16),
        scratch_shapes=[pltpu.VMEM((nb, M_HEADS, M_HEAD_DIM + 2 * SUBLANES, M_HEAD_DIM), F32),
                        pltpu.VMEM((nb, 1, LANES), F32)],
        compiler_params=pltpu.CompilerParams(dimension_semantics=("arbitrary", "arbitrary"),
                                             vmem_limit_bytes=VMEM_LIMIT),
        name="mlstm",
    )(seq3(mq), seq3(mk), mv, seq3(mo), seq3(ifg), g)
    return out.reshape(batch * seq, M_WIDTH)


def _t5_bucket(n):
    max_exact = REL_BUCKETS // 2
    nf = jnp.maximum(n, 1).astype(F32)
    large = max_exact + (jnp.log(nf / max_exact) / math.log(REL_MAX_DIST / max_exact)
                         * (REL_BUCKETS - max_exact)).astype(jnp.int32)
    large = jnp.minimum(large, REL_BUCKETS - 1)
    return jnp.where(n < max_exact, n, large)


def _attn_kernel(rb_ref, lam_ref, q_ref, k_ref, v_ref, g_ref, out_ref,
                 bias_ref, m_ref, l_ref, acc_ref, s_ref, *, lam_init):
    tq = ATT_BLOCK
    tk = tq
    sb = ATT_SUB
    nq = q_ref.shape[0] // tq
    h = pl.program_id(1)

    kk = lax.broadcasted_iota(jnp.int32, (sb, sb), 0)
    qq = lax.broadcasted_iota(jnp.int32, (sb, sb), 1)

    @pl.when((pl.program_id(0) == 0) & (h == 0))
    def _():
        for delta in range(2):
            bucket = _t5_bucket(jnp.maximum(qq + delta * sb - kk, 0))
            for hh in range(DA_HEADS):
                t = jnp.zeros((sb, sb), F32)
                for j in range(REL_BUCKETS):
                    t = jnp.where(bucket == j, rb_ref[j, hh], t)
                bias_ref[hh, delta] = (t - rb_ref[REL_BUCKETS - 1, hh]) * LOG2E

    lv = lam_ref[...]
    lam = (jnp.exp(jnp.sum(lv[0:1] * lv[1:2], axis=1, keepdims=True))
           - jnp.exp(jnp.sum(lv[2:3] * lv[3:4], axis=1, keepdims=True)) + lam_init)

    n_groups = 2 * tq // sb
    d0 = bias_ref[h, 0]
    d1 = bias_ref[h, 1]
    causal = qq >= kk
    ident = lambda s: s
    near = lambda s: jnp.concatenate([s[:sb], s[sb:] + d1], axis=0)
    early = lambda s: jnp.where(causal, s + d0, NEG_BIG)
    late = lambda s: jnp.concatenate([s[:sb] + d1, jnp.where(causal, s[sb:] + d0, NEG_BIG)], axis=0)

    def plan(j, n):
        if j == n - 1:
            return [(sb, early), (tk, late)] * (n_groups // 2)
        if j == n - 2:
            return [(tk, near), (tk, ident)] * (n_groups // 2)
        return [(tk, ident)] * n_groups

    lane = lax.broadcasted_iota(jnp.int32, (tq, DA_V_DIM), 1)
    step = 0
    for qb in range(nq):
        n = qb + 1
        st = qb % 2
        q = q_ref[qb * tq:(qb + 1) * tq, :]
        zero = jnp.zeros_like(q)
        qcat = jnp.concatenate([jnp.where(lane < DA_HEAD_DIM, q, zero),
                                jnp.where(lane >= DA_HEAD_DIM, q, zero)], axis=0)
        m_ref[st] = jnp.full(m_ref.shape[1:], -jnp.inf, F32)
        l_ref[st] = jnp.zeros(l_ref.shape[1:], F32)
        acc_ref[st] = jnp.zeros(acc_ref.shape[1:], F32)

        def scores(j, buf):
            k = k_ref[j * tk:(j + 1) * tk, :]
            for g, (rows, _) in enumerate(plan(j, n)):
                s_ref[buf, g, 0:rows, :] = _dot_nt(k[:rows], qcat[g * sb:(g + 1) * sb])

        def softmax_pv(j, buf):
            v_t = v_ref[:, j * tk:(j + 1) * tk]
            for g, (rows, fix) in enumerate(plan(j, n)):
                cols = slice(g * sb, (g + 1) * sb)
                s = fix(s_ref[buf, g, 0:rows, :])
                m_prev = m_ref[st, :, cols]
                m_new = jnp.maximum(m_prev, jnp.max(s, axis=0, keepdims=True))
                alpha = jnp.exp2(m_prev - m_new)
                p = jnp.exp2(s - m_new)
                l_ref[st, :, cols] = alpha * l_ref[st, :, cols] + jnp.sum(p, axis=0, keepdims=True)
                acc_ref[st, :, cols] = (alpha * acc_ref[st, :, cols]
                                        + _dot(v_t[:, :rows], p.astype(BF16)))
                m_ref[st, :, cols] = m_new

        scores(0, step % 2)
        for j in range(n):
            if j + 1 < n:
                scores(j + 1, (step + 1) % 2)
            softmax_pv(j, step % 2)
            step += 1

        o = acc_ref[st] * (1.0 / l_ref[st])
        out = (o[:, :tq] - lam * o[:, tq:]).T
        out_ref[qb * tq:(qb + 1) * tq, :] = (_rms(out, g_ref[...]) * (1.0 - lam_init)).astype(BF16)


def _attention(rel_bias, lam_vec, dq, dk, dv, g, batch, seq, lam_init):
    T = dq.shape[0]
    tq = ATT_BLOCK
    seq_blk = pl.BlockSpec((seq, DA_V_DIM), lambda b, h: (b, h))
    return pl.pallas_call(
        functools.partial(_attn_kernel, lam_init=lam_init),
        grid=(batch, DA_HEADS),
        in_specs=[pl.BlockSpec(memory_space=pltpu.SMEM),
                  pl.BlockSpec((4, DA_HEAD_DIM), lambda b, h: (0, 0)),
                  seq_blk, seq_blk,
                  pl.BlockSpec((None, DA_V_DIM, seq), lambda b, h: (b, h, 0)),
                  pl.BlockSpec((1, DA_V_DIM), lambda b, h: (0, h))],
        out_specs=seq_blk,
        out_shape=jax.ShapeDtypeStruct((T, DA_WIDTH), BF16),
        scratch_shapes=[pltpu.VMEM((DA_HEADS, 2, ATT_SUB, ATT_SUB), F32),
                        pltpu.VMEM((2, 1, 2 * tq), F32),
                        pltpu.VMEM((2, 1, 2 * tq), F32),
                        pltpu.VMEM((2, DA_V_DIM, 2 * tq), F32),
                        pltpu.VMEM((2, 2 * tq // ATT_SUB, tq, ATT_SUB), F32)],
        compiler_params=pltpu.CompilerParams(dimension_semantics=("arbitrary",) * 2,
                                             vmem_limit_bytes=VMEM_LIMIT),
        name="diffattn",
    )(rel_bias, lam_vec, dq, dk, dv, g)


def _gelu_tanh(x):
    return 0.5 * x * (1.0 + jnp.tanh(math.sqrt(2.0 / math.pi) * (x + 0.044715 * (x * x * x))))


def _ffn_kernel(gm_ref, hd_ref, gates_ref, x_ref, p_ref, wm_ref, wd_ref, wo_ref,
                gf_ref, wup_ref, cw_ref, cb_ref, wdn_ref, gp_ref, wpg_ref, wp_ref, gl_ref,
                out_ref, carry_ref, slab_ref, acc_ref, *, tiles_per_seq, final_norm):
    tm = x_ref.shape[0]
    i = pl.program_id(0)

    ya = _dot(gm_ref[...], wm_ref[...])
    yb = _dot(hd_ref[...], wd_ref[...])
    ga = _sigmoid(gates_ref[:, :D_MODEL].astype(F32))
    gb = _sigmoid(gates_ref[:, D_MODEL:].astype(F32))
    x1 = x_ref[...] + _dot((ga * ya + gb * yb).astype(BF16), wo_ref[...])

    hb = _rms(x1, gf_ref[...]).astype(BF16)

    @pl.when(i % tiles_per_seq == 0)
    def _():
        carry_ref[...] = jnp.zeros(carry_ref.shape, F32)

    def slab0(c, half):
        return (c % FF_SLAB_SETS) * FF_SLABS + half * (FF_CHUNK // LANES)

    def up(c):
        for half, off in enumerate((0, D_FF)):
            lo = off + c * FF_CHUNK
            _slab_store(_dot(hb, wup_ref[:, lo:lo + FF_CHUNK]), slab_ref, slab0(c, half))

    def conv(c, half):
        return _slab_conv(tm, FF_CHUNK, half * D_FF + c * FF_CHUNK, slab_ref, slab0(c, half),
                          carry_ref, cw_ref, cb_ref)

    for c in range(FF_LOOKAHEAD):
        up(c)
    for c in range(N_FF_CHUNKS):
        if c + FF_LOOKAHEAD < N_FF_CHUNKS:
            up(c + FF_LOOKAHEAD)
        act = (_gelu_tanh(conv(c, 1)) * conv(c, 0)).astype(BF16)
        d = _dot(act, wdn_ref[c * FF_CHUNK:(c + 1) * FF_CHUNK, :])
        if c == 0:
            acc_ref[...] = d
        else:
            acc_ref[...] += d

    x2 = x1 + acc_ref[...]
    hg = _rms(x2, gp_ref[...]).astype(BF16)
    gate = _sigmoid(_dot(hg, wpg_ref[...]))
    pe = _dot(p_ref[...].astype(BF16), wp_ref[...])
    x3 = x2 + gate * pe
    out_ref[...] = _rms(x3, gl_ref[...]) if final_norm else x3


def _ffn(gm, hd, gates, x2, p2, wm, wd, wo, gf, wup, cw, cb, wdn, gp, wpg, wp, gl, seq, final_norm):
    T = x2.shape[0]
    tm = TOKEN_TILE
    row = lambda i: (i, 0)
    return pl.pallas_call(
        functools.partial(_ffn_kernel, tiles_per_seq=seq // tm, final_norm=final_norm),
        grid=(T // tm,),
        in_specs=[pl.BlockSpec((tm, M_WIDTH), row), pl.BlockSpec((tm, DA_WIDTH), row),
                  pl.BlockSpec((tm, 2 * D_MODEL), row), pl.BlockSpec((tm, D_MODEL), row),
                  pl.BlockSpec((tm, PLE_DIM), row),
                  _const_spec((M_WIDTH, D_MODEL)), _const_spec((DA_WIDTH, D_MODEL)),
                  _const_spec((D_MODEL, D_MODEL)), _const_spec((1, D_MODEL)),
                  _const_spec((D_MODEL, 2 * D_FF)), _const_spec((FFN_CONV, 2 * D_FF)),
                  _const_spec((1, 2 * D_FF)), _const_spec((D_FF, D_MODEL)),
                  _const_spec((1, D_MODEL)), _const_spec((D_MODEL, D_MODEL)),
                  _const_spec((PLE_DIM, D_MODEL)), _const_spec((1, D_MODEL))],
        out_specs=pl.BlockSpec((tm, D_MODEL), row),
        out_shape=jax.ShapeDtypeStruct((T, D_MODEL), F32),
        scratch_shapes=[pltpu.VMEM((SUBLANES, 2 * D_FF), F32),
                        pltpu.VMEM((FF_SLAB_SETS * FF_SLABS, SUBLANES + tm, LANES), F32),
                        pltpu.VMEM((tm, D_MODEL), F32)],
        compiler_params=pltpu.CompilerParams(dimension_semantics=("arbitrary",), vmem_limit_bytes=VMEM_LIMIT),
        name="convffn",
    )(gm, hd, gates, x2, p2, wm, wd, wo, gf, wup, cw, cb, wdn, gp, wpg, wp, gl)


def kernel(x, p, rel_bias, norm_mix_g, w_in, b_if, m_conv_w, m_conv_b, m_norm_g, da_lambda, da_norm_g,
           w_br_m, w_br_d, w_out, norm_ffn_g, w_up, ffn_conv_w, ffn_conv_b, w_down, norm_ple_g, w_ple_gate,
           w_ple, norm_final_g):
    batch, seq, _ = x.shape
    depth = w_in.shape[0]
    T = batch * seq
    assert seq % TOKEN_TILE == 0 and seq % M_CHUNK == 0 and seq % ATT_BLOCK == 0 and batch % M_BATCH == 0
    xt = x.reshape(T, D_MODEL)
    row = lambda v: v.reshape(1, -1).astype(F32)

    for l in range(depth):
        w_all = _wprep(jnp.swapaxes(w_in[l].astype(F32), 0, 1))
        bif = jnp.concatenate([b_if[l].astype(F32), jnp.zeros((LANES - N_IF,), F32)]).reshape(1, LANES)

        mq, mk, mv, mo, dq, dk, dv, gates, ifg = _inproj(
            xt, row(norm_mix_g[l]), w_all, m_conv_w[l].astype(F32), row(m_conv_b[l]), bif, seq)

        gm = _mlstm(mq, mk, mv, mo, ifg, row(m_norm_g[l]), batch, seq)

        lam_init = 0.8 - 0.6 * math.exp(-0.3 * l)
        hd = _attention(rel_bias.astype(F32), da_lambda[l].astype(F32), dq, dk, dv, row(da_norm_g[l]),
                        batch, seq, lam_init)

        xt = _ffn(gm, hd, gates, xt, p[l].reshape(T, PLE_DIM),
                  w_br_m[l].astype(BF16), w_br_d[l].astype(BF16), w_out[l].astype(BF16),
                  row(norm_ffn_g[l]), w_up[l].astype(BF16),
                  ffn_conv_w[l].astype(F32), row(ffn_conv_b[l]), w_down[l].astype(BF16), row(norm_ple_g[l]),
                  w_ple_gate[l].astype(BF16), w_ple[l].astype(BF16), row(norm_final_g), seq,
                  final_norm=(l == depth - 1))

    return xt.reshape(batch, seq, D_MODEL)
```

```python
import functools
import math

import jax
import jax.numpy as jnp
from jax import lax
from jax.experimental import pallas as pl
from jax.experimental.pallas import tpu as pltpu

D_MODEL = 1024
PLE_DIM = 256
M_HEADS = 4
M_HEAD_DIM = 128
M_WIDTH = M_HEADS * M_HEAD_DIM
M_CONV = 4
DA_HEADS = 4
DA_HEAD_DIM = 64
DA_V_DIM = 2 * DA_HEAD_DIM
DA_WIDTH = DA_HEADS * DA_V_DIM
REL_BUCKETS = 32
REL_MAX_DIST = 128
D_FF = 2816
FFN_CONV = 3
EPS = 1e-6
NEG_BIG = -1e30
LOG2E = 1.4426950408889634

LANES = 128
SUBLANES = 8
VMEM_LIMIT = 56 * 1024 * 1024

TOKEN_TILE = 512
M_CHUNK = 256
M_BATCH = 4
ATT_BLOCK = 512
ATT_SUB = ATT_BLOCK // 2
FF_CHUNK = 256
N_FF_CHUNKS = D_FF // FF_CHUNK
FF_SLABS = 2 * FF_CHUNK // LANES
FF_LOOKAHEAD = 3
FF_SLAB_SETS = FF_LOOKAHEAD + 1
FF_MERGE_AT = 3

F32 = jnp.float32
BF16 = jnp.bfloat16

C_IF = 4 * M_WIDTH
N_IF = 2 * M_HEADS
IN_COLS = C_IF + N_IF + 3 * DA_WIDTH + 2 * D_MODEL
C_TAIL = IN_COLS - N_IF


def _dot(a, b):
    return jnp.dot(a, b, preferred_element_type=F32)


def _dot_nt(a, b):
    return lax.dot_general(a, b, (((1,), (1,)), ((), ())), preferred_element_type=F32)


def _rms(x, g):
    return x * lax.rsqrt(jnp.mean(x * x, -1, keepdims=True) + EPS) * g


def _sigmoid(x):
    return 1.0 / (1.0 + jnp.exp(-x))


def _slab_store(u, slab_ref, slab0):
    tm, n = u.shape
    for j in range(n // LANES):
        slab_ref[slab0 + j, SUBLANES:SUBLANES + tm, :] = u[:, j * LANES:(j + 1) * LANES]


def _slab_conv(tm, n, col0, slab_ref, slab0, carry_ref, cw_ref, cb_ref):
    taps = cw_ref.shape[0]
    outs = []
    for j in range(n // LANES):
        cols = slice(col0 + j * LANES, col0 + (j + 1) * LANES)
        s = slab0 + j
        slab_ref[s, 0:SUBLANES, :] = carry_ref[:, cols]
        carry_ref[:, cols] = slab_ref[s, tm:tm + SUBLANES, :]
        y = cb_ref[:, cols]
        for k in range(taps):
            y = y + cw_ref[taps - 1 - k:taps - k, cols] * slab_ref[s, pl.ds(SUBLANES - k, tm), :]
        outs.append(y)
    return jnp.concatenate(outs, axis=1)


def _cumsum_rows(tril, x):
    hi = x.astype(BF16)
    r1 = x - hi.astype(F32)
    mid = r1.astype(BF16)
    lo = (r1 - mid.astype(F32)).astype(BF16)
    return _dot(tril, hi) + _dot(tril, mid) + _dot(tril, lo)


def _const_spec(shape):
    nd = len(shape)
    return pl.BlockSpec(shape, lambda *_: (0,) * nd, pipeline_mode=pl.Buffered(1))


W_BLOCK = 512


def _wprep_kernel(w_ref, out_ref):
    j = pl.program_id(0)
    row = lax.broadcasted_iota(jnp.int32, w_ref.shape, 0)
    keep = (j < C_TAIL // W_BLOCK) | (row < N_IF)
    out_ref[...] = jnp.where(keep, w_ref[...], 0.0).T.astype(BF16)


def _wprep(w_t):
    n_main = C_TAIL // W_BLOCK

    def src(j):
        shifted = j * W_BLOCK + jnp.where(j >= C_IF // W_BLOCK, N_IF, 0)
        return pl.multiple_of(jnp.where(j < n_main, shifted, C_IF), SUBLANES), 0

    return pl.pallas_call(
        _wprep_kernel,
        grid=(n_main + 1,),
        in_specs=[pl.BlockSpec((pl.Element(W_BLOCK), pl.Element(D_MODEL)), src)],
        out_specs=pl.BlockSpec((D_MODEL, W_BLOCK), lambda j: (0, j)),
        out_shape=jax.ShapeDtypeStruct((D_MODEL, C_TAIL + W_BLOCK), BF16),
        compiler_params=pltpu.CompilerParams(dimension_semantics=("arbitrary",), vmem_limit_bytes=VMEM_LIMIT),
        name="wprep",
    )(w_t)


def _inproj_kernel(x_ref, g_ref, w_ref, cw_ref, cb_ref, bif_ref,
                   mq_ref, mk_ref, mv_ref, mo_ref, dq_ref, dk_ref, dv_ref, gates_ref, ifg_ref,
                   carry_ref, slab_ref, hb_ref, *, tiles_per_seq):
    tm = x_ref.shape[0]
    s = pl.program_id(0)
    slot = s % 2

    def normalise():
        hb_ref[slot] = _rms(x_ref[...], g_ref[...]).astype(BF16)

    pl.when(s == 0)(normalise)

    @pl.when((s > 0) & ((s - 1) % tiles_per_seq == 0))
    def _():
        carry_ref[...] = jnp.zeros(carry_ref.shape, F32)

    @pl.when(s > 0)
    def _():
        _inproj_tile(tm, hb_ref[1 - slot], w_ref, cw_ref, cb_ref, bif_ref, mq_ref, mk_ref, mv_ref, mo_ref,
                     dq_ref, dk_ref, dv_ref, gates_ref, ifg_ref, carry_ref, slab_ref)
        normalise()


def _inproj_tile(tm, hb, w_ref, cw_ref, cb_ref, bif_ref, mq_ref, mk_ref, mv_ref, mo_ref,
                 dq_ref, dk_ref, dv_ref, gates_ref, ifg_ref, carry_ref, slab_ref):
    for col0 in (0, M_WIDTH):
        _slab_store(_dot(hb, w_ref[:, col0:col0 + M_WIDTH]), slab_ref, col0 // LANES)

    def conv_silu(col0):
        y = _slab_conv(tm, M_WIDTH, col0, slab_ref, col0 // LANES, carry_ref, cw_ref, cb_ref)
        return y * _sigmoid(y)

    c = 2 * M_WIDTH
    mv_ref[...] = _dot(hb, w_ref[:, c:c + M_WIDTH]).astype(BF16).T; c += M_WIDTH
    mo_ref[...] = _dot(hb, w_ref[:, c:c + M_WIDTH]).astype(BF16); c += M_WIDTH
    dq_ref[...] = (_dot(hb, w_ref[:, c:c + DA_WIDTH]) * (LOG2E * DA_HEAD_DIM ** -0.5)).astype(BF16); c += DA_WIDTH
    dk_ref[...] = _dot(hb, w_ref[:, c:c + DA_WIDTH]).astype(BF16); c += DA_WIDTH
    dv_ref[...] = _dot(hb, w_ref[:, c:c + DA_WIDTH]).astype(BF16).T; c += DA_WIDTH
    for j in range(4):
        gates_ref[:, j * 512:(j + 1) * 512] = _dot(hb, w_ref[:, c:c + 512]).astype(BF16); c += 512

    ifg_ref[...] = _dot(hb, w_ref[:, C_TAIL:C_TAIL + LANES]) + bif_ref[...]

    mq_ref[...] = conv_silu(0).astype(BF16)
    mk_ref[...] = (conv_silu(M_WIDTH) * (M_HEAD_DIM ** -0.5)).astype(BF16)


def _inproj(x2, g, w_all, cw, cb, bif, seq):
    T = x2.shape[0]
    tm = TOKEN_TILE
    tps = seq // tm
    n_tiles = T // tm
    row = lambda s: (jnp.maximum(s - 1, 0), 0)
    bf = lambda n: jax.ShapeDtypeStruct((T, n), BF16)
    bf_t = lambda n: jax.ShapeDtypeStruct((T // seq, n, seq), BF16)
    blk = pl.BlockSpec((tm, M_WIDTH), row)
    blk_t = pl.BlockSpec((None, M_WIDTH, tm),
                         lambda s: (jnp.maximum(s - 1, 0) // tps, 0, jnp.maximum(s - 1, 0) % tps))
    out_shapes = [bf(M_WIDTH), bf(M_WIDTH), bf_t(M_WIDTH), bf(M_WIDTH), bf(DA_WIDTH), bf(DA_WIDTH), bf_t(DA_WIDTH),
                  bf(2 * D_MODEL), jax.ShapeDtypeStruct((T, LANES), F32)]
    out_specs = [blk, blk, blk_t, blk, blk, blk, blk_t,
                 pl.BlockSpec((tm, 2 * D_MODEL), row), pl.BlockSpec((tm, LANES), row)]
    return pl.pallas_call(
        functools.partial(_inproj_kernel, tiles_per_seq=tps),
        grid=(n_tiles + 1,),
        in_specs=[pl.BlockSpec((tm, D_MODEL), lambda s: (jnp.minimum(s, n_tiles - 1), 0)),
                  _const_spec((1, D_MODEL)), _const_spec(w_all.shape),
                  _const_spec((M_CONV, 2 * M_WIDTH)), _const_spec((1, 2 * M_WIDTH)), _const_spec((1, LANES))],
        out_specs=out_specs,
        out_shape=out_shapes,
        scratch_shapes=[pltpu.VMEM((SUBLANES, 2 * M_WIDTH), F32),
                        pltpu.VMEM((2 * M_WIDTH // LANES, SUBLANES + tm, LANES), F32),
                        pltpu.VMEM((2, tm, D_MODEL), BF16)],
        compiler_params=pltpu.CompilerParams(dimension_semantics=("arbitrary",), vmem_limit_bytes=VMEM_LIMIT),
        name="inproj",
    )(x2, g, w_all, cw, cb, bif)


def _mlstm_kernel(mq_ref, mk_ref, mv_ref, mo_ref, ifg_ref, g_ref, out_ref, ct_ref, m_ref):
    L = mq_ref.shape[1]
    d = M_HEAD_DIM

    @pl.when(pl.program_id(1) == 0)
    def _():
        ct_ref[...] = jnp.zeros(ct_ref.shape, F32)
        m_ref[...] = jnp.zeros(m_ref.shape, F32)

    kidx = lax.broadcasted_iota(jnp.int32, (L, L), 0)
    qidx = lax.broadcasted_iota(jnp.int32, (L, L), 1)
    tril = jnp.where(kidx >= qidx, 1.0, 0.0).astype(BF16)
    causal = kidx <= qidx

    for s in range(mq_ref.shape[0]):
        ig = ifg_ref[s]
        fg = pltpu.roll(ig, LANES - M_HEADS, axis=1)
        logf = jnp.minimum(fg, 0.0) - jnp.log1p(jnp.exp(-jnp.abs(fg)))
        b = _cumsum_rows(tril, logf)
        b_tot = b[L - 1:L, :]
        u = ig - b
        w_end = b_tot + u
        m_loc = jnp.max(w_end, axis=0, keepdims=True)
        a = jnp.exp(w_end - m_loc)
        m_prev = m_ref[s]
        m_new = jnp.maximum(b_tot + m_prev, m_loc)
        s_old = jnp.exp(b_tot + m_prev - m_new)
        s_loc = jnp.exp(m_loc - m_new)
        li_t = (b + m_prev).T
        b_t = b.T
        m_ref[s] = m_new

        for h in range(M_HEADS):
            hs = slice(h * d, (h + 1) * d)
            q = mq_ref[s, :, hs]
            k = mk_ref[s, :, hs]
            v_t = mv_ref[s, hs, :]
            ct_prev = ct_ref[s, h]
            s_t = _dot_nt(k, q)
            inter = _dot_nt(ct_prev.astype(BF16), q)
            log_d = jnp.where(causal, u[:, h:h + 1] + b_t[h:h + 1, :], -jnp.inf)
            li = li_t[h:h + 1, :]
            m_t = jnp.maximum(li, jnp.max(log_d, axis=0, keepdims=True))
            p_t = s_t * jnp.exp(log_d - m_t)
            s_inter = jnp.exp(li - m_t)
            num = s_inter * inter[:d] + _dot(v_t, p_t.astype(BF16))
            den = s_inter * inter[d:d + 1] + jnp.sum(p_t, axis=0, keepdims=True)
            hh = num / jnp.maximum(jnp.abs(den), jnp.exp(-m_t))
            y = (hh * lax.rsqrt(jnp.mean(hh * hh, axis=0, keepdims=True) + EPS)).T
            out_ref[s, :, hs] = (_sigmoid(mo_ref[s, :, hs].astype(F32)) * (y * g_ref[:, hs])).astype(BF16)

            ak = a[:, h:h + 1] * k.astype(F32)
            so = s_old[:, h:h + 1]
            sl = s_loc[:, h:h + 1]
            ct_ref[s, h, 0:d, :] = so * ct_prev[:d] + sl * _dot(v_t, ak.astype(BF16))
            ct_ref[s, h, d:d + 1, :] = so * ct_prev[d:d + 1] + sl * jnp.sum(ak, axis=0, keepdims=True)


def _mlstm(mq, mk, mv, mo, ifg, g, batch, seq):
    L = M_CHUNK
    nb = M_BATCH
    seq3 = lambda t: t.reshape(batch, seq, t.shape[-1])
    blk = pl.BlockSpec((nb, L, M_WIDTH), lambda b, c: (b, c, 0))
    blk_t = pl.BlockSpec((nb, M_WIDTH, L), lambda b, c: (b, 0, c))
    out = pl.pallas_call(
        _mlstm_kernel,
        grid=(batch // nb, seq // L),
        in_specs=[blk, blk, blk_t, blk, pl.BlockSpec((nb, L, LANES), lambda b, c: (b, c, 0)),
                  _const_spec((1, M_WIDTH))],
        out_specs=blk,
        out_shape=jax.ShapeDtypeStruct((batch, seq, M_WIDTH), BF16),
        scratch_shapes=[pltpu.VMEM((nb, M_HEADS, M_HEAD_DIM + 2 * SUBLANES, M_HEAD_DIM), F32),
                        pltpu.VMEM((nb, 1, LANES), F32)],
        compiler_params=pltpu.CompilerParams(dimension_semantics=("arbitrary", "arbitrary"),
                                             vmem_limit_bytes=VMEM_LIMIT),
        name="mlstm",
    )(seq3(mq), seq3(mk), mv, seq3(mo), seq3(ifg), g)
    return out.reshape(batch * seq, M_WIDTH)


def _t5_bucket(n):
    max_exact = REL_BUCKETS // 2
    nf = jnp.maximum(n, 1).astype(F32)
    large = max_exact + (jnp.log(nf / max_exact) / math.log(REL_MAX_DIST / max_exact)
                         * (REL_BUCKETS - max_exact)).astype(jnp.int32)
    large = jnp.minimum(large, REL_BUCKETS - 1)
    return jnp.where(n < max_exact, n, large)


def _attn_kernel(rb_ref, lam_ref, q_ref, k_ref, v_ref, g_ref, out_ref,
                 bias_ref, m_ref, l_ref, acc_ref, s_ref, *, lam_init):
    tq = ATT_BLOCK
    tk = tq
    sb = ATT_SUB
    nq = q_ref.shape[0] // tq
    h = pl.program_id(1)

    kk = lax.broadcasted_iota(jnp.int32, (sb, sb), 0)
    qq = lax.broadcasted_iota(jnp.int32, (sb, sb), 1)

    @pl.when((pl.program_id(0) == 0) & (h == 0))
    def _():
        for delta in range(2):
            bucket = _t5_bucket(jnp.maximum(qq + delta * sb - kk, 0))
            for hh in range(DA_HEADS):
                t = jnp.zeros((sb, sb), F32)
                for j in range(REL_BUCKETS):
                    t = jnp.where(bucket == j, rb_ref[j, hh], t)
                bias_ref[hh, delta] = (t - rb_ref[REL_BUCKETS - 1, hh]) * LOG2E

    lv = lam_ref[...]
    lam = (jnp.exp(jnp.sum(lv[0:1] * lv[1:2], axis=1, keepdims=True))
           - jnp.exp(jnp.sum(lv[2:3] * lv[3:4], axis=1, keepdims=True)) + lam_init)

    n_groups = 2 * tq // sb
    d0 = bias_ref[h, 0]
    d1 = bias_ref[h, 1]
    causal = qq >= kk
    ident = lambda s: s
    near = lambda s: jnp.concatenate([s[:sb], s[sb:] + d1], axis=0)
    early = lambda s: jnp.where(causal, s + d0, NEG_BIG)
    late = lambda s: jnp.concatenate([s[:sb] + d1, jnp.where(causal, s[sb:] + d0, NEG_BIG)], axis=0)

    def plan(j, n):
        if j == n - 1:
            return [(sb, early), (tk, late)] * (n_groups // 2)
        if j == n - 2:
            return [(tk, near), (tk, ident)] * (n_groups // 2)
        return [(tk, ident)] * n_groups

    lane = lax.broadcasted_iota(jnp.int32, (tq, DA_V_DIM), 1)
    step = 0
    for qb in range(nq):
        n = qb + 1
        st = qb % 2
        q = q_ref[qb * tq:(qb + 1) * tq, :]
        zero = jnp.zeros_like(q)
        qcat = jnp.concatenate([jnp.where(lane < DA_HEAD_DIM, q, zero),
                                jnp.where(lane >= DA_HEAD_DIM, q, zero)], axis=0)
        m_ref[st] = jnp.full(m_ref.shape[1:], -jnp.inf, F32)
        l_ref[st] = jnp.zeros(l_ref.shape[1:], F32)
        acc_ref[st] = jnp.zeros(acc_ref.shape[1:], F32)

        def scores(j, buf):
            k = k_ref[j * tk:(j + 1) * tk, :]
            for g, (rows, _) in enumerate(plan(j, n)):
                s_ref[buf, g, 0:rows, :] = _dot_nt(k[:rows], qcat[g * sb:(g + 1) * sb])

        def softmax_pv(j, buf):
            v_t = v_ref[:, j * tk:(j + 1) * tk]
            for g, (rows, fix) in enumerate(plan(j, n)):
                cols = slice(g * sb, (g + 1) * sb)
                s = fix(s_ref[buf, g, 0:rows, :])
                m_prev = m_ref[st, :, cols]
                m_new = jnp.maximum(m_prev, jnp.max(s, axis=0, keepdims=True))
                alpha = jnp.exp2(m_prev - m_new)
                p = jnp.exp2(s - m_new)
                l_ref[st, :, cols] = alpha * l_ref[st, :, cols] + jnp.sum(p, axis=0, keepdims=True)
                acc_ref[st, :, cols] = (alpha * acc_ref[st, :, cols]
                                        + _dot(v_t[:, :rows], p.astype(BF16)))
                m_ref[st, :, cols] = m_new

        scores(0, step % 2)
        for j in range(n):
            if j + 1 < n:
                scores(j + 1, (step + 1) % 2)
            softmax_pv(j, step % 2)
            step += 1

        o = acc_ref[st] * (1.0 / l_ref[st])
        out = (o[:, :tq] - lam * o[:, tq:]).T
        out_ref[qb * tq:(qb + 1) * tq, :] = (_rms(out, g_ref[...]) * (1.0 - lam_init)).astype(BF16)


def _attention(rel_bias, lam_vec, dq, dk, dv, g, batch, seq, lam_init):
    T = dq.shape[0]
    tq = ATT_BLOCK
    seq_blk = pl.BlockSpec((seq, DA_V_DIM), lambda b, h: (b, h))
    return pl.pallas_call(
        functools.partial(_attn_kernel, lam_init=lam_init),
        grid=(batch, DA_HEADS),
        in_specs=[pl.BlockSpec(memory_space=pltpu.SMEM),
                  pl.BlockSpec((4, DA_HEAD_DIM), lambda b, h: (0, 0)),
                  seq_blk, seq_blk,
                  pl.BlockSpec((None, DA_V_DIM, seq), lambda b, h: (b, h, 0)),
                  pl.BlockSpec((1, DA_V_DIM), lambda b, h: (0, h))],
        out_specs=seq_blk,
        out_shape=jax.ShapeDtypeStruct((T, DA_WIDTH), BF16),
        scratch_shapes=[pltpu.VMEM((DA_HEADS, 2, ATT_SUB, ATT_SUB), F32),
                        pltpu.VMEM((2, 1, 2 * tq), F32),
                        pltpu.VMEM((2, 1, 2 * tq), F32),
                        pltpu.VMEM((2, DA_V_DIM, 2 * tq), F32),
                        pltpu.VMEM((2, 2 * tq // ATT_SUB, tq, ATT_SUB), F32)],
        compiler_params=pltpu.CompilerParams(dimension_semantics=("arbitrary",) * 2,
                                             vmem_limit_bytes=VMEM_LIMIT),
        name="diffattn",
    )(rel_bias, lam_vec, dq, dk, dv, g)


def _gelu_tanh(x):
    return 0.5 * x * (1.0 + jnp.tanh(math.sqrt(2.0 / math.pi) * (x + 0.044715 * (x * x * x))))


def _ffn_kernel(gm_ref, hd_ref, gates_ref, x_ref, p_ref, wm_ref, wd_ref, wo_ref,
                gf_ref, wup_ref, cw_ref, cb_ref, wdn_ref, gp_ref, wpg_ref, wp_ref, gl_ref,
                out_ref, carry_ref, slab_ref, acc_ref, x1_ref, hb_ref, *, tiles_per_seq, final_norm):
    tm = x_ref.shape[0]
    s = pl.program_id(0)
    slot = s % 2

    def branches():
        ya = _dot(gm_ref[...], wm_ref[...])
        yb = _dot(hd_ref[...], wd_ref[...])
        ga = _sigmoid(gates_ref[:, :D_MODEL].astype(F32))
        gb = _sigmoid(gates_ref[:, D_MODEL:].astype(F32))
        return (ga * ya + gb * yb).astype(BF16)

    def project(mixed):
        x1 = x_ref[...] + _dot(mixed, wo_ref[...])
        x1_ref[slot] = x1
        hb_ref[slot] = _rms(x1, gf_ref[...]).astype(BF16)

    @pl.when(s == 0)
    def _():
        project(branches())

    @pl.when((s > 0) & ((s - 1) % tiles_per_seq == 0))
    def _():
        carry_ref[...] = jnp.zeros(carry_ref.shape, F32)

    @pl.when(s > 0)
    def _():
        _ffn_tile(tm, x1_ref.at[1 - slot], hb_ref.at[1 - slot], p_ref, wup_ref, cw_ref, cb_ref, wdn_ref, gp_ref,
                  wpg_ref, wp_ref, gl_ref, out_ref, carry_ref, slab_ref, acc_ref, branches, project, final_norm)


def _ffn_tile(tm, x1_ref, hb_ref, p_ref, wup_ref, cw_ref, cb_ref, wdn_ref, gp_ref, wpg_ref, wp_ref, gl_ref,
              out_ref, carry_ref, slab_ref, acc_ref, next_branches, next_project, final_norm):
    hb = hb_ref[...]
    mixed = next_branches()

    def slab0(c, half):
        return (c % FF_SLAB_SETS) * FF_SLABS + half * (FF_CHUNK // LANES)

    def up(c):
        for half, off in enumerate((0, D_FF)):
            lo = off + c * FF_CHUNK
            _slab_store(_dot(hb, wup_ref[:, lo:lo + FF_CHUNK]), slab_ref, slab0(c, half))

    def conv(c, half):
        return _slab_conv(tm, FF_CHUNK, half * D_FF + c * FF_CHUNK, slab_ref, slab0(c, half),
                          carry_ref, cw_ref, cb_ref)

    for c in range(FF_LOOKAHEAD):
        up(c)
    for c in range(N_FF_CHUNKS):
        if c + FF_LOOKAHEAD < N_FF_CHUNKS:
            up(c + FF_LOOKAHEAD)
        if c == FF_MERGE_AT:
            next_project(mixed)
        act = (_gelu_tanh(conv(c, 1)) * conv(c, 0)).astype(BF16)
        d = _dot(act, wdn_ref[c * FF_CHUNK:(c + 1) * FF_CHUNK, :])
        if c == 0:
            acc_ref[...] = d
        else:
            acc_ref[...] += d

    x2 = x1_ref[...] + acc_ref[...]
    hg = _rms(x2, gp_ref[...]).astype(BF16)
    gate = _sigmoid(_dot(hg, wpg_ref[...]))
    pe = _dot(p_ref[...].astype(BF16), wp_ref[...])
    x3 = x2 + gate * pe
    out_ref[...] = _rms(x3, gl_ref[...]) if final_norm else x3


def _ffn(gm, hd, gates, x2, p2, wm, wd, wo, gf, wup, cw, cb, wdn, gp, wpg, wp, gl, seq, final_norm):
    T = x2.shape[0]
    tm = TOKEN_TILE
    n_tiles = T // tm
    nxt = lambda s: (jnp.minimum(s, n_tiles - 1), 0)
    row = lambda s: (jnp.maximum(s - 1, 0), 0)
    return pl.pallas_call(
        functools.partial(_ffn_kernel, tiles_per_seq=seq // tm, final_norm=final_norm),
        grid=(n_tiles + 1,),
        in_specs=[pl.BlockSpec((tm, M_WIDTH), nxt), pl.BlockSpec((tm, DA_WIDTH), nxt),
                  pl.BlockSpec((tm, 2 * D_MODEL), nxt), pl.BlockSpec((tm, D_MODEL), nxt),
                  pl.BlockSpec((tm, PLE_DIM), row),
                  _const_spec((M_WIDTH, D_MODEL)), _const_spec((DA_WIDTH, D_MODEL)),
                  _const_spec((D_MODEL, D_MODEL)), _const_spec((1, D_MODEL)),
                  _const_spec((D_MODEL, 2 * D_FF)), _const_spec((FFN_CONV, 2 * D_FF)),
                  _const_spec((1, 2 * D_FF)), _const_spec((D_FF, D_MODEL)),
                  _const_spec((1, D_MODEL)), _const_spec((D_MODEL, D_MODEL)),
                  _const_spec((PLE_DIM, D_MODEL)), _const_spec((1, D_MODEL))],
        out_specs=pl.BlockSpec((tm, D_MODEL), row),
        out_shape=jax.ShapeDtypeStruct((T, D_MODEL), F32),
        scratch_shapes=[pltpu.VMEM((SUBLANES, 2 * D_FF), F32),
                        pltpu.VMEM((FF_SLAB_SETS * FF_SLABS, SUBLANES + tm, LANES), F32),
                        pltpu.VMEM((tm, D_MODEL), F32),
                        pltpu.VMEM((2, tm, D_MODEL), F32),
                        pltpu.VMEM((2, tm, D_MODEL), BF16)],
        compiler_params=pltpu.CompilerParams(dimension_semantics=("arbitrary",), vmem_limit_bytes=VMEM_LIMIT),
        name="convffn",
    )(gm, hd, gates, x2, p2, wm, wd, wo, gf, wup, cw, cb, wdn, gp, wpg, wp, gl)


def kernel(x, p, rel_bias, norm_mix_g, w_in, b_if, m_conv_w, m_conv_b, m_norm_g, da_lambda, da_norm_g,
           w_br_m, w_br_d, w_out, norm_ffn_g, w_up, ffn_conv_w, ffn_conv_b, w_down, norm_ple_g, w_ple_gate,
           w_ple, norm_final_g):
    batch, seq, _ = x.shape
    depth = w_in.shape[0]
    T = batch * seq
    assert seq % TOKEN_TILE == 0 and seq % M_CHUNK == 0 and seq % ATT_BLOCK == 0 and batch % M_BATCH == 0
    xt = x.reshape(T, D_MODEL)
    row = lambda v: v.reshape(1, -1).astype(F32)

    for l in range(depth):
        w_all = _wprep(jnp.swapaxes(w_in[l].astype(F32), 0, 1))
        bif = jnp.concatenate([b_if[l].astype(F32), jnp.zeros((LANES - N_IF,), F32)]).reshape(1, LANES)

        mq, mk, mv, mo, dq, dk, dv, gates, ifg = _inproj(
            xt, row(norm_mix_g[l]), w_all, m_conv_w[l].astype(F32), row(m_conv_b[l]), bif, seq)

        gm = _mlstm(mq, mk, mv, mo, ifg, row(m_norm_g[l]), batch, seq)

        lam_init = 0.8 - 0.6 * math.exp(-0.3 * l)
        hd = _attention(rel_bias.astype(F32), da_lambda[l].astype(F32), dq, dk, dv, row(da_norm_g[l]),
                        batch, seq, lam_init)

        xt = _ffn(gm, hd, gates, xt, p[l].reshape(T, PLE_DIM),
                  w_br_m[l].astype(BF16), w_br_d[l].astype(BF16), w_out[l].astype(BF16),
                  row(norm_ffn_g[l]), w_up[l].astype(BF16),
                  ffn_conv_w[l].astype(F32), row(ffn_conv_b[l]), w_down[l].astype(BF16), row(norm_ple_g[l]),
                  w_ple_gate[l].astype(BF16), w_ple[l].astype(BF16), row(norm_final_g), seq,
                  final_norm=(l == depth - 1))

    return xt.reshape(batch, seq, D_MODEL)
```

```python
import functools
import math

import jax
import jax.numpy as jnp
from jax import lax
from jax.experimental import pallas as pl
from jax.experimental.pallas import tpu as pltpu

D_MODEL = 1024
PLE_DIM = 256
M_HEADS = 4
M_HEAD_DIM = 128
M_WIDTH = M_HEADS * M_HEAD_DIM
M_CONV = 4
DA_HEADS = 4
DA_HEAD_DIM = 64
DA_V_DIM = 2 * DA_HEAD_DIM
DA_WIDTH = DA_HEADS * DA_V_DIM
REL_BUCKETS = 32
REL_MAX_DIST = 128
D_FF = 2816
FFN_CONV = 3
EPS = 1e-6
NEG_BIG = -1e30
LOG2E = 1.4426950408889634

LANES = 128
SUBLANES = 8
VMEM_LIMIT = 56 * 1024 * 1024

TOKEN_TILE = 512
M_CHUNK = 256
M_BATCH = 8
ATT_BLOCK = 512
ATT_SUB = ATT_BLOCK // 2
FF_CHUNK = 256
N_FF_CHUNKS = D_FF // FF_CHUNK
FF_SLABS = 2 * FF_CHUNK // LANES
FF_LOOKAHEAD = 3
FF_SLAB_SETS = FF_LOOKAHEAD + 1

F32 = jnp.float32
BF16 = jnp.bfloat16

C_IF = 4 * M_WIDTH
N_IF = 2 * M_HEADS
IN_COLS = C_IF + N_IF + 3 * DA_WIDTH + 2 * D_MODEL
C_TAIL = IN_COLS - N_IF


def _dot(a, b):
    return jnp.dot(a, b, preferred_element_type=F32)


def _dot_nt(a, b):
    return lax.dot_general(a, b, (((1,), (1,)), ((), ())), preferred_element_type=F32)


def _rms(x, g):
    return x * lax.rsqrt(jnp.mean(x * x, -1, keepdims=True) + EPS) * g


def _sigmoid(x):
    return 1.0 / (1.0 + jnp.exp(-x))


def _slab_store(u, slab_ref, slab0):
    tm, n = u.shape
    for j in range(n // LANES):
        slab_ref[slab0 + j, SUBLANES:SUBLANES + tm, :] = u[:, j * LANES:(j + 1) * LANES]


def _slab_conv(tm, n, col0, slab_ref, slab0, carry_ref, cw_ref, cb_ref):
    taps = cw_ref.shape[0]
    outs = []
    for j in range(n // LANES):
        cols = slice(col0 + j * LANES, col0 + (j + 1) * LANES)
        s = slab0 + j
        slab_ref[s, 0:SUBLANES, :] = carry_ref[:, cols]
        carry_ref[:, cols] = slab_ref[s, tm:tm + SUBLANES, :]
        y = cb_ref[:, cols]
        for k in range(taps):
            y = y + cw_ref[taps - 1 - k:taps - k, cols] * slab_ref[s, pl.ds(SUBLANES - k, tm), :]
        outs.append(y)
    return jnp.concatenate(outs, axis=1)


def _cumsum_rows(tril, x):
    hi = x.astype(BF16)
    r1 = x - hi.astype(F32)
    mid = r1.astype(BF16)
    lo = (r1 - mid.astype(F32)).astype(BF16)
    return _dot(tril, hi) + _dot(tril, mid) + _dot(tril, lo)


def _const_spec(shape):
    nd = len(shape)
    return pl.BlockSpec(shape, lambda *_: (0,) * nd, pipeline_mode=pl.Buffered(1))


W_BLOCK = 512


def _wprep_kernel(w_ref, out_ref):
    j = pl.program_id(0)
    row = lax.broadcasted_iota(jnp.int32, w_ref.shape, 0)
    keep = (j < C_TAIL // W_BLOCK) | (row < N_IF)
    out_ref[...] = jnp.where(keep, w_ref[...], 0.0).T.astype(BF16)


def _wprep(w_t):
    n_main = C_TAIL // W_BLOCK

    def src(j):
        shifted = j * W_BLOCK + jnp.where(j >= C_IF // W_BLOCK, N_IF, 0)
        return pl.multiple_of(jnp.where(j < n_main, shifted, C_IF), SUBLANES), 0

    return pl.pallas_call(
        _wprep_kernel,
        grid=(n_main + 1,),
        in_specs=[pl.BlockSpec((pl.Element(W_BLOCK), pl.Element(D_MODEL)), src)],
        out_specs=pl.BlockSpec((D_MODEL, W_BLOCK), lambda j: (0, j)),
        out_shape=jax.ShapeDtypeStruct((D_MODEL, C_TAIL + W_BLOCK), BF16),
        compiler_params=pltpu.CompilerParams(dimension_semantics=("arbitrary",), vmem_limit_bytes=VMEM_LIMIT),
        name="wprep",
    )(w_t)


def _inproj_kernel(x_ref, g_ref, w_ref, cw_ref, cb_ref, bif_ref,
                   mq_ref, mk_ref, mv_ref, mo_ref, dq_ref, dk_ref, dv_ref, gates_ref, ifg_ref,
                   carry_ref, slab_ref, *, tiles_per_seq):
    tm = x_ref.shape[0]
    i = pl.program_id(0)
    hb = _rms(x_ref[...], g_ref[...]).astype(BF16)

    @pl.when(i % tiles_per_seq == 0)
    def _():
        carry_ref[...] = jnp.zeros(carry_ref.shape, F32)

    def conv_silu(col0):
        y = _slab_conv(tm, M_WIDTH, col0, slab_ref, col0 // LANES, carry_ref, cw_ref, cb_ref)
        return y * _sigmoid(y)

    c = 2 * M_WIDTH
    mv_ref[...] = _dot(hb, w_ref[:, c:c + M_WIDTH]).astype(BF16).T; c += M_WIDTH
    mo_ref[...] = _dot(hb, w_ref[:, c:c + M_WIDTH]).astype(BF16); c += M_WIDTH
    _slab_store(_dot(hb, w_ref[:, 0:M_WIDTH]), slab_ref, 0)
    dq_ref[...] = (_dot(hb, w_ref[:, c:c + DA_WIDTH]) * (LOG2E * DA_HEAD_DIM ** -0.5)).astype(BF16); c += DA_WIDTH
    dk_ref[...] = _dot(hb, w_ref[:, c:c + DA_WIDTH]).astype(BF16); c += DA_WIDTH
    dv_ref[...] = _dot(hb, w_ref[:, c:c + DA_WIDTH]).astype(BF16).T; c += DA_WIDTH
    _slab_store(_dot(hb, w_ref[:, M_WIDTH:2 * M_WIDTH]), slab_ref, M_WIDTH // LANES)
    for j in range(4):
        gates_ref[:, j * 512:(j + 1) * 512] = _dot(hb, w_ref[:, c:c + 512]).astype(BF16); c += 512

    ifg_ref[...] = _dot(hb, w_ref[:, C_TAIL:C_TAIL + LANES]) + bif_ref[...]

    mq_ref[...] = conv_silu(0).astype(BF16)
    mk_ref[...] = (conv_silu(M_WIDTH) * (M_HEAD_DIM ** -0.5)).astype(BF16)


def _inproj(x2, g, w_all, cw, cb, bif, seq):
    T = x2.shape[0]
    tm = TOKEN_TILE
    tps = seq // tm
    row = lambda i: (i, 0)
    bf = lambda n: jax.ShapeDtypeStruct((T, n), BF16)
    bf_t = lambda n: jax.ShapeDtypeStruct((T // seq, n, seq), BF16)
    blk = pl.BlockSpec((tm, M_WIDTH), row)
    blk_t = pl.BlockSpec((None, M_WIDTH, tm), lambda i: (i // tps, 0, i % tps))
    out_shapes = [bf(M_WIDTH), bf(M_WIDTH), bf_t(M_WIDTH), bf(M_WIDTH), bf(DA_WIDTH), bf(DA_WIDTH), bf_t(DA_WIDTH),
                  bf(2 * D_MODEL), jax.ShapeDtypeStruct((T, LANES), F32)]
    out_specs = [blk, blk, blk_t, blk, blk, blk, blk_t,
                 pl.BlockSpec((tm, 2 * D_MODEL), row), pl.BlockSpec((tm, LANES), row)]
    return pl.pallas_call(
        functools.partial(_inproj_kernel, tiles_per_seq=tps),
        grid=(T // tm,),
        in_specs=[pl.BlockSpec((tm, D_MODEL), row), _const_spec((1, D_MODEL)), _const_spec(w_all.shape),
                  _const_spec((M_CONV, 2 * M_WIDTH)), _const_spec((1, 2 * M_WIDTH)), _const_spec((1, LANES))],
        out_specs=out_specs,
        out_shape=out_shapes,
        scratch_shapes=[pltpu.VMEM((SUBLANES, 2 * M_WIDTH), F32),
                        pltpu.VMEM((2 * M_WIDTH // LANES, SUBLANES + tm, LANES), F32)],
        compiler_params=pltpu.CompilerParams(dimension_semantics=("arbitrary",), vmem_limit_bytes=VMEM_LIMIT),
        name="inproj",
    )(x2, g, w_all, cw, cb, bif)


def _mlstm_kernel(mq_ref, mk_ref, mv_ref, mo_ref, ifg_ref, g_ref, out_ref, ct_ref, m_ref):
    L = mq_ref.shape[1]
    d = M_HEAD_DIM

    @pl.when(pl.program_id(1) == 0)
    def _():
        ct_ref[...] = jnp.zeros(ct_ref.shape, F32)
        m_ref[...] = jnp.zeros(m_ref.shape, F32)

    kidx = lax.broadcasted_iota(jnp.int32, (L, L), 0)
    qidx = lax.broadcasted_iota(jnp.int32, (L, L), 1)
    tril = jnp.where(kidx >= qidx, 1.0, 0.0).astype(BF16)
    causal = kidx <= qidx

    for s in range(mq_ref.shape[0]):
        ig = ifg_ref[s]
        fg = pltpu.roll(ig, LANES - M_HEADS, axis=1)
        logf = jnp.minimum(fg, 0.0) - jnp.log1p(jnp.exp(-jnp.abs(fg)))
        b = _cumsum_rows(tril, logf)
        b_tot = b[L - 1:L, :]
        u = ig - b
        w_end = b_tot + u
        m_loc = jnp.max(w_end, axis=0, keepdims=True)
        a = jnp.exp(w_end - m_loc)
        m_prev = m_ref[s]
        m_new = jnp.maximum(b_tot + m_prev, m_loc)
        s_old = jnp.exp(b_tot + m_prev - m_new)
        s_loc = jnp.exp(m_loc - m_new)
        li_t = (b + m_prev).T
        b_t = b.T
        m_ref[s] = m_new

        for h in range(M_HEADS):
            hs = slice(h * d, (h + 1) * d)
            q = mq_ref[s, :, hs]
            k = mk_ref[s, :, hs]
            v_t = mv_ref[s, hs, :]
            ct_prev = ct_ref[s, h]
            s_t = _dot_nt(k, q)
            inter = _dot_nt(ct_prev.astype(BF16), q)
            log_d = jnp.where(causal, u[:, h:h + 1] + b_t[h:h + 1, :], -jnp.inf)
            li = li_t[h:h + 1, :]
            m_t = jnp.maximum(li, jnp.max(log_d, axis=0, keepdims=True))
            p_t = s_t * jnp.exp(log_d - m_t)
            s_inter = jnp.exp(li - m_t)
            num = s_inter * inter[:d] + _dot(v_t, p_t.astype(BF16))
            den = s_inter * inter[d:d + 1] + jnp.sum(p_t, axis=0, keepdims=True)
            hh = num / jnp.maximum(jnp.abs(den), jnp.exp(-m_t))
            y = (hh * lax.rsqrt(jnp.mean(hh * hh, axis=0, keepdims=True) + EPS)).T
            out_ref[s, :, hs] = (_sigmoid(mo_ref[s, :, hs].astype(F32)) * (y * g_ref[:, hs])).astype(BF16)

            ak = a[:, h:h + 1] * k.astype(F32)
            so = s_old[:, h:h + 1]
            sl = s_loc[:, h:h + 1]
            ct_ref[s, h, 0:d, :] = so * ct_prev[:d] + sl * _dot(v_t, ak.astype(BF16))
            ct_ref[s, h, d:d + 1, :] = so * ct_prev[d:d + 1] + sl * jnp.sum(ak, axis=0, keepdims=True)


def _mlstm(mq, mk, mv, mo, ifg, g, batch, seq):
    L = M_CHUNK
    nb = M_BATCH
    seq3 = lambda t: t.reshape(batch, seq, t.shape[-1])
    blk = pl.BlockSpec((nb, L, M_WIDTH), lambda b, c: (b, c, 0))
    blk_t = pl.BlockSpec((nb, M_WIDTH, L), lambda b, c: (b, 0, c))
    out = pl.pallas_call(
        _mlstm_kernel,
        grid=(batch // nb, seq // L),
        in_specs=[blk, blk, blk_t, blk, pl.BlockSpec((nb, L, LANES), lambda b, c: (b, c, 0)),
                  _const_spec((1, M_WIDTH))],
        out_specs=blk,
        out_shape=jax.ShapeDtypeStruct((batch, seq, M_WIDTH), BF16),
        scratch_shapes=[pltpu.VMEM((nb, M_HEADS, M_HEAD_DIM + 2 * SUBLANES, M_HEAD_DIM), F32),
                        pltpu.VMEM((nb, 1, LANES), F32)],
        compiler_params=pltpu.CompilerParams(dimension_semantics=("arbitrary", "arbitrary"),
                                             vmem_limit_bytes=VMEM_LIMIT),
        name="mlstm",
    )(seq3(mq), seq3(mk), mv, seq3(mo), seq3(ifg), g)
    return out.reshape(batch * seq, M_WIDTH)


def _t5_bucket(n):
    max_exact = REL_BUCKETS // 2
    nf = jnp.maximum(n, 1).astype(F32)
    large = max_exact + (jnp.log(nf / max_exact) / math.log(REL_MAX_DIST / max_exact)
                         * (REL_BUCKETS - max_exact)).astype(jnp.int32)
    large = jnp.minimum(large, REL_BUCKETS - 1)
    return jnp.where(n < max_exact, n, large)


def _attn_kernel(rb_ref, lam_ref, q_ref, k_ref, v_ref, g_ref, out_ref,
                 bias_ref, m_ref, l_ref, acc_ref, s_ref, *, lam_init):
    tq = ATT_BLOCK
    tk = tq
    sb = ATT_SUB
    nq = q_ref.shape[0] // tq
    h = pl.program_id(1)

    kk = lax.broadcasted_iota(jnp.int32, (sb, sb), 0)
    qq = lax.broadcasted_iota(jnp.int32, (sb, sb), 1)

    @pl.when((pl.program_id(0) == 0) & (h == 0))
    def _():
        for delta in range(2):
            bucket = _t5_bucket(jnp.maximum(qq + delta * sb - kk, 0))
            for hh in range(DA_HEADS):
                t = jnp.zeros((sb, sb), F32)
                for j in range(REL_BUCKETS):
                    t = jnp.where(bucket == j, rb_ref[j, hh], t)
                bias_ref[hh, delta] = (t - rb_ref[REL_BUCKETS - 1, hh]) * LOG2E

    lv = lam_ref[...]
    lam = (jnp.exp(jnp.sum(lv[0:1] * lv[1:2], axis=1, keepdims=True))
           - jnp.exp(jnp.sum(lv[2:3] * lv[3:4], axis=1, keepdims=True)) + lam_init)

    n_groups = 2 * tq // sb
    d0 = bias_ref[h, 0]
    d1 = bias_ref[h, 1]
    causal = qq >= kk
    ident = lambda s: s
    near = lambda s: jnp.concatenate([s[:sb], s[sb:] + d1], axis=0)
    early = lambda s: jnp.where(causal, s + d0, NEG_BIG)
    late = lambda s: jnp.concatenate([s[:sb] + d1, jnp.where(causal, s[sb:] + d0, NEG_BIG)], axis=0)

    def plan(j, n):
        if j == n - 1:
            return [(sb, early), (tk, late)] * (n_groups // 2)
        if j == n - 2:
            return [(tk, near), (tk, ident)] * (n_groups // 2)
        return [(tk, ident)] * n_groups

    lane = lax.broadcasted_iota(jnp.int32, (tq, DA_V_DIM), 1)
    step = 0
    for qb in range(nq):
        n = qb + 1
        st = qb % 2
        q = q_ref[qb * tq:(qb + 1) * tq, :]
        zero = jnp.zeros_like(q)
        qcat = jnp.concatenate([jnp.where(lane < DA_HEAD_DIM, q, zero),
                                jnp.where(lane >= DA_HEAD_DIM, q, zero)], axis=0)
        m_ref[st] = jnp.full(m_ref.shape[1:], -jnp.inf, F32)
        l_ref[st] = jnp.zeros(l_ref.shape[1:], F32)
        acc_ref[st] = jnp.zeros(acc_ref.shape[1:], F32)

        def scores(j, buf):
            k = k_ref[j * tk:(j + 1) * tk, :]
            for g, (rows, _) in enumerate(plan(j, n)):
                s_ref[buf, g, 0:rows, :] = _dot_nt(k[:rows], qcat[g * sb:(g + 1) * sb])

        def softmax_pv(j, buf):
            v_t = v_ref[:, j * tk:(j + 1) * tk]
            for g, (rows, fix) in enumerate(plan(j, n)):
                cols = slice(g * sb, (g + 1) * sb)
                s = fix(s_ref[buf, g, 0:rows, :])
                m_prev = m_ref[st, :, cols]
                m_new = jnp.maximum(m_prev, jnp.max(s, axis=0, keepdims=True))
                alpha = jnp.exp2(m_prev - m_new)
                p = jnp.exp2(s - m_new)
                l_ref[st, :, cols] = alpha * l_ref[st, :, cols] + jnp.sum(p, axis=0, keepdims=True)
                acc_ref[st, :, cols] = (alpha * acc_ref[st, :, cols]
                                        + _dot(v_t[:, :rows], p.astype(BF16)))
                m_ref[st, :, cols] = m_new

        scores(0, step % 2)
        for j in range(n):
            if j + 1 < n:
                scores(j + 1, (step + 1) % 2)
            softmax_pv(j, step % 2)
            step += 1

        o = acc_ref[st] * (1.0 / l_ref[st])
        out = (o[:, :tq] - lam * o[:, tq:]).T
        out_ref[qb * tq:(qb + 1) * tq, :] = (_rms(out, g_ref[...]) * (1.0 - lam_init)).astype(BF16)


def _attention(rel_bias, lam_vec, dq, dk, dv, g, batch, seq, lam_init):
    T = dq.shape[0]
    tq = ATT_BLOCK
    seq_blk = pl.BlockSpec((seq, DA_V_DIM), lambda b, h: (b, h))
    return pl.pallas_call(
        functools.partial(_attn_kernel, lam_init=lam_init),
        grid=(batch, DA_HEADS),
        in_specs=[pl.BlockSpec(memory_space=pltpu.SMEM),
                  pl.BlockSpec((4, DA_HEAD_DIM), lambda b, h: (0, 0)),
                  seq_blk, seq_blk,
                  pl.BlockSpec((None, DA_V_DIM, seq), lambda b, h: (b, h, 0)),
                  pl.BlockSpec((1, DA_V_DIM), lambda b, h: (0, h))],
        out_specs=seq_blk,
        out_shape=jax.ShapeDtypeStruct((T, DA_WIDTH), BF16),
        scratch_shapes=[pltpu.VMEM((DA_HEADS, 2, ATT_SUB, ATT_SUB), F32),
                        pltpu.VMEM((2, 1, 2 * tq), F32),
                        pltpu.VMEM((2, 1, 2 * tq), F32),
                        pltpu.VMEM((2, DA_V_DIM, 2 * tq), F32),
                        pltpu.VMEM((2, 2 * tq // ATT_SUB, tq, ATT_SUB), F32)],
        compiler_params=pltpu.CompilerParams(dimension_semantics=("arbitrary",) * 2,
                                             vmem_limit_bytes=VMEM_LIMIT),
        name="diffattn",
    )(rel_bias, lam_vec, dq, dk, dv, g)


def _gelu_tanh(x):
    return 0.5 * x * (1.0 + jnp.tanh(math.sqrt(2.0 / math.pi) * (x + 0.044715 * (x * x * x))))


def _ffn_kernel(gm_ref, hd_ref, gates_ref, x_ref, p_ref, wm_ref, wd_ref, wo_ref,
                gf_ref, wup_ref, cw_ref, cb_ref, wdn_ref, gp_ref, wpg_ref, wp_ref, gl_ref,
                out_ref, carry_ref, slab_ref, acc_ref, *, tiles_per_seq, final_norm):
    tm = x_ref.shape[0]
    i = pl.program_id(0)

    ya = _dot(gm_ref[...], wm_ref[...])
    yb = _dot(hd_ref[...], wd_ref[...])
    ga = _sigmoid(gates_ref[:, :D_MODEL].astype(F32))
    gb = _sigmoid(gates_ref[:, D_MODEL:].astype(F32))
    x1 = x_ref[...] + _dot((ga * ya + gb * yb).astype(BF16), wo_ref[...])

    hb = _rms(x1, gf_ref[...]).astype(BF16)

    @pl.when(i % tiles_per_seq == 0)
    def _():
        carry_ref[...] = jnp.zeros(carry_ref.shape, F32)

    def slab0(c, half):
        return (c % FF_SLAB_SETS) * FF_SLABS + half * (FF_CHUNK // LANES)

    def up(c):
        for half, off in enumerate((0, D_FF)):
            lo = off + c * FF_CHUNK
            _slab_store(_dot(hb, wup_ref[:, lo:lo + FF_CHUNK]), slab_ref, slab0(c, half))

    def conv(c, half):
        return _slab_conv(tm, FF_CHUNK, half * D_FF + c * FF_CHUNK, slab_ref, slab0(c, half),
                          carry_ref, cw_ref, cb_ref)

    for c in range(FF_LOOKAHEAD):
        up(c)
    for c in range(N_FF_CHUNKS):
        if c + FF_LOOKAHEAD < N_FF_CHUNKS:
            up(c + FF_LOOKAHEAD)
        act = (_gelu_tanh(conv(c, 1)) * conv(c, 0)).astype(BF16)
        d = _dot(act, wdn_ref[c * FF_CHUNK:(c + 1) * FF_CHUNK, :])
        if c == 0:
            acc_ref[...] = d
        else:
            acc_ref[...] += d

    x2 = x1 + acc_ref[...]
    hg = _rms(x2, gp_ref[...]).astype(BF16)
    gate = _sigmoid(_dot(hg, wpg_ref[...]))
    pe = _dot(p_ref[...].astype(BF16), wp_ref[...])
    x3 = x2 + gate * pe
    out_ref[...] = _rms(x3, gl_ref[...]) if final_norm else x3


def _ffn(gm, hd, gates, x2, p2, wm, wd, wo, gf, wup, cw, cb, wdn, gp, wpg, wp, gl, seq, final_norm):
    T = x2.shape[0]
    tm = TOKEN_TILE
    row = lambda i: (i, 0)
    return pl.pallas_call(
        functools.partial(_ffn_kernel, tiles_per_seq=seq // tm, final_norm=final_norm),
        grid=(T // tm,),
        in_specs=[pl.BlockSpec((tm, M_WIDTH), row), pl.BlockSpec((tm, DA_WIDTH), row),
                  pl.BlockSpec((tm, 2 * D_MODEL), row), pl.BlockSpec((tm, D_MODEL), row),
                  pl.BlockSpec((tm, PLE_DIM), row),
                  _const_spec((M_WIDTH, D_MODEL)), _const_spec((DA_WIDTH, D_MODEL)),
                  _const_spec((D_MODEL, D_MODEL)), _const_spec((1, D_MODEL)),
                  _const_spec((D_MODEL, 2 * D_FF)), _const_spec((FFN_CONV, 2 * D_FF)),
                  _const_spec((1, 2 * D_FF)), _const_spec((D_FF, D_MODEL)),
                  _const_spec((1, D_MODEL)), _const_spec((D_MODEL, D_MODEL)),
                  _const_spec((PLE_DIM, D_MODEL)), _const_spec((1, D_MODEL))],
        out_specs=pl.BlockSpec((tm, D_MODEL), row),
        out_shape=jax.ShapeDtypeStruct((T, D_MODEL), F32),
        scratch_shapes=[pltpu.VMEM((SUBLANES, 2 * D_FF), F32),
                        pltpu.VMEM((FF_SLAB_SETS * FF_SLABS, SUBLANES + tm, LANES), F32),
                        pltpu.VMEM((tm, D_MODEL), F32)],
        compiler_params=pltpu.CompilerParams(dimension_semantics=("arbitrary",), vmem_limit_bytes=VMEM_LIMIT),
        name="convffn",
    )(gm, hd, gates, x2, p2, wm, wd, wo, gf, wup, cw, cb, wdn, gp, wpg, wp, gl)


def kernel(x, p, rel_bias, norm_mix_g, w_in, b_if, m_conv_w, m_conv_b, m_norm_g, da_lambda, da_norm_g,
           w_br_m, w_br_d, w_out, norm_ffn_g, w_up, ffn_conv_w, ffn_conv_b, w_down, norm_ple_g, w_ple_gate,
           w_ple, norm_final_g):
    batch, seq, _ = x.shape
    depth = w_in.shape[0]
    T = batch * seq
    assert seq % TOKEN_TILE == 0 and seq % M_CHUNK == 0 and seq % ATT_BLOCK == 0 and batch % M_BATCH == 0
    xt = x.reshape(T, D_MODEL)
    row = lambda v: v.reshape(1, -1).astype(F32)

    for l in range(depth):
        w_all = _wprep(jnp.swapaxes(w_in[l].astype(F32), 0, 1))
        bif = jnp.concatenate([b_if[l].astype(F32), jnp.zeros((LANES - N_IF,), F32)]).reshape(1, LANES)

        mq, mk, mv, mo, dq, dk, dv, gates, ifg = _inproj(
            xt, row(norm_mix_g[l]), w_all, m_conv_w[l].astype(F32), row(m_conv_b[l]), bif, seq)

        gm = _mlstm(mq, mk, mv, mo, ifg, row(m_norm_g[l]), batch, seq)

        lam_init = 0.8 - 0.6 * math.exp(-0.3 * l)
        hd = _attention(rel_bias.astype(F32), da_lambda[l].astype(F32), dq, dk, dv, row(da_norm_g[l]),
                        batch, seq, lam_init)

        xt = _ffn(gm, hd, gates, xt, p[l].reshape(T, PLE_DIM),
                  w_br_m[l].astype(BF16), w_br_d[l].astype(BF16), w_out[l].astype(BF16),
                  row(norm_ffn_g[l]), w_up[l].astype(BF16),
                  ffn_conv_w[l].astype(F32), row(ffn_conv_b[l]), w_down[l].astype(BF16), row(norm_ple_g[l]),
                  w_ple_gate[l].astype(BF16), w_ple[l].astype(BF16), row(norm_final_g), seq,
                  final_norm=(l == depth - 1))

    return xt.reshape(batch, seq, D_MODEL)
```

```python
import functools
import math

import jax
import jax.numpy as jnp
from jax import lax
from jax.experimental import pallas as pl
from jax.experimental.pallas import tpu as pltpu

D_MODEL = 1024
PLE_DIM = 256
M_HEADS = 4
M_HEAD_DIM = 128
M_WIDTH = M_HEADS * M_HEAD_DIM
M_CONV = 4
DA_HEADS = 4
DA_HEAD_DIM = 64
DA_V_DIM = 2 * DA_HEAD_DIM
DA_WIDTH = DA_HEADS * DA_V_DIM
REL_BUCKETS = 32
REL_MAX_DIST = 128
D_FF = 2816
FFN_CONV = 3
EPS = 1e-6
NEG_BIG = -1e30
LOG2E = 1.4426950408889634

LANES = 128
SUBLANES = 8
VMEM_LIMIT = 56 * 1024 * 1024

TOKEN_TILE = 512
M_CHUNK = 256
M_BATCH = 8
ATT_BLOCK = 512
ATT_SUB = ATT_BLOCK // 2
FF_CHUNK = 256
N_FF_CHUNKS = D_FF // FF_CHUNK
FF_SLABS = 2 * FF_CHUNK // LANES
FF_LOOKAHEAD = 3
FF_SLAB_SETS = FF_LOOKAHEAD + 1

F32 = jnp.float32
BF16 = jnp.bfloat16

C_IF = 4 * M_WIDTH
N_IF = 2 * M_HEADS
IN_COLS = C_IF + N_IF + 3 * DA_WIDTH + 2 * D_MODEL
C_TAIL = IN_COLS - N_IF


def _dot(a, b):
    return jnp.dot(a, b, preferred_element_type=F32)


def _dot_nt(a, b):
    return lax.dot_general(a, b, (((1,), (1,)), ((), ())), preferred_element_type=F32)


def _rms(x, g):
    return x * lax.rsqrt(jnp.mean(x * x, -1, keepdims=True) + EPS) * g


def _sigmoid(x):
    return 1.0 / (1.0 + jnp.exp(-x))


def _slab_store(u, slab_ref, slab0):
    tm, n = u.shape
    for j in range(n // LANES):
        slab_ref[slab0 + j, SUBLANES:SUBLANES + tm, :] = u[:, j * LANES:(j + 1) * LANES]


def _slab_conv(tm, n, col0, slab_ref, slab0, carry_ref, cw_ref, cb_ref):
    taps = cw_ref.shape[0]
    outs = []
    for j in range(n // LANES):
        cols = slice(col0 + j * LANES, col0 + (j + 1) * LANES)
        s = slab0 + j
        slab_ref[s, 0:SUBLANES, :] = carry_ref[:, cols]
        carry_ref[:, cols] = slab_ref[s, tm:tm + SUBLANES, :]
        y = cb_ref[:, cols]
        for k in range(taps):
            y = y + cw_ref[taps - 1 - k:taps - k, cols] * slab_ref[s, pl.ds(SUBLANES - k, tm), :]
        outs.append(y)
    return jnp.concatenate(outs, axis=1)


def _cumsum_rows(tril, x):
    hi = x.astype(BF16)
    r1 = x - hi.astype(F32)
    mid = r1.astype(BF16)
    lo = (r1 - mid.astype(F32)).astype(BF16)
    return _dot(tril, hi) + _dot(tril, mid) + _dot(tril, lo)


def _const_spec(shape):
    nd = len(shape)
    return pl.BlockSpec(shape, lambda *_: (0,) * nd, pipeline_mode=pl.Buffered(1))


W_BLOCK = 512


def _wprep_kernel(w_ref, out_ref):
    j = pl.program_id(0)
    row = lax.broadcasted_iota(jnp.int32, w_ref.shape, 0)
    keep = (j < C_TAIL // W_BLOCK) | (row < N_IF)
    out_ref[...] = jnp.where(keep, w_ref[...], 0.0).T.astype(BF16)


def _wprep(w_t):
    n_main = C_TAIL // W_BLOCK

    def src(j):
        shifted = j * W_BLOCK + jnp.where(j >= C_IF // W_BLOCK, N_IF, 0)
        return pl.multiple_of(jnp.where(j < n_main, shifted, C_IF), SUBLANES), 0

    return pl.pallas_call(
        _wprep_kernel,
        grid=(n_main + 1,),
        in_specs=[pl.BlockSpec((pl.Element(W_BLOCK), pl.Element(D_MODEL)), src)],
        out_specs=pl.BlockSpec((D_MODEL, W_BLOCK), lambda j: (0, j)),
        out_shape=jax.ShapeDtypeStruct((D_MODEL, C_TAIL + W_BLOCK), BF16),
        compiler_params=pltpu.CompilerParams(dimension_semantics=("arbitrary",), vmem_limit_bytes=VMEM_LIMIT),
        name="wprep",
    )(w_t)


def _inproj_kernel(x_ref, g_ref, w_ref, cw_ref, cb_ref, bif_ref,
                   mq_ref, mk_ref, mv_ref, mo_ref, dq_ref, dk_ref, dv_ref, gates_ref, ifg_ref,
                   carry_ref, slab_ref, *, tiles_per_seq):
    tm = x_ref.shape[0]
    i = pl.program_id(0)
    hb = _rms(x_ref[...], g_ref[...]).astype(BF16)

    @pl.when(i % tiles_per_seq == 0)
    def _():
        carry_ref[...] = jnp.zeros(carry_ref.shape, F32)

    def conv_silu(col0):
        y = _slab_conv(tm, M_WIDTH, col0, slab_ref, col0 // LANES, carry_ref, cw_ref, cb_ref)
        return y * _sigmoid(y)

    c = 2 * M_WIDTH
    mv_ref[...] = _dot(hb, w_ref[:, c:c + M_WIDTH]).astype(BF16).T; c += M_WIDTH
    mo_ref[...] = _dot(hb, w_ref[:, c:c + M_WIDTH]).astype(BF16); c += M_WIDTH
    _slab_store(_dot(hb, w_ref[:, 0:M_WIDTH]), slab_ref, 0)
    dq_ref[...] = (_dot(hb, w_ref[:, c:c + DA_WIDTH]) * (LOG2E * DA_HEAD_DIM ** -0.5)).astype(BF16); c += DA_WIDTH
    dk_ref[...] = _dot(hb, w_ref[:, c:c + DA_WIDTH]).astype(BF16); c += DA_WIDTH
    dv_ref[...] = _dot(hb, w_ref[:, c:c + DA_WIDTH]).astype(BF16).T; c += DA_WIDTH
    _slab_store(_dot(hb, w_ref[:, M_WIDTH:2 * M_WIDTH]), slab_ref, M_WIDTH // LANES)
    for j in range(4):
        gates_ref[:, j * 512:(j + 1) * 512] = _dot(hb, w_ref[:, c:c + 512]).astype(BF16); c += 512

    ifg_ref[...] = _dot(hb, w_ref[:, C_TAIL:C_TAIL + LANES]) + bif_ref[...]

    mq_ref[...] = conv_silu(0).astype(BF16)
    mk_ref[...] = (conv_silu(M_WIDTH) * (M_HEAD_DIM ** -0.5)).astype(BF16)


def _inproj(x2, g, w_all, cw, cb, bif, seq):
    T = x2.shape[0]
    tm = TOKEN_TILE
    tps = seq // tm
    row = lambda i: (i, 0)
    bf = lambda n: jax.ShapeDtypeStruct((T, n), BF16)
    bf_t = lambda n: jax.ShapeDtypeStruct((T // seq, n, seq), BF16)
    blk = pl.BlockSpec((tm, M_WIDTH), row)
    blk_t = pl.BlockSpec((None, M_WIDTH, tm), lambda i: (i // tps, 0, i % tps))
    out_shapes = [bf(M_WIDTH), bf(M_WIDTH), bf_t(M_WIDTH), bf(M_WIDTH), bf(DA_WIDTH), bf(DA_WIDTH), bf_t(DA_WIDTH),
                  bf(2 * D_MODEL), jax.ShapeDtypeStruct((T, LANES), F32)]
    out_specs = [blk, blk, blk_t, blk, blk, blk, blk_t,
                 pl.BlockSpec((tm, 2 * D_MODEL), row), pl.BlockSpec((tm, LANES), row)]
    return pl.pallas_call(
        functools.partial(_inproj_kernel, tiles_per_seq=tps),
        grid=(T // tm,),
        in_specs=[pl.BlockSpec((tm, D_MODEL), row), _const_spec((1, D_MODEL)), _const_spec(w_all.shape),
                  _const_spec((M_CONV, 2 * M_WIDTH)), _const_spec((1, 2 * M_WIDTH)), _const_spec((1, LANES))],
        out_specs=out_specs,
        out_shape=out_shapes,
        scratch_shapes=[pltpu.VMEM((SUBLANES, 2 * M_WIDTH), F32),
                        pltpu.VMEM((2 * M_WIDTH // LANES, SUBLANES + tm, LANES), F32)],
        compiler_params=pltpu.CompilerParams(dimension_semantics=("arbitrary",), vmem_limit_bytes=VMEM_LIMIT),
        name="inproj",
    )(x2, g, w_all, cw, cb, bif)


def _mlstm_kernel(mq_ref, mk_ref, mv_ref, mo_ref, ifg_ref, g_ref, out_ref, ct_ref, m_ref):
    L = mq_ref.shape[1]
    d = M_HEAD_DIM

    @pl.when(pl.program_id(1) == 0)
    def _():
        ct_ref[...] = jnp.zeros(ct_ref.shape, F32)
        m_ref[...] = jnp.zeros(m_ref.shape, F32)

    kidx = lax.broadcasted_iota(jnp.int32, (L, L), 0)
    qidx = lax.broadcasted_iota(jnp.int32, (L, L), 1)
    tril = jnp.where(kidx >= qidx, 1.0, 0.0).astype(BF16)
    causal = kidx <= qidx

    for s in range(mq_ref.shape[0]):
        ig = ifg_ref[s]
        fg = pltpu.roll(ig, LANES - M_HEADS, axis=1)
        logf = jnp.minimum(fg, 0.0) - jnp.log1p(jnp.exp(-jnp.abs(fg)))
        b = _cumsum_rows(tril, logf)
        b_tot = b[L - 1:L, :]
        u = ig - b
        w_end = b_tot + u
        m_loc = jnp.max(w_end, axis=0, keepdims=True)
        a = jnp.exp(w_end - m_loc)
        m_prev = m_ref[s]
        m_new = jnp.maximum(b_tot + m_prev, m_loc)
        s_old = jnp.exp(b_tot + m_prev - m_new)
        s_loc = jnp.exp(m_loc - m_new)
        li_t = (b + m_prev).T
        b_t = b.T
        m_ref[s] = m_new

        for h in range(M_HEADS):
            hs = slice(h * d, (h + 1) * d)
            q = mq_ref[s, :, hs]
            k = mk_ref[s, :, hs]
            v_t = mv_ref[s, hs, :]
            ct_prev = ct_ref[s, h]
            s_t = _dot_nt(k, q)
            inter = _dot_nt(ct_prev.astype(BF16), q)
            log_d = jnp.where(causal, u[:, h:h + 1] + b_t[h:h + 1, :], -jnp.inf)
            li = li_t[h:h + 1, :]
            m_t = jnp.maximum(li, jnp.max(log_d, axis=0, keepdims=True))
            p_t = s_t * jnp.exp(log_d - m_t)
            s_inter = jnp.exp(li - m_t)
            num = s_inter * inter[:d] + _dot(v_t, p_t.astype(BF16))
            den = s_inter * inter[d:d + 1] + jnp.sum(p_t, axis=0, keepdims=True)
            hh = num / jnp.maximum(jnp.abs(den), jnp.exp(-m_t))
            y = (hh * lax.rsqrt(jnp.mean(hh * hh, axis=0, keepdims=True) + EPS)).T
            out_ref[s, :, hs] = (_sigmoid(mo_ref[s, :, hs].astype(F32)) * (y * g_ref[:, hs])).astype(BF16)

            ak = a[:, h:h + 1] * k.astype(F32)
            so = s_old[:, h:h + 1]
            sl = s_loc[:, h:h + 1]
            ct_ref[s, h, 0:d, :] = so * ct_prev[:d] + sl * _dot(v_t, ak.astype(BF16))
            ct_ref[s, h, d:d + 1, :] = so * ct_prev[d:d + 1] + sl * jnp.sum(ak, axis=0, keepdims=True)


def _mlstm(mq, mk, mv, mo, ifg, g, batch, seq):
    L = M_CHUNK
    nb = M_BATCH
    seq3 = lambda t: t.reshape(batch, seq, t.shape[-1])
    blk = pl.BlockSpec((nb, L, M_WIDTH), lambda b, c: (b, c, 0))
    blk_t = pl.BlockSpec((nb, M_WIDTH, L), lambda b, c: (b, 0, c))
    out = pl.pallas_call(
        _mlstm_kernel,
        grid=(batch // nb, seq // L),
        in_specs=[blk, blk, blk_t, blk, pl.BlockSpec((nb, L, LANES), lambda b, c: (b, c, 0)),
                  _const_spec((1, M_WIDTH))],
        out_specs=blk,
        out_shape=jax.ShapeDtypeStruct((batch, seq, M_WIDTH), BF16),
        scratch_shapes=[pltpu.VMEM((nb, M_HEADS, M_HEAD_DIM + 2 * SUBLANES, M_HEAD_DIM), F32),
                        pltpu.VMEM((nb, 1, LANES), F32)],
        compiler_params=pltpu.CompilerParams(dimension_semantics=("arbitrary", "arbitrary"),
                                             vmem_limit_bytes=VMEM_LIMIT),
        name="mlstm",
    )(seq3(mq), seq3(mk), mv, seq3(mo), seq3(ifg), g)
    return out.reshape(batch * seq, M_WIDTH)


def _t5_bucket(n):
    max_exact = REL_BUCKETS // 2
    nf = jnp.maximum(n, 1).astype(F32)
    large = max_exact + (jnp.log(nf / max_exact) / math.log(REL_MAX_DIST / max_exact)
                         * (REL_BUCKETS - max_exact)).astype(jnp.int32)
    large = jnp.minimum(large, REL_BUCKETS - 1)
    return jnp.where(n < max_exact, n, large)


def _attn_kernel(rb_ref, lam_ref, q_ref, k_ref, v_ref, g_ref, out_ref,
                 bias_ref, m_ref, l_ref, acc_ref, s_ref, *, lam_init):
    tq = ATT_BLOCK
    tk = tq
    sb = ATT_SUB
    nq = q_ref.shape[0] // tq
    h = pl.program_id(1)

    kk = lax.broadcasted_iota(jnp.int32, (sb, sb), 0)
    qq = lax.broadcasted_iota(jnp.int32, (sb, sb), 1)

    @pl.when((pl.program_id(0) == 0) & (h == 0))
    def _():
        for delta in range(2):
            bucket = _t5_bucket(jnp.maximum(qq + delta * sb - kk, 0))
            for hh in range(DA_HEADS):
                t = jnp.zeros((sb, sb), F32)
                for j in range(REL_BUCKETS):
                    t = jnp.where(bucket == j, rb_ref[j, hh], t)
                bias_ref[hh, delta] = (t - rb_ref[REL_BUCKETS - 1, hh]) * LOG2E

    lv = lam_ref[...]
    lam = (jnp.exp(jnp.sum(lv[0:1] * lv[1:2], axis=1, keepdims=True))
           - jnp.exp(jnp.sum(lv[2:3] * lv[3:4], axis=1, keepdims=True)) + lam_init)

    n_groups = 2 * tq // sb
    d0 = bias_ref[h, 0]
    d1 = bias_ref[h, 1]
    causal = qq >= kk
    ident = lambda s: s
    near = lambda s: jnp.concatenate([s[:sb], s[sb:] + d1], axis=0)
    early = lambda s: jnp.where(causal, s + d0, NEG_BIG)
    late = lambda s: jnp.concatenate([s[:sb] + d1, jnp.where(causal, s[sb:] + d0, NEG_BIG)], axis=0)

    def plan(j, n):
        if j == n - 1:
            return [(sb, early), (tk, late)] * (n_groups // 2)
        if j == n - 2:
            return [(tk, near), (tk, ident)] * (n_groups // 2)
        return [(tk, ident)] * n_groups

    lane = lax.broadcasted_iota(jnp.int32, (tq, DA_V_DIM), 1)
    step = 0
    for qb in range(nq):
        n = qb + 1
        st = qb % 2
        q = q_ref[qb * tq:(qb + 1) * tq, :]
        zero = jnp.zeros_like(q)
        qcat = jnp.concatenate([jnp.where(lane < DA_HEAD_DIM, q, zero),
                                jnp.where(lane >= DA_HEAD_DIM, q, zero)], axis=0)
        m_ref[st] = jnp.full(m_ref.shape[1:], -jnp.inf, F32)
        l_ref[st] = jnp.zeros(l_ref.shape[1:], F32)
        acc_ref[st] = jnp.zeros(acc_ref.shape[1:], F32)

        def scores(j, buf):
            k = k_ref[j * tk:(j + 1) * tk, :]
            for g, (rows, _) in enumerate(plan(j, n)):
                s_ref[buf, g, 0:rows, :] = _dot_nt(k[:rows], qcat[g * sb:(g + 1) * sb])

        def softmax_pv(j, buf):
            v_t = v_ref[:, j * tk:(j + 1) * tk]
            for g, (rows, fix) in enumerate(plan(j, n)):
                cols = slice(g * sb, (g + 1) * sb)
                s = fix(s_ref[buf, g, 0:rows, :])
                m_prev = m_ref[st, :, cols]
                m_new = jnp.maximum(m_prev, jnp.max(s, axis=0, keepdims=True))
                alpha = jnp.exp2(m_prev - m_new)
                p = jnp.exp2(s - m_new)
                l_ref[st, :, cols] = alpha * l_ref[st, :, cols] + jnp.sum(p, axis=0, keepdims=True)
                acc_ref[st, :, cols] = (alpha * acc_ref[st, :, cols]
                                        + _dot(v_t[:, :rows], p.astype(BF16)))
                m_ref[st, :, cols] = m_new

        scores(0, step % 2)
        for j in range(n):
            if j + 1 < n:
                scores(j + 1, (step + 1) % 2)
            softmax_pv(j, step % 2)
            step += 1

        o = acc_ref[st] * (1.0 / l_ref[st])
        out = (o[:, :tq] - lam * o[:, tq:]).T
        out_ref[qb * tq:(qb + 1) * tq, :] = (_rms(out, g_ref[...]) * (1.0 - lam_init)).astype(BF16)


def _attention(rel_bias, lam_vec, dq, dk, dv, g, batch, seq, lam_init):
    T = dq.shape[0]
    tq = ATT_BLOCK
    seq_blk = pl.BlockSpec((seq, DA_V_DIM), lambda b, h: (b, h))
    return pl.pallas_call(
        functools.partial(_attn_kernel, lam_init=lam_init),
        grid=(batch, DA_HEADS),
        in_specs=[pl.BlockSpec(memory_space=pltpu.SMEM),
                  pl.BlockSpec((4, DA_HEAD_DIM), lambda b, h: (0, 0)),
                  seq_blk, seq_blk,
                  pl.BlockSpec((None, DA_V_DIM, seq), lambda b, h: (b, h, 0)),
                  pl.BlockSpec((1, DA_V_DIM), lambda b, h: (0, h))],
        out_specs=seq_blk,
        out_shape=jax.ShapeDtypeStruct((T, DA_WIDTH), BF16),
        scratch_shapes=[pltpu.VMEM((DA_HEADS, 2, ATT_SUB, ATT_SUB), F32),
                        pltpu.VMEM((2, 1, 2 * tq), F32),
                        pltpu.VMEM((2, 1, 2 * tq), F32),
                        pltpu.VMEM((2, DA_V_DIM, 2 * tq), F32),
                        pltpu.VMEM((2, 2 * tq // ATT_SUB, tq, ATT_SUB), F32)],
        compiler_params=pltpu.CompilerParams(dimension_semantics=("arbitrary",) * 2,
                                             vmem_limit_bytes=VMEM_LIMIT),
        name="diffattn",
    )(rel_bias, lam_vec, dq, dk, dv, g)


def _gelu_tanh(x):
    return 0.5 * x * (1.0 + jnp.tanh(math.sqrt(2.0 / math.pi) * (x + 0.044715 * (x * x * x))))


def _ffn_kernel(gm_ref, hd_ref, gates_ref, x_ref, p_ref, wm_ref, wd_ref, wo_ref,
                gf_ref, wuv_ref, wug_ref, cw_ref, cb_ref, wdn_ref, gp_ref, wpg_ref, wp_ref, gl_ref,
                out_ref, carry_ref, slab_ref, acc_ref, wup_s, wdn_s, *, tiles_per_seq, final_norm):
    tm = x_ref.shape[0]
    step = pl.program_id(0)
    i = step - N_FF_CHUNKS

    @pl.when(step < N_FF_CHUNKS)
    def _():
        wup_s[step, 0] = wuv_ref[...].astype(BF16)
        wup_s[step, 1] = wug_ref[...].astype(BF16)
        wdn_s[step] = wdn_ref[...].astype(BF16)

    @pl.when((i >= 0) & (i % tiles_per_seq == 0))
    def _():
        carry_ref[...] = jnp.zeros(carry_ref.shape, F32)

    @pl.when(i >= 0)
    def _():
        _ffn_tile(tm, gm_ref, hd_ref, gates_ref, x_ref, p_ref, wm_ref, wd_ref, wo_ref, gf_ref, cw_ref, cb_ref,
                  gp_ref, wpg_ref, wp_ref, gl_ref, out_ref, carry_ref, slab_ref, acc_ref, wup_s, wdn_s, final_norm)


def _ffn_tile(tm, gm_ref, hd_ref, gates_ref, x_ref, p_ref, wm_ref, wd_ref, wo_ref, gf_ref, cw_ref, cb_ref,
              gp_ref, wpg_ref, wp_ref, gl_ref, out_ref, carry_ref, slab_ref, acc_ref, wup_s, wdn_s, final_norm):
    ya = _dot(gm_ref[...], wm_ref[...])
    yb = _dot(hd_ref[...], wd_ref[...])
    ga = _sigmoid(gates_ref[:, :D_MODEL].astype(F32))
    gb = _sigmoid(gates_ref[:, D_MODEL:].astype(F32))
    x1 = x_ref[...] + _dot((ga * ya + gb * yb).astype(BF16), wo_ref[...])

    hb = _rms(x1, gf_ref[...]).astype(BF16)

    def slab0(c, half):
        return (c % FF_SLAB_SETS) * FF_SLABS + half * (FF_CHUNK // LANES)

    def up(c):
        for half in range(2):
            _slab_store(_dot(hb, wup_s[c, half]), slab_ref, slab0(c, half))

    def conv(c, half):
        return _slab_conv(tm, FF_CHUNK, half * D_FF + c * FF_CHUNK, slab_ref, slab0(c, half),
                          carry_ref, cw_ref, cb_ref)

    for c in range(FF_LOOKAHEAD):
        up(c)
    for c in range(N_FF_CHUNKS):
        if c + FF_LOOKAHEAD < N_FF_CHUNKS:
            up(c + FF_LOOKAHEAD)
        act = (_gelu_tanh(conv(c, 1)) * conv(c, 0)).astype(BF16)
        d = _dot(act, wdn_s[c])
        if c == 0:
            acc_ref[...] = d
        else:
            acc_ref[...] += d

    x2 = x1 + acc_ref[...]
    hg = _rms(x2, gp_ref[...]).astype(BF16)
    gate = _sigmoid(_dot(hg, wpg_ref[...]))
    pe = _dot(p_ref[...].astype(BF16), wp_ref[...])
    x3 = x2 + gate * pe
    out_ref[...] = _rms(x3, gl_ref[...]) if final_norm else x3


def _ffn(gm, hd, gates, x2, p2, wm, wd, wo, gf, wup, cw, cb, wdn, gp, wpg, wp, gl, seq, final_norm):
    T = x2.shape[0]
    tm = TOKEN_TILE
    n_stage = N_FF_CHUNKS
    row = lambda s: (jnp.maximum(s - n_stage, 0), 0)
    stage = lambda s: jnp.minimum(s, n_stage - 1)
    return pl.pallas_call(
        functools.partial(_ffn_kernel, tiles_per_seq=seq // tm, final_norm=final_norm),
        grid=(n_stage + T // tm,),
        in_specs=[pl.BlockSpec((tm, M_WIDTH), row), pl.BlockSpec((tm, DA_WIDTH), row),
                  pl.BlockSpec((tm, 2 * D_MODEL), row), pl.BlockSpec((tm, D_MODEL), row),
                  pl.BlockSpec((tm, PLE_DIM), row),
                  _const_spec((M_WIDTH, D_MODEL)), _const_spec((DA_WIDTH, D_MODEL)),
                  _const_spec((D_MODEL, D_MODEL)), _const_spec((1, D_MODEL)),
                  pl.BlockSpec((D_MODEL, FF_CHUNK), lambda s: (0, stage(s))),
                  pl.BlockSpec((D_MODEL, FF_CHUNK), lambda s: (0, n_stage + stage(s))),
                  _const_spec((FFN_CONV, 2 * D_FF)), _const_spec((1, 2 * D_FF)),
                  pl.BlockSpec((FF_CHUNK, D_MODEL), lambda s: (stage(s), 0)),
                  _const_spec((1, D_MODEL)), _const_spec((D_MODEL, D_MODEL)),
                  _const_spec((PLE_DIM, D_MODEL)), _const_spec((1, D_MODEL))],
        out_specs=pl.BlockSpec((tm, D_MODEL), row),
        out_shape=jax.ShapeDtypeStruct((T, D_MODEL), F32),
        scratch_shapes=[pltpu.VMEM((SUBLANES, 2 * D_FF), F32),
                        pltpu.VMEM((FF_SLAB_SETS * FF_SLABS, SUBLANES + tm, LANES), F32),
                        pltpu.VMEM((tm, D_MODEL), F32),
                        pltpu.VMEM((N_FF_CHUNKS, 2, D_MODEL, FF_CHUNK), BF16),
                        pltpu.VMEM((N_FF_CHUNKS, FF_CHUNK, D_MODEL), BF16)],
        compiler_params=pltpu.CompilerParams(dimension_semantics=("arbitrary",), vmem_limit_bytes=VMEM_LIMIT),
        name="convffn",
    )(gm, hd, gates, x2, p2, wm, wd, wo, gf, wup, wup, cw, cb, wdn, gp, wpg, wp, gl)


def kernel(x, p, rel_bias, norm_mix_g, w_in, b_if, m_conv_w, m_conv_b, m_norm_g, da_lambda, da_norm_g,
           w_br_m, w_br_d, w_out, norm_ffn_g, w_up, ffn_conv_w, ffn_conv_b, w_down, norm_ple_g, w_ple_gate,
           w_ple, norm_final_g):
    batch, seq, _ = x.shape
    depth = w_in.shape[0]
    T = batch * seq
    assert seq % TOKEN_TILE == 0 and seq % M_CHUNK == 0 and seq % ATT_BLOCK == 0 and batch % M_BATCH == 0
    xt = x.reshape(T, D_MODEL)
    row = lambda v: v.reshape(1, -1).astype(F32)

    for l in range(depth):
        w_all = _wprep(jnp.swapaxes(w_in[l].astype(F32), 0, 1))
        bif = jnp.concatenate([b_if[l].astype(F32), jnp.zeros((LANES - N_IF,), F32)]).reshape(1, LANES)

        mq, mk, mv, mo, dq, dk, dv, gates, ifg = _inproj(
            xt, row(norm_mix_g[l]), w_all, m_conv_w[l].astype(F32), row(m_conv_b[l]), bif, seq)

        gm = _mlstm(mq, mk, mv, mo, ifg, row(m_norm_g[l]), batch, seq)

        lam_init = 0.8 - 0.6 * math.exp(-0.3 * l)
        hd = _attention(rel_bias.astype(F32), da_lambda[l].astype(F32), dq, dk, dv, row(da_norm_g[l]),
                        batch, seq, lam_init)

        xt = _ffn(gm, hd, gates, xt, p[l].reshape(T, PLE_DIM),
                  w_br_m[l].astype(BF16), w_br_d[l].astype(BF16), w_out[l].astype(BF16),
                  row(norm_ffn_g[l]), w_up[l].astype(F32),
                  ffn_conv_w[l].astype(F32), row(ffn_conv_b[l]), w_down[l].astype(F32), row(norm_ple_g[l]),
                  w_ple_gate[l].astype(BF16), w_ple[l].astype(BF16), row(norm_final_g), seq,
                  final_norm=(l == depth - 1))

    return xt.reshape(batch, seq, D_MODEL)
```

```python
import functools
import math

import jax
import jax.numpy as jnp
from jax import lax
from jax.experimental import pallas as pl
from jax.experimental.pallas import tpu as pltpu

D_MODEL = 1024
PLE_DIM = 256
M_HEADS = 4
M_HEAD_DIM = 128
M_WIDTH = M_HEADS * M_HEAD_DIM
M_CONV = 4
DA_HEADS = 4
DA_HEAD_DIM = 64
DA_V_DIM = 2 * DA_HEAD_DIM
DA_WIDTH = DA_HEADS * DA_V_DIM
REL_BUCKETS = 32
REL_MAX_DIST = 128
D_FF = 2816
FFN_CONV = 3
EPS = 1e-6
NEG_BIG = -1e30
LOG2E = 1.4426950408889634

LANES = 128
SUBLANES = 8
VMEM_LIMIT = 56 * 1024 * 1024

TOKEN_TILE = 512
M_CHUNK = 256
M_BATCH = 8
ATT_BLOCK = 512
ATT_SUB = ATT_BLOCK // 2
FF_CHUNK = 256
N_FF_CHUNKS = D_FF // FF_CHUNK
FF_SLABS = 2 * FF_CHUNK // LANES
FF_LOOKAHEAD = 3
FF_SLAB_SETS = FF_LOOKAHEAD + 1

F32 = jnp.float32
BF16 = jnp.bfloat16

C_IF = 4 * M_WIDTH
N_IF = 2 * M_HEADS
IN_COLS = C_IF + N_IF + 3 * DA_WIDTH + 2 * D_MODEL
C_TAIL = IN_COLS - N_IF


def _dot(a, b):
    return jnp.dot(a, b, preferred_element_type=F32)


def _dot_nt(a, b):
    return lax.dot_general(a, b, (((1,), (1,)), ((), ())), preferred_element_type=F32)


def _rms(x, g):
    return x * lax.rsqrt(jnp.mean(x * x, -1, keepdims=True) + EPS) * g


def _sigmoid(x):
    return 1.0 / (1.0 + jnp.exp(-x))


def _slab_store(u, slab_ref, slab0):
    tm, n = u.shape
    for j in range(n // LANES):
        slab_ref[slab0 + j, SUBLANES:SUBLANES + tm, :] = u[:, j * LANES:(j + 1) * LANES]


def _slab_conv(tm, n, col0, slab_ref, slab0, carry_ref, cw_ref, cb_ref):
    taps = cw_ref.shape[0]
    outs = []
    for j in range(n // LANES):
        cols = slice(col0 + j * LANES, col0 + (j + 1) * LANES)
        s = slab0 + j
        slab_ref[s, 0:SUBLANES, :] = carry_ref[:, cols]
        carry_ref[:, cols] = slab_ref[s, tm:tm + SUBLANES, :]
        y = cb_ref[:, cols]
        for k in range(taps):
            y = y + cw_ref[taps - 1 - k:taps - k, cols] * slab_ref[s, pl.ds(SUBLANES - k, tm), :]
        outs.append(y)
    return jnp.concatenate(outs, axis=1)


def _cumsum_rows(tril, x):
    hi = x.astype(BF16)
    r1 = x - hi.astype(F32)
    mid = r1.astype(BF16)
    lo = (r1 - mid.astype(F32)).astype(BF16)
    return _dot(tril, hi) + _dot(tril, mid) + _dot(tril, lo)


def _const_spec(shape):
    nd = len(shape)
    return pl.BlockSpec(shape, lambda *_: (0,) * nd, pipeline_mode=pl.Buffered(1))


W_BLOCK = 512
N_W_BLOCKS = C_TAIL // W_BLOCK + 1


def _inproj_kernel(x_ref, g_ref, wt_ref, cw_ref, cb_ref, bif_ref,
                   mq_ref, mk_ref, mv_ref, mo_ref, dq_ref, dk_ref, dv_ref, gates_ref, ifg_ref,
                   carry_ref, slab_ref, w_s, *, tiles_per_seq):
    tm = x_ref.shape[0]
    step = pl.program_id(0)
    i = step - N_W_BLOCKS

    @pl.when(step < N_W_BLOCKS)
    def _():
        row = lax.broadcasted_iota(jnp.int32, wt_ref.shape, 0)
        keep = (step < N_W_BLOCKS - 1) | (row < N_IF)
        w_s[step] = jnp.where(keep, wt_ref[...], 0.0).T.astype(BF16)

    @pl.when((i >= 0) & (i % tiles_per_seq == 0))
    def _():
        carry_ref[...] = jnp.zeros(carry_ref.shape, F32)

    @pl.when(i >= 0)
    def _():
        _inproj_tile(tm, x_ref, g_ref, cw_ref, cb_ref, bif_ref, mq_ref, mk_ref, mv_ref, mo_ref, dq_ref, dk_ref,
                     dv_ref, gates_ref, ifg_ref, carry_ref, slab_ref, w_s)


def _inproj_tile(tm, x_ref, g_ref, cw_ref, cb_ref, bif_ref, mq_ref, mk_ref, mv_ref, mo_ref, dq_ref, dk_ref,
                 dv_ref, gates_ref, ifg_ref, carry_ref, slab_ref, w_s):
    hb = _rms(x_ref[...], g_ref[...]).astype(BF16)

    def conv_silu(col0):
        y = _slab_conv(tm, M_WIDTH, col0, slab_ref, col0 // LANES, carry_ref, cw_ref, cb_ref)
        return y * _sigmoid(y)

    mv_ref[...] = _dot(hb, w_s[2]).astype(BF16).T
    mo_ref[...] = _dot(hb, w_s[3]).astype(BF16)
    _slab_store(_dot(hb, w_s[0]), slab_ref, 0)
    dq_ref[...] = (_dot(hb, w_s[4]) * (LOG2E * DA_HEAD_DIM ** -0.5)).astype(BF16)
    dk_ref[...] = _dot(hb, w_s[5]).astype(BF16)
    dv_ref[...] = _dot(hb, w_s[6]).astype(BF16).T
    _slab_store(_dot(hb, w_s[1]), slab_ref, M_WIDTH // LANES)
    for j in range(4):
        gates_ref[:, j * W_BLOCK:(j + 1) * W_BLOCK] = _dot(hb, w_s[7 + j]).astype(BF16)

    ifg_ref[...] = _dot(hb, w_s[N_W_BLOCKS - 1, :, 0:LANES]) + bif_ref[...]

    mq_ref[...] = conv_silu(0).astype(BF16)
    mk_ref[...] = (conv_silu(M_WIDTH) * (M_HEAD_DIM ** -0.5)).astype(BF16)


def _inproj(x2, g, w_t, cw, cb, bif, seq):
    assert M_WIDTH == DA_WIDTH == W_BLOCK and C_IF % W_BLOCK == 0
    T = x2.shape[0]
    tm = TOKEN_TILE
    tps = seq // tm
    tile = lambda s: jnp.maximum(s - N_W_BLOCKS, 0)
    row = lambda s: (tile(s), 0)

    def src(s):
        j = jnp.minimum(s, N_W_BLOCKS - 1)
        shifted = j * W_BLOCK + jnp.where(j >= C_IF // W_BLOCK, N_IF, 0)
        return pl.multiple_of(jnp.where(j < N_W_BLOCKS - 1, shifted, C_IF), SUBLANES), 0

    bf = lambda n: jax.ShapeDtypeStruct((T, n), BF16)
    bf_t = lambda n: jax.ShapeDtypeStruct((T // seq, n, seq), BF16)
    blk = pl.BlockSpec((tm, M_WIDTH), row)
    blk_t = pl.BlockSpec((None, M_WIDTH, tm), lambda s: (tile(s) // tps, 0, tile(s) % tps))
    out_shapes = [bf(M_WIDTH), bf(M_WIDTH), bf_t(M_WIDTH), bf(M_WIDTH), bf(DA_WIDTH), bf(DA_WIDTH), bf_t(DA_WIDTH),
                  bf(2 * D_MODEL), jax.ShapeDtypeStruct((T, LANES), F32)]
    out_specs = [blk, blk, blk_t, blk, blk, blk, blk_t,
                 pl.BlockSpec((tm, 2 * D_MODEL), row), pl.BlockSpec((tm, LANES), row)]
    return pl.pallas_call(
        functools.partial(_inproj_kernel, tiles_per_seq=tps),
        grid=(N_W_BLOCKS + T // tm,),
        in_specs=[pl.BlockSpec((tm, D_MODEL), row), _const_spec((1, D_MODEL)),
                  pl.BlockSpec((pl.Element(W_BLOCK), pl.Element(D_MODEL)), src),
                  _const_spec((M_CONV, 2 * M_WIDTH)), _const_spec((1, 2 * M_WIDTH)), _const_spec((1, LANES))],
        out_specs=out_specs,
        out_shape=out_shapes,
        scratch_shapes=[pltpu.VMEM((SUBLANES, 2 * M_WIDTH), F32),
                        pltpu.VMEM((2 * M_WIDTH // LANES, SUBLANES + tm, LANES), F32),
                        pltpu.VMEM((N_W_BLOCKS, D_MODEL, W_BLOCK), BF16)],
        compiler_params=pltpu.CompilerParams(dimension_semantics=("arbitrary",), vmem_limit_bytes=VMEM_LIMIT),
        name="inproj",
    )(x2, g, w_t, cw, cb, bif)


def _mlstm_kernel(mq_ref, mk_ref, mv_ref, mo_ref, ifg_ref, g_ref, out_ref, ct_ref, m_ref):
    L = mq_ref.shape[1]
    d = M_HEAD_DIM

    @pl.when(pl.program_id(1) == 0)
    def _():
        ct_ref[...] = jnp.zeros(ct_ref.shape, F32)
        m_ref[...] = jnp.zeros(m_ref.shape, F32)

    kidx = lax.broadcasted_iota(jnp.int32, (L, L), 0)
    qidx = lax.broadcasted_iota(jnp.int32, (L, L), 1)
    tril = jnp.where(kidx >= qidx, 1.0, 0.0).astype(BF16)
    causal = kidx <= qidx

    for s in range(mq_ref.shape[0]):
        ig = ifg_ref[s]
        fg = pltpu.roll(ig, LANES - M_HEADS, axis=1)
        logf = jnp.minimum(fg, 0.0) - jnp.log1p(jnp.exp(-jnp.abs(fg)))
        b = _cumsum_rows(tril, logf)
        b_tot = b[L - 1:L, :]
        u = ig - b
        w_end = b_tot + u
        m_loc = jnp.max(w_end, axis=0, keepdims=True)
        a = jnp.exp(w_end - m_loc)
        m_prev = m_ref[s]
        m_new = jnp.maximum(b_tot + m_prev, m_loc)
        s_old = jnp.exp(b_tot + m_prev - m_new)
        s_loc = jnp.exp(m_loc - m_new)
        li_t = (b + m_prev).T
        b_t = b.T
        m_ref[s] = m_new

        for h in range(M_HEADS):
            hs = slice(h * d, (h + 1) * d)
            q = mq_ref[s, :, hs]
            k = mk_ref[s, :, hs]
            v_t = mv_ref[s, hs, :]
            ct_prev = ct_ref[s, h]
            s_t = _dot_nt(k, q)
            inter = _dot_nt(ct_prev.astype(BF16), q)
            log_d = jnp.where(causal, u[:, h:h + 1] + b_t[h:h + 1, :], -jnp.inf)
            li = li_t[h:h + 1, :]
            m_t = jnp.maximum(li, jnp.max(log_d, axis=0, keepdims=True))
            p_t = s_t * jnp.exp(log_d - m_t)
            s_inter = jnp.exp(li - m_t)
            num = s_inter * inter[:d] + _dot(v_t, p_t.astype(BF16))
            den = s_inter * inter[d:d + 1] + jnp.sum(p_t, axis=0, keepdims=True)
            hh = num / jnp.maximum(jnp.abs(den), jnp.exp(-m_t))
            y = (hh * lax.rsqrt(jnp.mean(hh * hh, axis=0, keepdims=True) + EPS)).T
            out_ref[s, :, hs] = (_sigmoid(mo_ref[s, :, hs].astype(F32)) * (y * g_ref[:, hs])).astype(BF16)

            ak = a[:, h:h + 1] * k.astype(F32)
            so = s_old[:, h:h + 1]
            sl = s_loc[:, h:h + 1]
            ct_ref[s, h, 0:d, :] = so * ct_prev[:d] + sl * _dot(v_t, ak.astype(BF16))
            ct_ref[s, h, d:d + 1, :] = so * ct_prev[d:d + 1] + sl * jnp.sum(ak, axis=0, keepdims=True)


def _mlstm(mq, mk, mv, mo, ifg, g, batch, seq):
    L = M_CHUNK
    nb = M_BATCH
    seq3 = lambda t: t.reshape(batch, seq, t.shape[-1])
    blk = pl.BlockSpec((nb, L, M_WIDTH), lambda b, c: (b, c, 0))
    blk_t = pl.BlockSpec((nb, M_WIDTH, L), lambda b, c: (b, 0, c))
    out = pl.pallas_call(
        _mlstm_kernel,
        grid=(batch // nb, seq // L),
        in_specs=[blk, blk, blk_t, blk, pl.BlockSpec((nb, L, LANES), lambda b, c: (b, c, 0)),
                  _const_spec((1, M_WIDTH))],
        out_specs=blk,
        out_shape=jax.ShapeDtypeStruct((batch, seq, M_WIDTH), BF16),
        scratch_shapes=[pltpu.VMEM((nb, M_HEADS, M_HEAD_DIM + 2 * SUBLANES, M_HEAD_DIM), F32),
                        pltpu.VMEM((nb, 1, LANES), F32)],
        compiler_params=pltpu.CompilerParams(dimension_semantics=("arbitrary", "arbitrary"),
                                             vmem_limit_bytes=VMEM_LIMIT),
        name="mlstm",
    )(seq3(mq), seq3(mk), mv, seq3(mo), seq3(ifg), g)
    return out.reshape(batch * seq, M_WIDTH)


def _t5_bucket(n):
    max_exact = REL_BUCKETS // 2
    nf = jnp.maximum(n, 1).astype(F32)
    large = max_exact + (jnp.log(nf / max_exact) / math.log(REL_MAX_DIST / max_exact)
                         * (REL_BUCKETS - max_exact)).astype(jnp.int32)
    large = jnp.minimum(large, REL_BUCKETS - 1)
    return jnp.where(n < max_exact, n, large)


def _attn_kernel(rb_ref, lam_ref, q_ref, k_ref, v_ref, g_ref, out_ref,
                 bias_ref, m_ref, l_ref, acc_ref, s_ref, *, lam_init):
    tq = ATT_BLOCK
    tk = tq
    sb = ATT_SUB
    nq = q_ref.shape[0] // tq
    h = pl.program_id(1)

    kk = lax.broadcasted_iota(jnp.int32, (sb, sb), 0)
    qq = lax.broadcasted_iota(jnp.int32, (sb, sb), 1)

    @pl.when((pl.program_id(0) == 0) & (h == 0))
    def _():
        for delta in range(2):
            bucket = _t5_bucket(jnp.maximum(qq + delta * sb - kk, 0))
            for hh in range(DA_HEADS):
                t = jnp.zeros((sb, sb), F32)
                for j in range(REL_BUCKETS):
                    t = jnp.where(bucket == j, rb_ref[j, hh], t)
                bias_ref[hh, delta] = (t - rb_ref[REL_BUCKETS - 1, hh]) * LOG2E

    lv = lam_ref[...]
    lam = (jnp.exp(jnp.sum(lv[0:1] * lv[1:2], axis=1, keepdims=True))
           - jnp.exp(jnp.sum(lv[2:3] * lv[3:4], axis=1, keepdims=True)) + lam_init)

    n_groups = 2 * tq // sb
    d0 = bias_ref[h, 0]
    d1 = bias_ref[h, 1]
    causal = qq >= kk
    ident = lambda s: s
    near = lambda s: jnp.concatenate([s[:sb], s[sb:] + d1], axis=0)
    early = lambda s: jnp.where(causal, s + d0, NEG_BIG)
    late = lambda s: jnp.concatenate([s[:sb] + d1, jnp.where(causal, s[sb:] + d0, NEG_BIG)], axis=0)

    def plan(j, n):
        if j == n - 1:
            return [(sb, early), (tk, late)] * (n_groups // 2)
        if j == n - 2:
            return [(tk, near), (tk, ident)] * (n_groups // 2)
        return [(tk, ident)] * n_groups

    lane = lax.broadcasted_iota(jnp.int32, (tq, DA_V_DIM), 1)
    step = 0
    for qb in range(nq):
        n = qb + 1
        st = qb % 2
        q = q_ref[qb * tq:(qb + 1) * tq, :]
        zero = jnp.zeros_like(q)
        qcat = jnp.concatenate([jnp.where(lane < DA_HEAD_DIM, q, zero),
                                jnp.where(lane >= DA_HEAD_DIM, q, zero)], axis=0)
        m_ref[st] = jnp.full(m_ref.shape[1:], -jnp.inf, F32)
        l_ref[st] = jnp.zeros(l_ref.shape[1:], F32)
        acc_ref[st] = jnp.zeros(acc_ref.shape[1:], F32)

        def scores(j, buf):
            k = k_ref[j * tk:(j + 1) * tk, :]
            for g, (rows, _) in enumerate(plan(j, n)):
                s_ref[buf, g, 0:rows, :] = _dot_nt(k[:rows], qcat[g * sb:(g + 1) * sb])

        def softmax_pv(j, buf):
            v_t = v_ref[:, j * tk:(j + 1) * tk]
            for g, (rows, fix) in enumerate(plan(j, n)):
                cols = slice(g * sb, (g + 1) * sb)
                s = fix(s_ref[buf, g, 0:rows, :])
                m_prev = m_ref[st, :, cols]
                m_new = jnp.maximum(m_prev, jnp.max(s, axis=0, keepdims=True))
                alpha = jnp.exp2(m_prev - m_new)
                p = jnp.exp2(s - m_new)
                l_ref[st, :, cols] = alpha * l_ref[st, :, cols] + jnp.sum(p, axis=0, keepdims=True)
                acc_ref[st, :, cols] = (alpha * acc_ref[st, :, cols]
                                        + _dot(v_t[:, :rows], p.astype(BF16)))
                m_ref[st, :, cols] = m_new

        scores(0, step % 2)
        for j in range(n):
            if j + 1 < n:
                scores(j + 1, (step + 1) % 2)
            softmax_pv(j, step % 2)
            step += 1

        o = acc_ref[st] * (1.0 / l_ref[st])
        out = (o[:, :tq] - lam * o[:, tq:]).T
        out_ref[qb * tq:(qb + 1) * tq, :] = (_rms(out, g_ref[...]) * (1.0 - lam_init)).astype(BF16)


def _attention(rel_bias, lam_vec, dq, dk, dv, g, batch, seq, lam_init):
    T = dq.shape[0]
    tq = ATT_BLOCK
    seq_blk = pl.BlockSpec((seq, DA_V_DIM), lambda b, h: (b, h))
    return pl.pallas_call(
        functools.partial(_attn_kernel, lam_init=lam_init),
        grid=(batch, DA_HEADS),
        in_specs=[pl.BlockSpec(memory_space=pltpu.SMEM),
                  pl.BlockSpec((4, DA_HEAD_DIM), lambda b, h: (0, 0)),
                  seq_blk, seq_blk,
                  pl.BlockSpec((None, DA_V_DIM, seq), lambda b, h: (b, h, 0)),
                  pl.BlockSpec((1, DA_V_DIM), lambda b, h: (0, h))],
        out_specs=seq_blk,
        out_shape=jax.ShapeDtypeStruct((T, DA_WIDTH), BF16),
        scratch_shapes=[pltpu.VMEM((DA_HEADS, 2, ATT_SUB, ATT_SUB), F32),
                        pltpu.VMEM((2, 1, 2 * tq), F32),
                        pltpu.VMEM((2, 1, 2 * tq), F32),
                        pltpu.VMEM((2, DA_V_DIM, 2 * tq), F32),
                        pltpu.VMEM((2, 2 * tq // ATT_SUB, tq, ATT_SUB), F32)],
        compiler_params=pltpu.CompilerParams(dimension_semantics=("arbitrary",) * 2,
                                             vmem_limit_bytes=VMEM_LIMIT),
        name="diffattn",
    )(rel_bias, lam_vec, dq, dk, dv, g)


def _gelu_tanh(x):
    return 0.5 * x * (1.0 + jnp.tanh(math.sqrt(2.0 / math.pi) * (x + 0.044715 * (x * x * x))))


def _ffn_kernel(gm_ref, hd_ref, gates_ref, x_ref, p_ref, wm_ref, wd_ref, wo_ref,
                gf_ref, wuv_ref, wug_ref, cw_ref, cb_ref, wdn_ref, gp_ref, wpg_ref, wp_ref, gl_ref,
                out_ref, carry_ref, slab_ref, acc_ref, wup_s, wdn_s, *, tiles_per_seq, final_norm):
    tm = x_ref.shape[0]
    step = pl.program_id(0)
    i = step - N_FF_CHUNKS

    @pl.when(step < N_FF_CHUNKS)
    def _():
        wup_s[step, 0] = wuv_ref[...].astype(BF16)
        wup_s[step, 1] = wug_ref[...].astype(BF16)
        wdn_s[step] = wdn_ref[...].astype(BF16)

    @pl.when((i >= 0) & (i % tiles_per_seq == 0))
    def _():
        carry_ref[...] = jnp.zeros(carry_ref.shape, F32)

    @pl.when(i >= 0)
    def _():
        _ffn_tile(tm, gm_ref, hd_ref, gates_ref, x_ref, p_ref, wm_ref, wd_ref, wo_ref, gf_ref, cw_ref, cb_ref,
                  gp_ref, wpg_ref, wp_ref, gl_ref, out_ref, carry_ref, slab_ref, acc_ref, wup_s, wdn_s, final_norm)


def _ffn_tile(tm, gm_ref, hd_ref, gates_ref, x_ref, p_ref, wm_ref, wd_ref, wo_ref, gf_ref, cw_ref, cb_ref,
              gp_ref, wpg_ref, wp_ref, gl_ref, out_ref, carry_ref, slab_ref, acc_ref, wup_s, wdn_s, final_norm):
    ya = _dot(gm_ref[...], wm_ref[...])
    yb = _dot(hd_ref[...], wd_ref[...])
    ga = _sigmoid(gates_ref[:, :D_MODEL].astype(F32))
    gb = _sigmoid(gates_ref[:, D_MODEL:].astype(F32))
    x1 = x_ref[...] + _dot((ga * ya + gb * yb).astype(BF16), wo_ref[...])

    hb = _rms(x1, gf_ref[...]).astype(BF16)

    def slab0(c, half):
        return (c % FF_SLAB_SETS) * FF_SLABS + half * (FF_CHUNK // LANES)

    def up(c):
        for half in range(2):
            _slab_store(_dot(hb, wup_s[c, half]), slab_ref, slab0(c, half))

    def conv(c, half):
        return _slab_conv(tm, FF_CHUNK, half * D_FF + c * FF_CHUNK, slab_ref, slab0(c, half),
                          carry_ref, cw_ref, cb_ref)

    for c in range(FF_LOOKAHEAD):
        up(c)
    for c in range(N_FF_CHUNKS):
        if c + FF_LOOKAHEAD < N_FF_CHUNKS:
            up(c + FF_LOOKAHEAD)
        act = (_gelu_tanh(conv(c, 1)) * conv(c, 0)).astype(BF16)
        d = _dot(act, wdn_s[c])
        if c == 0:
            acc_ref[...] = d
        else:
            acc_ref[...] += d

    x2 = x1 + acc_ref[...]
    hg = _rms(x2, gp_ref[...]).astype(BF16)
    gate = _sigmoid(_dot(hg, wpg_ref[...]))
    pe = _dot(p_ref[...].astype(BF16), wp_ref[...])
    x3 = x2 + gate * pe
    out_ref[...] = _rms(x3, gl_ref[...]) if final_norm else x3


def _ffn(gm, hd, gates, x2, p2, wm, wd, wo, gf, wup, cw, cb, wdn, gp, wpg, wp, gl, seq, final_norm):
    T = x2.shape[0]
    tm = TOKEN_TILE
    n_stage = N_FF_CHUNKS
    row = lambda s: (jnp.maximum(s - n_stage, 0), 0)
    stage = lambda s: jnp.minimum(s, n_stage - 1)
    return pl.pallas_call(
        functools.partial(_ffn_kernel, tiles_per_seq=seq // tm, final_norm=final_norm),
        grid=(n_stage + T // tm,),
        in_specs=[pl.BlockSpec((tm, M_WIDTH), row), pl.BlockSpec((tm, DA_WIDTH), row),
                  pl.BlockSpec((tm, 2 * D_MODEL), row), pl.BlockSpec((tm, D_MODEL), row),
                  pl.BlockSpec((tm, PLE_DIM), row),
                  _const_spec((M_WIDTH, D_MODEL)), _const_spec((DA_WIDTH, D_MODEL)),
                  _const_spec((D_MODEL, D_MODEL)), _const_spec((1, D_MODEL)),
                  pl.BlockSpec((D_MODEL, FF_CHUNK), lambda s: (0, stage(s))),
                  pl.BlockSpec((D_MODEL, FF_CHUNK), lambda s: (0, n_stage + stage(s))),
                  _const_spec((FFN_CONV, 2 * D_FF)), _const_spec((1, 2 * D_FF)),
                  pl.BlockSpec((FF_CHUNK, D_MODEL), lambda s: (stage(s), 0)),
                  _const_spec((1, D_MODEL)), _const_spec((D_MODEL, D_MODEL)),
                  _const_spec((PLE_DIM, D_MODEL)), _const_spec((1, D_MODEL))],
        out_specs=pl.BlockSpec((tm, D_MODEL), row),
        out_shape=jax.ShapeDtypeStruct((T, D_MODEL), F32),
        scratch_shapes=[pltpu.VMEM((SUBLANES, 2 * D_FF), F32),
                        pltpu.VMEM((FF_SLAB_SETS * FF_SLABS, SUBLANES + tm, LANES), F32),
                        pltpu.VMEM((tm, D_MODEL), F32),
                        pltpu.VMEM((N_FF_CHUNKS, 2, D_MODEL, FF_CHUNK), BF16),
                        pltpu.VMEM((N_FF_CHUNKS, FF_CHUNK, D_MODEL), BF16)],
        compiler_params=pltpu.CompilerParams(dimension_semantics=("arbitrary",), vmem_limit_bytes=VMEM_LIMIT),
        name="convffn",
    )(gm, hd, gates, x2, p2, wm, wd, wo, gf, wup, wup, cw, cb, wdn, gp, wpg, wp, gl)


def kernel(x, p, rel_bias, norm_mix_g, w_in, b_if, m_conv_w, m_conv_b, m_norm_g, da_lambda, da_norm_g,
           w_br_m, w_br_d, w_out, norm_ffn_g, w_up, ffn_conv_w, ffn_conv_b, w_down, norm_ple_g, w_ple_gate,
           w_ple, norm_final_g):
    batch, seq, _ = x.shape
    depth = w_in.shape[0]
    T = batch * seq
    assert seq % TOKEN_TILE == 0 and seq % M_CHUNK == 0 and seq % ATT_BLOCK == 0 and batch % M_BATCH == 0
    xt = x.reshape(T, D_MODEL)
    row = lambda v: v.reshape(1, -1).astype(F32)

    for l in range(depth):
        w_t = jnp.swapaxes(w_in[l].astype(F32), 0, 1)
        bif = jnp.concatenate([b_if[l].astype(F32), jnp.zeros((LANES - N_IF,), F32)]).reshape(1, LANES)

        mq, mk, mv, mo, dq, dk, dv, gates, ifg = _inproj(
            xt, row(norm_mix_g[l]), w_t, m_conv_w[l].astype(F32), row(m_conv_b[l]), bif, seq)

        gm = _mlstm(mq, mk, mv, mo, ifg, row(m_norm_g[l]), batch, seq)

        lam_init = 0.8 - 0.6 * math.exp(-0.3 * l)
        hd = _attention(rel_bias.astype(F32), da_lambda[l].astype(F32), dq, dk, dv, row(da_norm_g[l]),
                        batch, seq, lam_init)

        xt = _ffn(gm, hd, gates, xt, p[l].reshape(T, PLE_DIM),
                  w_br_m[l].astype(BF16), w_br_d[l].astype(BF16), w_out[l].astype(BF16),
                  row(norm_ffn_g[l]), w_up[l].astype(F32),
                  ffn_conv_w[l].astype(F32), row(ffn_conv_b[l]), w_down[l].astype(F32), row(norm_ple_g[l]),
                  w_ple_gate[l].astype(BF16), w_ple[l].astype(BF16), row(norm_final_g), seq,
                  final_norm=(l == depth - 1))

    return xt.reshape(batch, seq, D_MODEL)
```

```python
import functools
import math

import jax
import jax.numpy as jnp
from jax import lax
from jax.experimental import pallas as pl
from jax.experimental.pallas import tpu as pltpu

D_MODEL = 1024
PLE_DIM = 256
M_HEADS = 4
M_HEAD_DIM = 128
M_WIDTH = M_HEADS * M_HEAD_DIM
M_CONV = 4
DA_HEADS = 4
DA_HEAD_DIM = 64
DA_V_DIM = 2 * DA_HEAD_DIM
DA_WIDTH = DA_HEADS * DA_V_DIM
REL_BUCKETS = 32
REL_MAX_DIST = 128
D_FF = 2816
FFN_CONV = 3
EPS = 1e-6
NEG_BIG = -1e30
LOG2E = 1.4426950408889634

LANES = 128
SUBLANES = 8
VMEM_LIMIT = 56 * 1024 * 1024

TOKEN_TILE = 512
M_CHUNK = 256
M_BATCH = 8
ATT_BLOCK = 512
ATT_SUB = ATT_BLOCK // 2
FF_CHUNK = 256
N_FF_CHUNKS = D_FF // FF_CHUNK
FF_SLABS = 2 * FF_CHUNK // LANES
FF_LOOKAHEAD = 3
FF_SLAB_SETS = FF_LOOKAHEAD + 1
FF_DOWN_GROUP = 2

F32 = jnp.float32
BF16 = jnp.bfloat16

C_IF = 4 * M_WIDTH
N_IF = 2 * M_HEADS
IN_COLS = C_IF + N_IF + 3 * DA_WIDTH + 2 * D_MODEL
C_TAIL = IN_COLS - N_IF


def _dot(a, b):
    return jnp.dot(a, b, preferred_element_type=F32)


def _dot_nt(a, b):
    return lax.dot_general(a, b, (((1,), (1,)), ((), ())), preferred_element_type=F32)


def _rms(x, g):
    return x * lax.rsqrt(jnp.mean(x * x, -1, keepdims=True) + EPS) * g


def _sigmoid(x):
    return 1.0 / (1.0 + jnp.exp(-x))


def _slab_store(u, slab_ref, slab0):
    tm, n = u.shape
    for j in range(n // LANES):
        slab_ref[slab0 + j, SUBLANES:SUBLANES + tm, :] = u[:, j * LANES:(j + 1) * LANES]


def _slab_conv(tm, n, col0, slab_ref, slab0, carry_ref, cw_ref, cb_ref):
    taps = cw_ref.shape[0]
    outs = []
    for j in range(n // LANES):
        cols = slice(col0 + j * LANES, col0 + (j + 1) * LANES)
        s = slab0 + j
        slab_ref[s, 0:SUBLANES, :] = carry_ref[:, cols]
        carry_ref[:, cols] = slab_ref[s, tm:tm + SUBLANES, :]
        y = cb_ref[:, cols]
        for k in range(taps):
            y = y + cw_ref[taps - 1 - k:taps - k, cols] * slab_ref[s, pl.ds(SUBLANES - k, tm), :]
        outs.append(y)
    return jnp.concatenate(outs, axis=1)


def _cumsum_rows(tril, x):
    hi = x.astype(BF16)
    r1 = x - hi.astype(F32)
    mid = r1.astype(BF16)
    lo = (r1 - mid.astype(F32)).astype(BF16)
    return _dot(tril, hi) + _dot(tril, mid) + _dot(tril, lo)


def _const_spec(shape):
    nd = len(shape)
    return pl.BlockSpec(shape, lambda *_: (0,) * nd, pipeline_mode=pl.Buffered(1))


W_BLOCK = 512
N_W_BLOCKS = C_TAIL // W_BLOCK + 1


def _inproj_kernel(x_ref, g_ref, wt_ref, cw_ref, cb_ref, bif_ref,
                   mq_ref, mk_ref, mv_ref, mo_ref, dq_ref, dk_ref, dv_ref, gates_ref, ifg_ref,
                   carry_ref, slab_ref, w_s, *, tiles_per_seq):
    tm = x_ref.shape[0]
    step = pl.program_id(0)
    i = step - N_W_BLOCKS

    @pl.when(step < N_W_BLOCKS)
    def _():
        row = lax.broadcasted_iota(jnp.int32, wt_ref.shape, 0)
        keep = (step < N_W_BLOCKS - 1) | (row < N_IF)
        w_s[step] = jnp.where(keep, wt_ref[...], 0.0).T.astype(BF16)

    @pl.when((i >= 0) & (i % tiles_per_seq == 0))
    def _():
        carry_ref[...] = jnp.zeros(carry_ref.shape, F32)

    @pl.when(i >= 0)
    def _():
        _inproj_tile(tm, x_ref, g_ref, cw_ref, cb_ref, bif_ref, mq_ref, mk_ref, mv_ref, mo_ref, dq_ref, dk_ref,
                     dv_ref, gates_ref, ifg_ref, carry_ref, slab_ref, w_s)


def _inproj_tile(tm, x_ref, g_ref, cw_ref, cb_ref, bif_ref, mq_ref, mk_ref, mv_ref, mo_ref, dq_ref, dk_ref,
                 dv_ref, gates_ref, ifg_ref, carry_ref, slab_ref, w_s):
    hb = _rms(x_ref[...], g_ref[...]).astype(BF16)

    def conv_silu(col0):
        y = _slab_conv(tm, M_WIDTH, col0, slab_ref, col0 // LANES, carry_ref, cw_ref, cb_ref)
        return y * _sigmoid(y)

    mv_ref[...] = _dot(hb, w_s[2]).astype(BF16).T
    mo_ref[...] = _dot(hb, w_s[3]).astype(BF16)
    _slab_store(_dot(hb, w_s[0]), slab_ref, 0)
    dq_ref[...] = (_dot(hb, w_s[4]) * (LOG2E * DA_HEAD_DIM ** -0.5)).astype(BF16)
    dk_ref[...] = _dot(hb, w_s[5]).astype(BF16)
    dv_ref[...] = _dot(hb, w_s[6]).astype(BF16).T
    _slab_store(_dot(hb, w_s[1]), slab_ref, M_WIDTH // LANES)
    for j in range(4):
        gates_ref[:, j * W_BLOCK:(j + 1) * W_BLOCK] = _dot(hb, w_s[7 + j]).astype(BF16)

    ifg_ref[...] = _dot(hb, w_s[N_W_BLOCKS - 1, :, 0:LANES]) + bif_ref[...]

    mq_ref[...] = conv_silu(0).astype(BF16)
    mk_ref[...] = (conv_silu(M_WIDTH) * (M_HEAD_DIM ** -0.5)).astype(BF16)


def _inproj(x2, g, w_t, cw, cb, bif, seq):
    assert M_WIDTH == DA_WIDTH == W_BLOCK and C_IF % W_BLOCK == 0
    T = x2.shape[0]
    tm = TOKEN_TILE
    tps = seq // tm
    tile = lambda s: jnp.maximum(s - N_W_BLOCKS, 0)
    row = lambda s: (tile(s), 0)

    def src(s):
        j = jnp.minimum(s, N_W_BLOCKS - 1)
        shifted = j * W_BLOCK + jnp.where(j >= C_IF // W_BLOCK, N_IF, 0)
        return pl.multiple_of(jnp.where(j < N_W_BLOCKS - 1, shifted, C_IF), SUBLANES), 0

    bf = lambda n: jax.ShapeDtypeStruct((T, n), BF16)
    bf_t = lambda n: jax.ShapeDtypeStruct((T // seq, n, seq), BF16)
    blk = pl.BlockSpec((tm, M_WIDTH), row)
    blk_t = pl.BlockSpec((None, M_WIDTH, tm), lambda s: (tile(s) // tps, 0, tile(s) % tps))
    out_shapes = [bf(M_WIDTH), bf(M_WIDTH), bf_t(M_WIDTH), bf(M_WIDTH), bf(DA_WIDTH), bf(DA_WIDTH), bf_t(DA_WIDTH),
                  bf(2 * D_MODEL), jax.ShapeDtypeStruct((T, LANES), F32)]
    out_specs = [blk, blk, blk_t, blk, blk, blk, blk_t,
                 pl.BlockSpec((tm, 2 * D_MODEL), row), pl.BlockSpec((tm, LANES), row)]
    return pl.pallas_call(
        functools.partial(_inproj_kernel, tiles_per_seq=tps),
        grid=(N_W_BLOCKS + T // tm,),
        in_specs=[pl.BlockSpec((tm, D_MODEL), row), _const_spec((1, D_MODEL)),
                  pl.BlockSpec((pl.Element(W_BLOCK), pl.Element(D_MODEL)), src),
                  _const_spec((M_CONV, 2 * M_WIDTH)), _const_spec((1, 2 * M_WIDTH)), _const_spec((1, LANES))],
        out_specs=out_specs,
        out_shape=out_shapes,
        scratch_shapes=[pltpu.VMEM((SUBLANES, 2 * M_WIDTH), F32),
                        pltpu.VMEM((2 * M_WIDTH // LANES, SUBLANES + tm, LANES), F32),
                        pltpu.VMEM((N_W_BLOCKS, D_MODEL, W_BLOCK), BF16)],
        compiler_params=pltpu.CompilerParams(dimension_semantics=("arbitrary",), vmem_limit_bytes=VMEM_LIMIT),
        name="inproj",
    )(x2, g, w_t, cw, cb, bif)


def _mlstm_kernel(mq_ref, mk_ref, mv_ref, mo_ref, ifg_ref, g_ref, out_ref, ct_ref, m_ref):
    L = mq_ref.shape[1]
    d = M_HEAD_DIM

    @pl.when(pl.program_id(1) == 0)
    def _():
        ct_ref[...] = jnp.zeros(ct_ref.shape, F32)
        m_ref[...] = jnp.zeros(m_ref.shape, F32)

    kidx = lax.broadcasted_iota(jnp.int32, (L, L), 0)
    qidx = lax.broadcasted_iota(jnp.int32, (L, L), 1)
    tril = jnp.where(kidx >= qidx, 1.0, 0.0).astype(BF16)
    causal = kidx <= qidx

    for s in range(mq_ref.shape[0]):
        ig = ifg_ref[s]
        fg = pltpu.roll(ig, LANES - M_HEADS, axis=1)
        logf = jnp.minimum(fg, 0.0) - jnp.log1p(jnp.exp(-jnp.abs(fg)))
        b = _cumsum_rows(tril, logf)
        b_tot = b[L - 1:L, :]
        u = ig - b
        w_end = b_tot + u
        m_loc = jnp.max(w_end, axis=0, keepdims=True)
        a = jnp.exp(w_end - m_loc)
        m_prev = m_ref[s]
        m_new = jnp.maximum(b_tot + m_prev, m_loc)
        s_old = jnp.exp(b_tot + m_prev - m_new)
        s_loc = jnp.exp(m_loc - m_new)
        li_t = (b + m_prev).T
        b_t = b.T
        m_ref[s] = m_new

        for h in range(M_HEADS):
            hs = slice(h * d, (h + 1) * d)
            q = mq_ref[s, :, hs]
            k = mk_ref[s, :, hs]
            v_t = mv_ref[s, hs, :]
            ct_prev = ct_ref[s, h]
            s_t = _dot_nt(k, q)
            inter = _dot_nt(ct_prev.astype(BF16), q)
            log_d = jnp.where(causal, u[:, h:h + 1] + b_t[h:h + 1, :], -jnp.inf)
            li = li_t[h:h + 1, :]
            m_t = jnp.maximum(li, jnp.max(log_d, axis=0, keepdims=True))
            p_t = s_t * jnp.exp(log_d - m_t)
            s_inter = jnp.exp(li - m_t)
            num = s_inter * inter[:d] + _dot(v_t, p_t.astype(BF16))
            den = s_inter * inter[d:d + 1] + jnp.sum(p_t, axis=0, keepdims=True)
            hh = num / jnp.maximum(jnp.abs(den), jnp.exp(-m_t))
            y = (hh * lax.rsqrt(jnp.mean(hh * hh, axis=0, keepdims=True) + EPS)).T
            out_ref[s, :, hs] = (_sigmoid(mo_ref[s, :, hs].astype(F32)) * (y * g_ref[:, hs])).astype(BF16)

            ak = a[:, h:h + 1] * k.astype(F32)
            so = s_old[:, h:h + 1]
            sl = s_loc[:, h:h + 1]
            ct_ref[s, h, 0:d, :] = so * ct_prev[:d] + sl * _dot(v_t, ak.astype(BF16))
            ct_ref[s, h, d:d + 1, :] = so * ct_prev[d:d + 1] + sl * jnp.sum(ak, axis=0, keepdims=True)


def _mlstm(mq, mk, mv, mo, ifg, g, batch, seq):
    L = M_CHUNK
    nb = M_BATCH
    seq3 = lambda t: t.reshape(batch, seq, t.shape[-1])
    blk = pl.BlockSpec((nb, L, M_WIDTH), lambda b, c: (b, c, 0))
    blk_t = pl.BlockSpec((nb, M_WIDTH, L), lambda b, c: (b, 0, c))
    out = pl.pallas_call(
        _mlstm_kernel,
        grid=(batch // nb, seq // L),
        in_specs=[blk, blk, blk_t, blk, pl.BlockSpec((nb, L, LANES), lambda b, c: (b, c, 0)),
                  _const_spec((1, M_WIDTH))],
        out_specs=blk,
        out_shape=jax.ShapeDtypeStruct((batch, seq, M_WIDTH), BF16),
        scratch_shapes=[pltpu.VMEM((nb, M_HEADS, M_HEAD_DIM + 2 * SUBLANES, M_HEAD_DIM), F32),
                        pltpu.VMEM((nb, 1, LANES), F32)],
        compiler_params=pltpu.CompilerParams(dimension_semantics=("arbitrary", "arbitrary"),
                                             vmem_limit_bytes=VMEM_LIMIT),
        name="mlstm",
    )(seq3(mq), seq3(mk), mv, seq3(mo), seq3(ifg), g)
    return out.reshape(batch * seq, M_WIDTH)


def _t5_bucket(n):
    max_exact = REL_BUCKETS // 2
    nf = jnp.maximum(n, 1).astype(F32)
    large = max_exact + (jnp.log(nf / max_exact) / math.log(REL_MAX_DIST / max_exact)
                         * (REL_BUCKETS - max_exact)).astype(jnp.int32)
    large = jnp.minimum(large, REL_BUCKETS - 1)
    return jnp.where(n < max_exact, n, large)


def _attn_kernel(rb_ref, lam_ref, q_ref, k_ref, v_ref, g_ref, out_ref,
                 bias_ref, m_ref, l_ref, acc_ref, s_ref, *, lam_init):
    tq = ATT_BLOCK
    tk = tq
    sb = ATT_SUB
    nq = q_ref.shape[0] // tq
    h = pl.program_id(1)

    kk = lax.broadcasted_iota(jnp.int32, (sb, sb), 0)
    qq = lax.broadcasted_iota(jnp.int32, (sb, sb), 1)

    @pl.when((pl.program_id(0) == 0) & (h == 0))
    def _():
        for delta in range(2):
            bucket = _t5_bucket(jnp.maximum(qq + delta * sb - kk, 0))
            for hh in range(DA_HEADS):
                t = jnp.zeros((sb, sb), F32)
                for j in range(REL_BUCKETS):
                    t = jnp.where(bucket == j, rb_ref[j, hh], t)
                bias_ref[hh, delta] = (t - rb_ref[REL_BUCKETS - 1, hh]) * LOG2E

    lv = lam_ref[...]
    lam = (jnp.exp(jnp.sum(lv[0:1] * lv[1:2], axis=1, keepdims=True))
           - jnp.exp(jnp.sum(lv[2:3] * lv[3:4], axis=1, keepdims=True)) + lam_init)

    n_groups = 2 * tq // sb
    d0 = bias_ref[h, 0]
    d1 = bias_ref[h, 1]
    causal = qq >= kk
    ident = lambda s: s
    near = lambda s: jnp.concatenate([s[:sb], s[sb:] + d1], axis=0)
    early = lambda s: jnp.where(causal, s + d0, NEG_BIG)
    late = lambda s: jnp.concatenate([s[:sb] + d1, jnp.where(causal, s[sb:] + d0, NEG_BIG)], axis=0)

    def plan(j, n):
        if j == n - 1:
            return [(sb, early), (tk, late)] * (n_groups // 2)
        if j == n - 2:
            return [(tk, near), (tk, ident)] * (n_groups // 2)
        return [(tk, ident)] * n_groups

    lane = lax.broadcasted_iota(jnp.int32, (tq, DA_V_DIM), 1)
    step = 0
    for qb in range(nq):
        n = qb + 1
        st = qb % 2
        q = q_ref[qb * tq:(qb + 1) * tq, :]
        zero = jnp.zeros_like(q)
        qcat = jnp.concatenate([jnp.where(lane < DA_HEAD_DIM, q, zero),
                                jnp.where(lane >= DA_HEAD_DIM, q, zero)], axis=0)
        m_ref[st] = jnp.full(m_ref.shape[1:], -jnp.inf, F32)
        l_ref[st] = jnp.zeros(l_ref.shape[1:], F32)
        acc_ref[st] = jnp.zeros(acc_ref.shape[1:], F32)

        def scores(j, buf):
            k = k_ref[j * tk:(j + 1) * tk, :]
            for g, (rows, _) in enumerate(plan(j, n)):
                s_ref[buf, g, 0:rows, :] = _dot_nt(k[:rows], qcat[g * sb:(g + 1) * sb])

        def softmax_pv(j, buf):
            v_t = v_ref[:, j * tk:(j + 1) * tk]
            for g, (rows, fix) in enumerate(plan(j, n)):
                cols = slice(g * sb, (g + 1) * sb)
                s = fix(s_ref[buf, g, 0:rows, :])
                m_prev = m_ref[st, :, cols]
                m_new = jnp.maximum(m_prev, jnp.max(s, axis=0, keepdims=True))
                alpha = jnp.exp2(m_prev - m_new)
                p = jnp.exp2(s - m_new)
                l_ref[st, :, cols] = alpha * l_ref[st, :, cols] + jnp.sum(p, axis=0, keepdims=True)
                acc_ref[st, :, cols] = (alpha * acc_ref[st, :, cols]
                                        + _dot(v_t[:, :rows], p.astype(BF16)))
                m_ref[st, :, cols] = m_new

        scores(0, step % 2)
        for j in range(n):
            if j + 1 < n:
                scores(j + 1, (step + 1) % 2)
            softmax_pv(j, step % 2)
            step += 1

        o = acc_ref[st] * (1.0 / l_ref[st])
        out = (o[:, :tq] - lam * o[:, tq:]).T
        out_ref[qb * tq:(qb + 1) * tq, :] = (_rms(out, g_ref[...]) * (1.0 - lam_init)).astype(BF16)


def _attention(rel_bias, lam_vec, dq, dk, dv, g, batch, seq, lam_init):
    T = dq.shape[0]
    tq = ATT_BLOCK
    seq_blk = pl.BlockSpec((seq, DA_V_DIM), lambda b, h: (b, h))
    return pl.pallas_call(
        functools.partial(_attn_kernel, lam_init=lam_init),
        grid=(batch, DA_HEADS),
        in_specs=[pl.BlockSpec(memory_space=pltpu.SMEM),
                  pl.BlockSpec((4, DA_HEAD_DIM), lambda b, h: (0, 0)),
                  seq_blk, seq_blk,
                  pl.BlockSpec((None, DA_V_DIM, seq), lambda b, h: (b, h, 0)),
                  pl.BlockSpec((1, DA_V_DIM), lambda b, h: (0, h))],
        out_specs=seq_blk,
        out_shape=jax.ShapeDtypeStruct((T, DA_WIDTH), BF16),
        scratch_shapes=[pltpu.VMEM((DA_HEADS, 2, ATT_SUB, ATT_SUB), F32),
                        pltpu.VMEM((2, 1, 2 * tq), F32),
                        pltpu.VMEM((2, 1, 2 * tq), F32),
                        pltpu.VMEM((2, DA_V_DIM, 2 * tq), F32),
                        pltpu.VMEM((2, 2 * tq // ATT_SUB, tq, ATT_SUB), F32)],
        compiler_params=pltpu.CompilerParams(dimension_semantics=("arbitrary",) * 2,
                                             vmem_limit_bytes=VMEM_LIMIT),
        name="diffattn",
    )(rel_bias, lam_vec, dq, dk, dv, g)


def _gelu_tanh(x):
    return 0.5 * x * (1.0 + jnp.tanh(math.sqrt(2.0 / math.pi) * (x + 0.044715 * (x * x * x))))


def _ffn_kernel(gm_ref, hd_ref, gates_ref, x_ref, p_ref, wm_ref, wd_ref, wo_ref,
                gf_ref, wuv_ref, wug_ref, cw_ref, cb_ref, wdn_ref, gp_ref, wpg_ref, wp_ref, gl_ref,
                out_ref, carry_ref, slab_ref, acc_ref, wup_s, wdn_s, *, tiles_per_seq, final_norm):
    tm = x_ref.shape[0]
    step = pl.program_id(0)
    i = step - N_FF_CHUNKS

    @pl.when(step < N_FF_CHUNKS)
    def _():
        wup_s[step, 0] = wuv_ref[...].astype(BF16)
        wup_s[step, 1] = wug_ref[...].astype(BF16)
        wdn_s[pl.ds(pl.multiple_of(step * FF_CHUNK, FF_CHUNK), FF_CHUNK), :] = wdn_ref[...].astype(BF16)

    @pl.when((i >= 0) & (i % tiles_per_seq == 0))
    def _():
        carry_ref[...] = jnp.zeros(carry_ref.shape, F32)

    @pl.when(i >= 0)
    def _():
        _ffn_tile(tm, gm_ref, hd_ref, gates_ref, x_ref, p_ref, wm_ref, wd_ref, wo_ref, gf_ref, cw_ref, cb_ref,
                  gp_ref, wpg_ref, wp_ref, gl_ref, out_ref, carry_ref, slab_ref, acc_ref, wup_s, wdn_s, final_norm)


def _ffn_tile(tm, gm_ref, hd_ref, gates_ref, x_ref, p_ref, wm_ref, wd_ref, wo_ref, gf_ref, cw_ref, cb_ref,
              gp_ref, wpg_ref, wp_ref, gl_ref, out_ref, carry_ref, slab_ref, acc_ref, wup_s, wdn_s, final_norm):
    ya = _dot(gm_ref[...], wm_ref[...])
    yb = _dot(hd_ref[...], wd_ref[...])
    ga = _sigmoid(gates_ref[:, :D_MODEL].astype(F32))
    gb = _sigmoid(gates_ref[:, D_MODEL:].astype(F32))
    x1 = x_ref[...] + _dot((ga * ya + gb * yb).astype(BF16), wo_ref[...])

    hb = _rms(x1, gf_ref[...]).astype(BF16)

    def slab0(c, half):
        return (c % FF_SLAB_SETS) * FF_SLABS + half * (FF_CHUNK // LANES)

    def up(c):
        for half in range(2):
            _slab_store(_dot(hb, wup_s[c, half]), slab_ref, slab0(c, half))

    def conv(c, half):
        return _slab_conv(tm, FF_CHUNK, half * D_FF + c * FF_CHUNK, slab_ref, slab0(c, half),
                          carry_ref, cw_ref, cb_ref)

    for c in range(FF_LOOKAHEAD):
        up(c)
    acts = []
    for c in range(N_FF_CHUNKS):
        if c + FF_LOOKAHEAD < N_FF_CHUNKS:
            up(c + FF_LOOKAHEAD)
        acts.append((_gelu_tanh(conv(c, 1)) * conv(c, 0)).astype(BF16))
        if len(acts) == FF_DOWN_GROUP or c == N_FF_CHUNKS - 1:
            first = c + 1 - len(acts)
            d = _dot(jnp.concatenate(acts, axis=1), wdn_s[first * FF_CHUNK:(c + 1) * FF_CHUNK, :])
            if first == 0:
                acc_ref[...] = d
            else:
                acc_ref[...] += d
            acts = []

    x2 = x1 + acc_ref[...]
    hg = _rms(x2, gp_ref[...]).astype(BF16)
    gate = _sigmoid(_dot(hg, wpg_ref[...]))
    pe = _dot(p_ref[...].astype(BF16), wp_ref[...])
    x3 = x2 + gate * pe
    out_ref[...] = _rms(x3, gl_ref[...]) if final_norm else x3


def _ffn(gm, hd, gates, x2, p2, wm, wd, wo, gf, wup, cw, cb, wdn, gp, wpg, wp, gl, seq, final_norm):
    T = x2.shape[0]
    tm = TOKEN_TILE
    n_stage = N_FF_CHUNKS
    row = lambda s: (jnp.maximum(s - n_stage, 0), 0)
    stage = lambda s: jnp.minimum(s, n_stage - 1)
    return pl.pallas_call(
        functools.partial(_ffn_kernel, tiles_per_seq=seq // tm, final_norm=final_norm),
        grid=(n_stage + T // tm,),
        in_specs=[pl.BlockSpec((tm, M_WIDTH), row), pl.BlockSpec((tm, DA_WIDTH), row),
                  pl.BlockSpec((tm, 2 * D_MODEL), row), pl.BlockSpec((tm, D_MODEL), row),
                  pl.BlockSpec((tm, PLE_DIM), row),
                  _const_spec((M_WIDTH, D_MODEL)), _const_spec((DA_WIDTH, D_MODEL)),
                  _const_spec((D_MODEL, D_MODEL)), _const_spec((1, D_MODEL)),
                  pl.BlockSpec((D_MODEL, FF_CHUNK), lambda s: (0, stage(s))),
                  pl.BlockSpec((D_MODEL, FF_CHUNK), lambda s: (0, n_stage + stage(s))),
                  _const_spec((FFN_CONV, 2 * D_FF)), _const_spec((1, 2 * D_FF)),
                  pl.BlockSpec((FF_CHUNK, D_MODEL), lambda s: (stage(s), 0)),
                  _const_spec((1, D_MODEL)), _const_spec((D_MODEL, D_MODEL)),
                  _const_spec((PLE_DIM, D_MODEL)), _const_spec((1, D_MODEL))],
        out_specs=pl.BlockSpec((tm, D_MODEL), row),
        out_shape=jax.ShapeDtypeStruct((T, D_MODEL), F32),
        scratch_shapes=[pltpu.VMEM((SUBLANES, 2 * D_FF), F32),
                        pltpu.VMEM((FF_SLAB_SETS * FF_SLABS, SUBLANES + tm, LANES), F32),
                        pltpu.VMEM((tm, D_MODEL), F32),
                        pltpu.VMEM((N_FF_CHUNKS, 2, D_MODEL, FF_CHUNK), BF16),
                        pltpu.VMEM((D_FF, D_MODEL), BF16)],
        compiler_params=pltpu.CompilerParams(dimension_semantics=("arbitrary",), vmem_limit_bytes=VMEM_LIMIT),
        name="convffn",
    )(gm, hd, gates, x2, p2, wm, wd, wo, gf, wup, wup, cw, cb, wdn, gp, wpg, wp, gl)


def kernel(x, p, rel_bias, norm_mix_g, w_in, b_if, m_conv_w, m_conv_b, m_norm_g, da_lambda, da_norm_g,
           w_br_m, w_br_d, w_out, norm_ffn_g, w_up, ffn_conv_w, ffn_conv_b, w_down, norm_ple_g, w_ple_gate,
           w_ple, norm_final_g):
    batch, seq, _ = x.shape
    depth = w_in.shape[0]
    T = batch * seq
    assert seq % TOKEN_TILE == 0 and seq % M_CHUNK == 0 and seq % ATT_BLOCK == 0 and batch % M_BATCH == 0
    xt = x.reshape(T, D_MODEL)
    row = lambda v: v.reshape(1, -1).astype(F32)

    for l in range(depth):
        w_t = jnp.swapaxes(w_in[l].astype(F32), 0, 1)
        bif = jnp.concatenate([b_if[l].astype(F32), jnp.zeros((LANES - N_IF,), F32)]).reshape(1, LANES)

        mq, mk, mv, mo, dq, dk, dv, gates, ifg = _inproj(
            xt, row(norm_mix_g[l]), w_t, m_conv_w[l].astype(F32), row(m_conv_b[l]), bif, seq)

        gm = _mlstm(mq, mk, mv, mo, ifg, row(m_norm_g[l]), batch, seq)

        lam_init = 0.8 - 0.6 * math.exp(-0.3 * l)
        hd = _attention(rel_bias.astype(F32), da_lambda[l].astype(F32), dq, dk, dv, row(da_norm_g[l]),
                        batch, seq, lam_init)

        xt = _ffn(gm, hd, gates, xt, p[l].reshape(T, PLE_DIM),
                  w_br_m[l].astype(BF16), w_br_d[l].astype(BF16), w_out[l].astype(BF16),
                  row(norm_ffn_g[l]), w_up[l].astype(F32),
                  ffn_conv_w[l].astype(F32), row(ffn_conv_b[l]), w_down[l].astype(F32), row(norm_ple_g[l]),
                  w_ple_gate[l].astype(BF16), w_ple[l].astype(BF16), row(norm_final_g), seq,
                  final_norm=(l == depth - 1))

    return xt.reshape(batch, seq, D_MODEL)
```

```python
import functools
import math

import jax
import jax.numpy as jnp
from jax import lax
from jax.experimental import pallas as pl
from jax.experimental.pallas import tpu as pltpu

D_MODEL = 1024
PLE_DIM = 256
M_HEADS = 4
M_HEAD_DIM = 128
M_WIDTH = M_HEADS * M_HEAD_DIM
M_CONV = 4
DA_HEADS = 4
DA_HEAD_DIM = 64
DA_V_DIM = 2 * DA_HEAD_DIM
DA_WIDTH = DA_HEADS * DA_V_DIM
REL_BUCKETS = 32
REL_MAX_DIST = 128
D_FF = 2816
FFN_CONV = 3
EPS = 1e-6
NEG_BIG = -1e30
LOG2E = 1.4426950408889634

LANES = 128
SUBLANES = 8
VMEM_LIMIT = 56 * 1024 * 1024

TOKEN_TILE = 512
M_CHUNK = 256
M_BATCH = 8
ATT_BLOCK = 512
ATT_SUB = ATT_BLOCK // 2
FF_CHUNK = 256
N_FF_CHUNKS = D_FF // FF_CHUNK
FF_SLABS = 2 * FF_CHUNK // LANES
FF_LOOKAHEAD = 3
FF_SLAB_SETS = FF_LOOKAHEAD + 1
FF_DOWN_GROUP = 2

F32 = jnp.float32
BF16 = jnp.bfloat16

C_IF = 4 * M_WIDTH
N_IF = 2 * M_HEADS
IN_COLS = C_IF + N_IF + 3 * DA_WIDTH + 2 * D_MODEL
C_TAIL = IN_COLS - N_IF


def _dot(a, b):
    return jnp.dot(a, b, preferred_element_type=F32)


def _dot_nt(a, b):
    return lax.dot_general(a, b, (((1,), (1,)), ((), ())), preferred_element_type=F32)


def _rms(x, g):
    return x * lax.rsqrt(jnp.mean(x * x, -1, keepdims=True) + EPS) * g


def _sigmoid(x):
    return 1.0 / (1.0 + jnp.exp(-x))


def _slab_store(u, slab_ref, slab0):
    tm, n = u.shape
    for j in range(n // LANES):
        slab_ref[slab0 + j, SUBLANES:SUBLANES + tm, :] = u[:, j * LANES:(j + 1) * LANES]


def _slab_conv(tm, n, col0, slab_ref, slab0, carry_ref, cw_ref, cb_ref):
    taps = cw_ref.shape[0]
    outs = []
    for j in range(n // LANES):
        cols = slice(col0 + j * LANES, col0 + (j + 1) * LANES)
        s = slab0 + j
        slab_ref[s, 0:SUBLANES, :] = carry_ref[:, cols]
        carry_ref[:, cols] = slab_ref[s, tm:tm + SUBLANES, :]
        y = cb_ref[:, cols]
        for k in range(taps):
            y = y + cw_ref[taps - 1 - k:taps - k, cols] * slab_ref[s, pl.ds(SUBLANES - k, tm), :]
        outs.append(y)
    return jnp.concatenate(outs, axis=1)


def _cumsum_rows(tril, x):
    hi = x.astype(BF16)
    r1 = x - hi.astype(F32)
    mid = r1.astype(BF16)
    lo = (r1 - mid.astype(F32)).astype(BF16)
    return _dot(tril, hi) + _dot(tril, mid) + _dot(tril, lo)


def _const_spec(shape):
    nd = len(shape)
    return pl.BlockSpec(shape, lambda *_: (0,) * nd, pipeline_mode=pl.Buffered(1))


W_BLOCK = 512
N_W_BLOCKS = C_TAIL // W_BLOCK + 1


def _inproj_kernel(x_ref, g_ref, wt_ref, cw_ref, cb_ref, bif_ref,
                   mq_ref, mk_ref, mv_ref, mo_ref, dq_ref, dk_ref, dv_ref, gates_ref, ifg_ref,
                   carry_ref, slab_ref, w_s, *, tiles_per_seq):
    tm = x_ref.shape[0]
    step = pl.program_id(0)
    i = step - N_W_BLOCKS

    @pl.when(step < N_W_BLOCKS)
    def _():
        row = lax.broadcasted_iota(jnp.int32, wt_ref.shape, 0)
        keep = (step < N_W_BLOCKS - 1) | (row < N_IF)
        w_s[step] = jnp.where(keep, wt_ref[...], 0.0).T.astype(BF16)

    @pl.when((i >= 0) & (i % tiles_per_seq == 0))
    def _():
        carry_ref[...] = jnp.zeros(carry_ref.shape, F32)

    @pl.when(i >= 0)
    def _():
        _inproj_tile(tm, x_ref, g_ref, cw_ref, cb_ref, bif_ref, mq_ref, mk_ref, mv_ref, mo_ref, dq_ref, dk_ref,
                     dv_ref, gates_ref, ifg_ref, carry_ref, slab_ref, w_s)


def _inproj_tile(tm, x_ref, g_ref, cw_ref, cb_ref, bif_ref, mq_ref, mk_ref, mv_ref, mo_ref, dq_ref, dk_ref,
                 dv_ref, gates_ref, ifg_ref, carry_ref, slab_ref, w_s):
    hb = _rms(x_ref[...], g_ref[...]).astype(BF16)

    def conv_silu(col0):
        y = _slab_conv(tm, M_WIDTH, col0, slab_ref, col0 // LANES, carry_ref, cw_ref, cb_ref)
        return y * _sigmoid(y)

    mv_ref[...] = _dot(hb, w_s[2]).astype(BF16).T
    mo_ref[...] = _dot(hb, w_s[3]).astype(BF16)
    _slab_store(_dot(hb, w_s[0]), slab_ref, 0)
    dq_ref[...] = (_dot(hb, w_s[4]) * (LOG2E * DA_HEAD_DIM ** -0.5)).astype(BF16)
    dk_ref[...] = _dot(hb, w_s[5]).astype(BF16)
    dv_ref[...] = _dot(hb, w_s[6]).astype(BF16).T
    _slab_store(_dot(hb, w_s[1]), slab_ref, M_WIDTH // LANES)
    for j in range(4):
        gates_ref[:, j * W_BLOCK:(j + 1) * W_BLOCK] = _dot(hb, w_s[7 + j]).astype(BF16)

    ifg_ref[...] = _dot(hb, w_s[N_W_BLOCKS - 1, :, 0:LANES]) + bif_ref[...]

    mq_ref[...] = conv_silu(0).astype(BF16)
    mk_ref[...] = (conv_silu(M_WIDTH) * (M_HEAD_DIM ** -0.5)).astype(BF16)


def _inproj(x2, g, w_t, cw, cb, bif, seq):
    assert M_WIDTH == DA_WIDTH == W_BLOCK and C_IF % W_BLOCK == 0
    T = x2.shape[0]
    tm = TOKEN_TILE
    tps = seq // tm
    tile = lambda s: jnp.maximum(s - N_W_BLOCKS, 0)
    row = lambda s: (tile(s), 0)

    def src(s):
        j = jnp.minimum(s, N_W_BLOCKS - 1)
        shifted = j * W_BLOCK + jnp.where(j >= C_IF // W_BLOCK, N_IF, 0)
        return pl.multiple_of(jnp.where(j < N_W_BLOCKS - 1, shifted, C_IF), SUBLANES), 0

    bf = lambda n: jax.ShapeDtypeStruct((T, n), BF16)
    bf_t = lambda n: jax.ShapeDtypeStruct((T // seq, n, seq), BF16)
    blk = pl.BlockSpec((tm, M_WIDTH), row)
    blk_t = pl.BlockSpec((None, M_WIDTH, tm), lambda s: (tile(s) // tps, 0, tile(s) % tps))
    out_shapes = [bf(M_WIDTH), bf(M_WIDTH), bf_t(M_WIDTH), bf(M_WIDTH), bf(DA_WIDTH), bf(DA_WIDTH), bf_t(DA_WIDTH),
                  bf(2 * D_MODEL), jax.ShapeDtypeStruct((T, LANES), F32)]
    out_specs = [blk, blk, blk_t, blk, blk, blk, blk_t,
                 pl.BlockSpec((tm, 2 * D_MODEL), row), pl.BlockSpec((tm, LANES), row)]
    return pl.pallas_call(
        functools.partial(_inproj_kernel, tiles_per_seq=tps),
        grid=(N_W_BLOCKS + T // tm,),
        in_specs=[pl.BlockSpec((tm, D_MODEL), row), _const_spec((1, D_MODEL)),
                  pl.BlockSpec((pl.Element(W_BLOCK), pl.Element(D_MODEL)), src),
                  _const_spec((M_CONV, 2 * M_WIDTH)), _const_spec((1, 2 * M_WIDTH)), _const_spec((1, LANES))],
        out_specs=out_specs,
        out_shape=out_shapes,
        scratch_shapes=[pltpu.VMEM((SUBLANES, 2 * M_WIDTH), F32),
                        pltpu.VMEM((2 * M_WIDTH // LANES, SUBLANES + tm, LANES), F32),
                        pltpu.VMEM((N_W_BLOCKS, D_MODEL, W_BLOCK), BF16)],
        compiler_params=pltpu.CompilerParams(dimension_semantics=("arbitrary",), vmem_limit_bytes=VMEM_LIMIT),
        name="inproj",
    )(x2, g, w_t, cw, cb, bif)


def _mlstm_kernel(mq_ref, mk_ref, mv_ref, mo_ref, ifg_ref, g_ref, out_ref, ct_ref, m_ref):
    L = mq_ref.shape[1]
    d = M_HEAD_DIM

    @pl.when(pl.program_id(1) == 0)
    def _():
        ct_ref[...] = jnp.zeros(ct_ref.shape, F32)
        m_ref[...] = jnp.zeros(m_ref.shape, F32)

    kidx = lax.broadcasted_iota(jnp.int32, (L, L), 0)
    qidx = lax.broadcasted_iota(jnp.int32, (L, L), 1)
    tril = jnp.where(kidx >= qidx, 1.0, 0.0).astype(BF16)
    causal = kidx <= qidx

    nb = mq_ref.shape[0]
    assert nb * N_IF <= LANES
    pre = ifg_ref[0]
    for s in range(1, nb):
        pre = pre + pltpu.roll(ifg_ref[s], N_IF * s, axis=1)
    fg = pltpu.roll(pre, LANES - M_HEADS, axis=1)
    logf = jnp.minimum(fg, 0.0) - jnp.log1p(jnp.exp(-jnp.abs(fg)))
    ig = pre * LOG2E
    b = _cumsum_rows(tril, logf * LOG2E)
    b_tot = b[L - 1:L, :]
    u = ig - b
    w_end = b_tot + u
    m_loc = jnp.max(w_end, axis=0, keepdims=True)
    a = jnp.exp2(w_end - m_loc)
    m_prev = m_ref[...]
    m_new = jnp.maximum(b_tot + m_prev, m_loc)
    s_old = jnp.exp2(b_tot + m_prev - m_new)
    s_loc = jnp.exp2(m_loc - m_new)
    li_t = (b + m_prev).T
    b_t = b.T
    m_ref[...] = m_new

    for s in range(nb):
        for h in range(M_HEADS):
            hs = slice(h * d, (h + 1) * d)
            ln = N_IF * s + h
            q = mq_ref[s, :, hs]
            k = mk_ref[s, :, hs]
            v_t = mv_ref[s, hs, :]
            ct_prev = ct_ref[s, h]
            s_t = _dot_nt(k, q)
            inter = _dot_nt(ct_prev.astype(BF16), q)
            log_d = jnp.where(causal, u[:, ln:ln + 1] + b_t[ln:ln + 1, :], -jnp.inf)
            li = li_t[ln:ln + 1, :]
            m_t = jnp.maximum(li, jnp.max(log_d, axis=0, keepdims=True))
            p_t = s_t * jnp.exp2(log_d - m_t)
            s_inter = jnp.exp2(li - m_t)
            num = s_inter * inter[:d] + _dot(v_t, p_t.astype(BF16))
            den = s_inter * inter[d:d + 1] + jnp.sum(p_t, axis=0, keepdims=True)
            hh = num / jnp.maximum(jnp.abs(den), jnp.exp2(-m_t))
            y = (hh * lax.rsqrt(jnp.mean(hh * hh, axis=0, keepdims=True) + EPS)).T
            out_ref[s, :, hs] = (_sigmoid(mo_ref[s, :, hs].astype(F32)) * (y * g_ref[:, hs])).astype(BF16)

            ak = a[:, ln:ln + 1] * k.astype(F32)
            so = s_old[:, ln:ln + 1]
            sl = s_loc[:, ln:ln + 1]
            ct_ref[s, h, 0:d, :] = so * ct_prev[:d] + sl * _dot(v_t, ak.astype(BF16))
            ct_ref[s, h, d:d + 1, :] = so * ct_prev[d:d + 1] + sl * jnp.sum(ak, axis=0, keepdims=True)


def _mlstm(mq, mk, mv, mo, ifg, g, batch, seq):
    L = M_CHUNK
    nb = M_BATCH
    seq3 = lambda t: t.reshape(batch, seq, t.shape[-1])
    blk = pl.BlockSpec((nb, L, M_WIDTH), lambda b, c: (b, c, 0))
    blk_t = pl.BlockSpec((nb, M_WIDTH, L), lambda b, c: (b, 0, c))
    out = pl.pallas_call(
        _mlstm_kernel,
        grid=(batch // nb, seq // L),
        in_specs=[blk, blk, blk_t, blk, pl.BlockSpec((nb, L, LANES), lambda b, c: (b, c, 0)),
                  _const_spec((1, M_WIDTH))],
        out_specs=blk,
        out_shape=jax.ShapeDtypeStruct((batch, seq, M_WIDTH), BF16),
        scratch_shapes=[pltpu.VMEM((nb, M_HEADS, M_HEAD_DIM + 2 * SUBLANES, M_HEAD_DIM), F32),
                        pltpu.VMEM((1, LANES), F32)],
        compiler_params=pltpu.CompilerParams(dimension_semantics=("arbitrary", "arbitrary"),
                                             vmem_limit_bytes=VMEM_LIMIT),
        name="mlstm",
    )(seq3(mq), seq3(mk), mv, seq3(mo), seq3(ifg), g)
    return out.reshape(batch * seq, M_WIDTH)


def _t5_bucket(n):
    max_exact = REL_BUCKETS // 2
    nf = jnp.maximum(n, 1).astype(F32)
    large = max_exact + (jnp.log(nf / max_exact) / math.log(REL_MAX_DIST / max_exact)
                         * (REL_BUCKETS - max_exact)).astype(jnp.int32)
    large = jnp.minimum(large, REL_BUCKETS - 1)
    return jnp.where(n < max_exact, n, large)


def _attn_kernel(rb_ref, lam_ref, q_ref, k_ref, v_ref, g_ref, out_ref,
                 bias_ref, m_ref, l_ref, acc_ref, s_ref, *, lam_init):
    tq = ATT_BLOCK
    tk = tq
    sb = ATT_SUB
    nq = q_ref.shape[0] // tq
    h = pl.program_id(1)

    kk = lax.broadcasted_iota(jnp.int32, (sb, sb), 0)
    qq = lax.broadcasted_iota(jnp.int32, (sb, sb), 1)

    @pl.when((pl.program_id(0) == 0) & (h == 0))
    def _():
        for delta in range(2):
            bucket = _t5_bucket(jnp.maximum(qq + delta * sb - kk, 0))
            for hh in range(DA_HEADS):
                t = jnp.zeros((sb, sb), F32)
                for j in range(REL_BUCKETS):
                    t = jnp.where(bucket == j, rb_ref[j, hh], t)
                bias_ref[hh, delta] = (t - rb_ref[REL_BUCKETS - 1, hh]) * LOG2E

    lv = lam_ref[...]
    lam = (jnp.exp(jnp.sum(lv[0:1] * lv[1:2], axis=1, keepdims=True))
           - jnp.exp(jnp.sum(lv[2:3] * lv[3:4], axis=1, keepdims=True)) + lam_init)

    n_groups = 2 * tq // sb
    d0 = bias_ref[h, 0]
    d1 = bias_ref[h, 1]
    causal = qq >= kk
    ident = lambda s: s
    near = lambda s: jnp.concatenate([s[:sb], s[sb:] + d1], axis=0)
    early = lambda s: jnp.where(causal, s + d0, NEG_BIG)
    late = lambda s: jnp.concatenate([s[:sb] + d1, jnp.where(causal, s[sb:] + d0, NEG_BIG)], axis=0)

    def plan(j, n):
        if j == n - 1:
            return [(sb, early), (tk, late)] * (n_groups // 2)
        if j == n - 2:
            return [(tk, near), (tk, ident)] * (n_groups // 2)
        return [(tk, ident)] * n_groups

    lane = lax.broadcasted_iota(jnp.int32, (tq, DA_V_DIM), 1)
    step = 0
    for qb in range(nq):
        n = qb + 1
        st = qb % 2
        q = q_ref[qb * tq:(qb + 1) * tq, :]
        zero = jnp.zeros_like(q)
        qcat = jnp.concatenate([jnp.where(lane < DA_HEAD_DIM, q, zero),
                                jnp.where(lane >= DA_HEAD_DIM, q, zero)], axis=0)
        m_ref[st] = jnp.full(m_ref.shape[1:], -jnp.inf, F32)
        l_ref[st] = jnp.zeros(l_ref.shape[1:], F32)
        acc_ref[st] = jnp.zeros(acc_ref.shape[1:], F32)

        def scores(j, buf):
            k = k_ref[j * tk:(j + 1) * tk, :]
            for g, (rows, _) in enumerate(plan(j, n)):
                s_ref[buf, g, 0:rows, :] = _dot_nt(k[:rows], qcat[g * sb:(g + 1) * sb])

        def softmax_pv(j, buf):
            v_t = v_ref[:, j * tk:(j + 1) * tk]
            for g, (rows, fix) in enumerate(plan(j, n)):
                cols = slice(g * sb, (g + 1) * sb)
                s = fix(s_ref[buf, g, 0:rows, :])
                m_prev = m_ref[st, :, cols]
                m_new = jnp.maximum(m_prev, jnp.max(s, axis=0, keepdims=True))
                alpha = jnp.exp2(m_prev - m_new)
                p = jnp.exp2(s - m_new)
                l_ref[st, :, cols] = alpha * l_ref[st, :, cols] + jnp.sum(p, axis=0, keepdims=True)
                acc_ref[st, :, cols] = (alpha * acc_ref[st, :, cols]
                                        + _dot(v_t[:, :rows], p.astype(BF16)))
                m_ref[st, :, cols] = m_new

        scores(0, step % 2)
        for j in range(n):
            if j + 1 < n:
                scores(j + 1, (step + 1) % 2)
            softmax_pv(j, step % 2)
            step += 1

        o = acc_ref[st] * (1.0 / l_ref[st])
        out = (o[:, :tq] - lam * o[:, tq:]).T
        out_ref[qb * tq:(qb + 1) * tq, :] = (_rms(out, g_ref[...]) * (1.0 - lam_init)).astype(BF16)


def _attention(rel_bias, lam_vec, dq, dk, dv, g, batch, seq, lam_init):
    T = dq.shape[0]
    tq = ATT_BLOCK
    seq_blk = pl.BlockSpec((seq, DA_V_DIM), lambda b, h: (b, h))
    return pl.pallas_call(
        functools.partial(_attn_kernel, lam_init=lam_init),
        grid=(batch, DA_HEADS),
        in_specs=[pl.BlockSpec(memory_space=pltpu.SMEM),
                  pl.BlockSpec((4, DA_HEAD_DIM), lambda b, h: (0, 0)),
                  seq_blk, seq_blk,
                  pl.BlockSpec((None, DA_V_DIM, seq), lambda b, h: (b, h, 0)),
                  pl.BlockSpec((1, DA_V_DIM), lambda b, h: (0, h))],
        out_specs=seq_blk,
        out_shape=jax.ShapeDtypeStruct((T, DA_WIDTH), BF16),
        scratch_shapes=[pltpu.VMEM((DA_HEADS, 2, ATT_SUB, ATT_SUB), F32),
                        pltpu.VMEM((2, 1, 2 * tq), F32),
                        pltpu.VMEM((2, 1, 2 * tq), F32),
                        pltpu.VMEM((2, DA_V_DIM, 2 * tq), F32),
                        pltpu.VMEM((2, 2 * tq // ATT_SUB, tq, ATT_SUB), F32)],
        compiler_params=pltpu.CompilerParams(dimension_semantics=("arbitrary",) * 2,
                                             vmem_limit_bytes=VMEM_LIMIT),
        name="diffattn",
    )(rel_bias, lam_vec, dq, dk, dv, g)


def _gelu_tanh(x):
    return 0.5 * x * (1.0 + jnp.tanh(math.sqrt(2.0 / math.pi) * (x + 0.044715 * (x * x * x))))


def _ffn_kernel(gm_ref, hd_ref, gates_ref, x_ref, p_ref, wm_ref, wd_ref, wo_ref,
                gf_ref, wuv_ref, wug_ref, cw_ref, cb_ref, wdn_ref, gp_ref, wpg_ref, wp_ref, gl_ref,
                out_ref, carry_ref, slab_ref, acc_ref, wup_s, wdn_s, *, tiles_per_seq, final_norm):
    tm = x_ref.shape[0]
    step = pl.program_id(0)
    i = step - N_FF_CHUNKS

    @pl.when(step < N_FF_CHUNKS)
    def _():
        wup_s[step, :, 0:FF_CHUNK] = wuv_ref[...].astype(BF16)
        wup_s[step, :, FF_CHUNK:2 * FF_CHUNK] = wug_ref[...].astype(BF16)
        wdn_s[pl.ds(pl.multiple_of(step * FF_CHUNK, FF_CHUNK), FF_CHUNK), :] = wdn_ref[...].astype(BF16)

    @pl.when((i >= 0) & (i % tiles_per_seq == 0))
    def _():
        carry_ref[...] = jnp.zeros(carry_ref.shape, F32)

    @pl.when(i >= 0)
    def _():
        _ffn_tile(tm, gm_ref, hd_ref, gates_ref, x_ref, p_ref, wm_ref, wd_ref, wo_ref, gf_ref, cw_ref, cb_ref,
                  gp_ref, wpg_ref, wp_ref, gl_ref, out_ref, carry_ref, slab_ref, acc_ref, wup_s, wdn_s, final_norm)


def _ffn_tile(tm, gm_ref, hd_ref, gates_ref, x_ref, p_ref, wm_ref, wd_ref, wo_ref, gf_ref, cw_ref, cb_ref,
              gp_ref, wpg_ref, wp_ref, gl_ref, out_ref, carry_ref, slab_ref, acc_ref, wup_s, wdn_s, final_norm):
    ya = _dot(gm_ref[...], wm_ref[...])
    yb = _dot(hd_ref[...], wd_ref[...])
    ga = _sigmoid(gates_ref[:, :D_MODEL].astype(F32))
    gb = _sigmoid(gates_ref[:, D_MODEL:].astype(F32))
    x1 = x_ref[...] + _dot((ga * ya + gb * yb).astype(BF16), wo_ref[...])

    hb = _rms(x1, gf_ref[...]).astype(BF16)

    def slab0(c, half):
        return (c % FF_SLAB_SETS) * FF_SLABS + half * (FF_CHUNK // LANES)

    def up(c):
        _slab_store(_dot(hb, wup_s[c]), slab_ref, slab0(c, 0))

    def conv(c, half):
        return _slab_conv(tm, FF_CHUNK, half * D_FF + c * FF_CHUNK, slab_ref, slab0(c, half),
                          carry_ref, cw_ref, cb_ref)

    for c in range(FF_LOOKAHEAD):
        up(c)
    acts = []
    for c in range(N_FF_CHUNKS):
        if c + FF_LOOKAHEAD < N_FF_CHUNKS:
            up(c + FF_LOOKAHEAD)
        acts.append((_gelu_tanh(conv(c, 1)) * conv(c, 0)).astype(BF16))
        if len(acts) == FF_DOWN_GROUP or c == N_FF_CHUNKS - 1:
            first = c + 1 - len(acts)
            d = _dot(jnp.concatenate(acts, axis=1), wdn_s[first * FF_CHUNK:(c + 1) * FF_CHUNK, :])
            if first == 0:
                acc_ref[...] = d
            else:
                acc_ref[...] += d
            acts = []

    x2 = x1 + acc_ref[...]
    hg = _rms(x2, gp_ref[...]).astype(BF16)
    gate = _sigmoid(_dot(hg, wpg_ref[...]))
    pe = _dot(p_ref[...].astype(BF16), wp_ref[...])
    x3 = x2 + gate * pe
    out_ref[...] = _rms(x3, gl_ref[...]) if final_norm else x3


def _ffn(gm, hd, gates, x2, p2, wm, wd, wo, gf, wup, cw, cb, wdn, gp, wpg, wp, gl, seq, final_norm):
    T = x2.shape[0]
    tm = TOKEN_TILE
    n_stage = N_FF_CHUNKS
    row = lambda s: (jnp.maximum(s - n_stage, 0), 0)
    stage = lambda s: jnp.minimum(s, n_stage - 1)
    return pl.pallas_call(
        functools.partial(_ffn_kernel, tiles_per_seq=seq // tm, final_norm=final_norm),
        grid=(n_stage + T // tm,),
        in_specs=[pl.BlockSpec((tm, M_WIDTH), row), pl.BlockSpec((tm, DA_WIDTH), row),
                  pl.BlockSpec((tm, 2 * D_MODEL), row), pl.BlockSpec((tm, D_MODEL), row),
                  pl.BlockSpec((tm, PLE_DIM), row),
                  _const_spec((M_WIDTH, D_MODEL)), _const_spec((DA_WIDTH, D_MODEL)),
                  _const_spec((D_MODEL, D_MODEL)), _const_spec((1, D_MODEL)),
                  pl.BlockSpec((D_MODEL, FF_CHUNK), lambda s: (0, stage(s))),
                  pl.BlockSpec((D_MODEL, FF_CHUNK), lambda s: (0, n_stage + stage(s))),
                  _const_spec((FFN_CONV, 2 * D_FF)), _const_spec((1, 2 * D_FF)),
                  pl.BlockSpec((FF_CHUNK, D_MODEL), lambda s: (stage(s), 0)),
                  _const_spec((1, D_MODEL)), _const_spec((D_MODEL, D_MODEL)),
                  _const_spec((PLE_DIM, D_MODEL)), _const_spec((1, D_MODEL))],
        out_specs=pl.BlockSpec((tm, D_MODEL), row),
        out_shape=jax.ShapeDtypeStruct((T, D_MODEL), F32),
        scratch_shapes=[pltpu.VMEM((SUBLANES, 2 * D_FF), F32),
                        pltpu.VMEM((FF_SLAB_SETS * FF_SLABS, SUBLANES + tm, LANES), F32),
                        pltpu.VMEM((tm, D_MODEL), F32),
                        pltpu.VMEM((N_FF_CHUNKS, D_MODEL, 2 * FF_CHUNK), BF16),
                        pltpu.VMEM((D_FF, D_MODEL), BF16)],
        compiler_params=pltpu.CompilerParams(dimension_semantics=("arbitrary",), vmem_limit_bytes=VMEM_LIMIT),
        name="convffn",
    )(gm, hd, gates, x2, p2, wm, wd, wo, gf, wup, wup, cw, cb, wdn, gp, wpg, wp, gl)


def kernel(x, p, rel_bias, norm_mix_g, w_in, b_if, m_conv_w, m_conv_b, m_norm_g, da_lambda, da_norm_g,
           w_br_m, w_br_d, w_out, norm_ffn_g, w_up, ffn_conv_w, ffn_conv_b, w_down, norm_ple_g, w_ple_gate,
           w_ple, norm_final_g):
    batch, seq, _ = x.shape
    depth = w_in.shape[0]
    T = batch * seq
    assert seq % TOKEN_TILE == 0 and seq % M_CHUNK == 0 and seq % ATT_BLOCK == 0 and batch % M_BATCH == 0
    xt = x.reshape(T, D_MODEL)
    row = lambda v: v.reshape(1, -1).astype(F32)

    for l in range(depth):
        w_t = jnp.swapaxes(w_in[l].astype(F32), 0, 1)
        bif = jnp.concatenate([b_if[l].astype(F32), jnp.zeros((LANES - N_IF,), F32)]).reshape(1, LANES)

        mq, mk, mv, mo, dq, dk, dv, gates, ifg = _inproj(
            xt, row(norm_mix_g[l]), w_t, m_conv_w[l].astype(F32), row(m_conv_b[l]), bif, seq)

        gm = _mlstm(mq, mk, mv, mo, ifg, row(m_norm_g[l]), batch, seq)

        lam_init = 0.8 - 0.6 * math.exp(-0.3 * l)
        hd = _attention(rel_bias.astype(F32), da_lambda[l].astype(F32), dq, dk, dv, row(da_norm_g[l]),
                        batch, seq, lam_init)

        xt = _ffn(gm, hd, gates, xt, p[l].reshape(T, PLE_DIM),
                  w_br_m[l].astype(BF16), w_br_d[l].astype(BF16), w_out[l].astype(BF16),
                  row(norm_ffn_g[l]), w_up[l].astype(F32),
                  ffn_conv_w[l].astype(F32), row(ffn_conv_b[l]), w_down[l].astype(F32), row(norm_ple_g[l]),
                  w_ple_gate[l].astype(BF16), w_ple[l].astype(BF16), row(norm_final_g), seq,
                  final_norm=(l == depth - 1))

    return xt.reshape(batch, seq, D_MODEL)
```
